```python
import jax, jax.numpy as jnp
from jax import lax
import numpy as np

D_MODEL = 1024
BATCH = 4
SEQ = 4096
DEPTH = 1

MIX_WIDTH = 2 * D_MODEL
SSD_WIDTH = MIX_WIDTH // 2
GLA_V_WIDTH = MIX_WIDTH - SSD_WIDTH
SSD_HEAD_DIM = 64
SSD_HEADS = SSD_WIDTH // SSD_HEAD_DIM
SSD_GROUPS = 2
SSD_STATE = 128
SSD_CONV = 4
SSD_CHUNK = 128
SSD_CONV_DIM = SSD_WIDTH + 2 * SSD_GROUPS * SSD_STATE
GLA_HEADS = 4
GLA_K_WIDTH = GLA_V_WIDTH // 2
GLA_HEAD_K = GLA_K_WIDTH // GLA_HEADS
GLA_HEAD_V = GLA_V_WIDTH // GLA_HEADS
GLA_GATE_RANK = 16
GLA_GATE_NORM = 16.0
GLA_CHUNK = 64
FFN_HIDDEN = int(round(8 * D_MODEL / 3 / 128)) * 128
FFN_CONV = 3
ADA_CHUNKS = 6
NORM_EPS = 1e-6

kernel_name = "hymba_ssd_gla_convffn_block"


def rmsnorm(x, w, eps=NORM_EPS):
    xf = x.astype(jnp.float32)
    y = xf * lax.rsqrt(jnp.mean(xf * xf, axis=-1, keepdims=True) + eps)
    return (y * w.astype(jnp.float32)).astype(x.dtype)


def causal_dwconv(x, w, b):
    k = w.shape[0]
    y = lax.conv_general_dilated(
        x, w[:, None, :].astype(x.dtype), window_strides=(1,), padding=[(k - 1, 0)],
        dimension_numbers=("NWC", "WIO", "NWC"), feature_group_count=x.shape[-1])
    return y + b.astype(x.dtype)


def ssd_mixer(z, xbc, dt_raw, conv_w, conv_b, dt_bias, a_log, d_skip, norm_w):
    f32 = jnp.float32
    bsz, seqlen, _ = z.shape
    nc, Q = seqlen // SSD_CHUNK, SSD_CHUNK
    G, R, P, N = SSD_GROUPS, SSD_HEADS // SSD_GROUPS, SSD_HEAD_DIM, SSD_STATE
    xbc = jax.nn.silu(causal_dwconv(xbc, conv_w, conv_b)).astype(f32)
    xs, bm, cm = jnp.split(xbc, [SSD_WIDTH, SSD_WIDTH + G * N], axis=-1)
    xs = xs.reshape(bsz, nc, Q, G, R, P)
    bm = bm.reshape(bsz, nc, Q, G, N)
    cm = cm.reshape(bsz, nc, Q, G, N)
    dt = jax.nn.softplus(dt_raw.astype(f32) + dt_bias.astype(f32)).reshape(bsz, nc, Q, G, R)
    a = -jnp.exp(a_log.astype(f32)).reshape(G, R)
    cs = jnp.cumsum(dt * a, axis=2)
    xdt = xs * dt[..., None]
    causal = jnp.tril(jnp.ones((Q, Q), dtype=bool))[:, :, None, None]
    seg = cs[:, :, :, None] - cs[:, :, None, :]
    L = jnp.exp(jnp.where(causal, seg, -jnp.inf))
    scores = jnp.einsum("bclgn,bcsgn->bclsg", cm, bm)
    y_diag = jnp.einsum("bclsgr,bcsgrp->bclgrp", scores[..., None] * L, xdt)
    decay_states = jnp.exp(cs[:, :, -1:] - cs)
    states = jnp.einsum("bcsgn,bcsgrp->bcgrpn", bm, xdt * decay_states[..., None])
    chunk_decay = jnp.exp(cs[:, :, -1])

    def step(h, inp):
        st, dc = inp
        return dc[..., None, None] * h + st, h

    h0 = jnp.zeros((bsz, G, R, P, N), f32)
    _, prev = lax.scan(step, h0, (jnp.moveaxis(states, 1, 0), jnp.moveaxis(chunk_decay, 1, 0)))
    prev = jnp.moveaxis(prev, 0, 1)
    y_off = jnp.einsum("bclgn,bcgrpn->bclgrp", cm, prev) * jnp.exp(cs)[..., None]
    y = y_diag + y_off + xs * d_skip.astype(f32).reshape(G, R)[:, :, None]
    y = y.reshape(bsz, seqlen, SSD_WIDTH) * jax.nn.silu(z.astype(f32))
    y = y.reshape(bsz, seqlen, G, SSD_WIDTH // G)
    y = y * lax.rsqrt(jnp.mean(y * y, axis=-1, keepdims=True) + NORM_EPS)
    y = y.reshape(bsz, seqlen, SSD_WIDTH) * norm_w.astype(f32)
    return y.astype(z.dtype)


def gla_mixer(q, k, v, g_lr, g_out, gate_w, gate_b, norm_w):
    f32 = jnp.float32
    bsz, seqlen, _ = q.shape
    nc, C, H, DK, DV = seqlen // GLA_CHUNK, GLA_CHUNK, GLA_HEADS, GLA_HEAD_K, GLA_HEAD_V
    q = q.astype(f32).reshape(bsz, nc, C, H, DK) * (DK ** -0.5)
    k = k.astype(f32).reshape(bsz, nc, C, H, DK)
    v = v.astype(f32).reshape(bsz, nc, C, H, DV)
    logit = jnp.einsum("bsr,rk->bsk", g_lr.astype(f32), gate_w.astype(f32)) + gate_b.astype(f32)
    lg = (jax.nn.log_sigmoid(logit) / GLA_GATE_NORM).reshape(bsz, nc, C, H, DK)
    bcum = jnp.cumsum(lg, axis=2)
    qt = q * jnp.exp(bcum)
    kt = k * jnp.exp(-bcum)
    causal = jnp.tril(jnp.ones((C, C), dtype=bool))
    attn = jnp.where(causal, jnp.einsum("bnihd,bnjhd->bnhij", qt, kt), 0.0)
    o_intra = jnp.einsum("bnhij,bnjhv->bnihv", attn, v)
    blast = bcum[:, :, -1]
    u = jnp.einsum("bnjhd,bnjhv->bnhdv", k * jnp.exp(blast[:, :, None] - bcum), v)

    def step(s, inp):
        un, dn = inp
        return jnp.exp(dn)[..., None] * s + un, s

    s0 = jnp.zeros((bsz, H, DK, DV), f32)
    _, s_prev = lax.scan(step, s0, (jnp.moveaxis(u, 1, 0), jnp.moveaxis(blast, 1, 0)))
    s_prev = jnp.moveaxis(s_prev, 0, 1)
    o = o_intra + jnp.einsum("bnihd,bnhdv->bnihv", qt, s_prev)
    o = o.reshape(bsz, seqlen, H, DV)
    o = o * lax.rsqrt(jnp.mean(o * o, axis=-1, keepdims=True) + NORM_EPS) * norm_w.astype(f32)
    o = o * jax.nn.silu(g_out.astype(f32).reshape(bsz, seqlen, H, DV))
    return o.reshape(bsz, seqlen, H * DV).astype(g_out.dtype)


def conv_ffn(h, w_up, conv_w, conv_b, w_down):
    u = causal_dwconv(h @ w_up, conv_w, conv_b)
    a, b = jnp.split(u, 2, axis=-1)
    return (jax.nn.silu(a) * b) @ w_down


def setup_inputs(seed: int = 0) -> dict:
    key = jax.random.key(seed)
    ks = jax.random.split(key, 24)
    f32 = jnp.float32
    n_in = SSD_WIDTH + SSD_CONV_DIM + SSD_HEADS + 2 * GLA_K_WIDTH + GLA_V_WIDTH + GLA_GATE_RANK + GLA_V_WIDTH

    def nrm(k, shape, scale):
        return jax.random.normal(k, shape, f32) * scale

    def gain(k, shape):
        return 1.0 + 0.02 * jax.random.normal(k, shape, f32)

    dt = jnp.exp(jax.random.uniform(ks[10], (DEPTH, SSD_HEADS), f32, np.log(1e-3), np.log(1e-1)))
    dt_bias = dt + jnp.log(-jnp.expm1(-dt))
    a_log = jnp.log(jax.random.uniform(ks[11], (DEPTH, SSD_HEADS), f32, 1.0, 16.0))
    return {
        "x": nrm(ks[0], (BATCH, SEQ, D_MODEL), 1.0),
        "c": nrm(ks[1], (BATCH, D_MODEL), 1.0),
        "w_ada": nrm(ks[2], (DEPTH, D_MODEL, ADA_CHUNKS * D_MODEL), 0.5 * D_MODEL ** -0.5),
        "b_ada": nrm(ks[3], (DEPTH, ADA_CHUNKS * D_MODEL), 0.02),
        "norm_mix_pre": gain(ks[4], (DEPTH, D_MODEL)),
        "norm_mix_post": gain(ks[5], (DEPTH, D_MODEL)),
        "norm_ffn_pre": gain(ks[6], (DEPTH, D_MODEL)),
        "norm_ffn_post": gain(ks[7], (DEPTH, D_MODEL)),
        "w_in": nrm(ks[8], (DEPTH, D_MODEL, n_in), D_MODEL ** -0.5),
        "ssd_conv_w": nrm(ks[9], (DEPTH, SSD_CONV, SSD_CONV_DIM), SSD_CONV ** -0.5),
        "ssd_conv_b": nrm(ks[12], (DEPTH, SSD_CONV_DIM), 0.02),
        "ssd_dt_bias": dt_bias,
        "ssd_a_log": a_log,
        "ssd_d": gain(ks[13], (DEPTH, SSD_HEADS)),
        "ssd_norm": gain(ks[14], (DEPTH, SSD_WIDTH)),
        "gla_gate_w": nrm(ks[15], (DEPTH, GLA_GATE_RANK, GLA_K_WIDTH), GLA_GATE_RANK ** -0.5),
        "gla_gate_b": nrm(ks[16], (DEPTH, GLA_K_WIDTH), 0.02),
        "gla_norm": gain(ks[17], (DEPTH, GLA_HEAD_V)),
        "w_out": nrm(ks[18], (DEPTH, MIX_WIDTH, D_MODEL), MIX_WIDTH ** -0.5),
        "ffn_up": nrm(ks[19], (DEPTH, D_MODEL, 2 * FFN_HIDDEN), D_MODEL ** -0.5),
        "ffn_conv_w": nrm(ks[20], (DEPTH, FFN_CONV, 2 * FFN_HIDDEN), FFN_CONV ** -0.5),
        "ffn_conv_b": nrm(ks[21], (DEPTH, 2 * FFN_HIDDEN), 0.02),
        "ffn_down": nrm(ks[22], (DEPTH, FFN_HIDDEN, D_MODEL), FFN_HIDDEN ** -0.5),
    }


def reference(x, c, w_ada, b_ada, norm_mix_pre, norm_mix_post, norm_ffn_pre, norm_ffn_post,
              w_in, ssd_conv_w, ssd_conv_b, ssd_dt_bias, ssd_a_log, ssd_d, ssd_norm,
              gla_gate_w, gla_gate_b, gla_norm, w_out, ffn_up, ffn_conv_w, ffn_conv_b, ffn_down):
    sizes = [SSD_WIDTH, SSD_CONV_DIM, SSD_HEADS, GLA_K_WIDTH, GLA_K_WIDTH, GLA_V_WIDTH,
             GLA_GATE_RANK, GLA_V_WIDTH]
    split_idx = np.cumsum(sizes)[:-1].tolist()
    c_act = jax.nn.silu(c)
    for i in range(DEPTH):
        mod = (c_act @ w_ada[i] + b_ada[i])[:, None, :]
        shift1, scale1, gate1, shift2, scale2, gate2 = jnp.split(mod, ADA_CHUNKS, axis=-1)
        h = rmsnorm(x, norm_mix_pre[i]) * (1 + scale1) + shift1
        proj = h @ w_in[i]
        z, xbc, dt_raw, q, k, v, g_lr, g_out = jnp.split(proj, split_idx, axis=-1)
        y_ssd = ssd_mixer(z, xbc, dt_raw, ssd_conv_w[i], ssd_conv_b[i], ssd_dt_bias[i],
                          ssd_a_log[i], ssd_d[i], ssd_norm[i])
        y_gla = gla_mixer(q, k, v, g_lr, g_out, gla_gate_w[i], gla_gate_b[i], gla_norm[i])
        y = jnp.concatenate([y_ssd, y_gla], axis=-1) @ w_out[i]
        x = x + gate1 * rmsnorm(y, norm_mix_post[i])
        h = rmsnorm(x, norm_ffn_pre[i]) * (1 + scale2) + shift2
        f = conv_ffn(h, ffn_up[i], ffn_conv_w[i], ffn_conv_b[i], ffn_down[i])
        x = x + gate2 * rmsnorm(f, norm_ffn_post[i])
    return x
```

```python
import functools

import numpy as np
import jax
import jax.numpy as jnp
from jax import lax
from jax.experimental import pallas as pl
from jax.experimental.pallas import tpu as pltpu

F32 = jnp.float32
BF16 = jnp.bfloat16

D_MODEL = 1024
MIX_WIDTH = 2 * D_MODEL
SSD_WIDTH = MIX_WIDTH // 2
GLA_V_WIDTH = MIX_WIDTH - SSD_WIDTH
SSD_HEAD_DIM = 64
SSD_HEADS = SSD_WIDTH // SSD_HEAD_DIM
SSD_GROUPS = 2
SSD_STATE = 128
SSD_CONV = 4
SSD_CHUNK = 128
SSD_CONV_DIM = SSD_WIDTH + 2 * SSD_GROUPS * SSD_STATE
GLA_HEADS = 4
GLA_K_WIDTH = GLA_V_WIDTH // 2
GLA_HEAD_K = GLA_K_WIDTH // GLA_HEADS
GLA_HEAD_V = GLA_V_WIDTH // GLA_HEADS
GLA_GATE_RANK = 16
GLA_GATE_NORM = 16.0
GLA_CHUNK = 64
FFN_HIDDEN = int(round(8 * D_MODEL / 3 / 128)) * 128
FFN_CONV = 3
ADA_CHUNKS = 6
NORM_EPS = 1e-6

LANES = 128
SUBLANES = 8
VMEM_LIMIT_BYTES = 56 * 1024 * 1024

SMALL_W = LANES
GROUP_W = SSD_WIDTH // SSD_GROUPS
HEADS_PER_GROUP = SSD_HEADS // SSD_GROUPS
HALO = SUBLANES


def _silu(x):
    return x * jax.nn.sigmoid(x)


def _softplus(x):
    return jnp.maximum(x, 0.0) + jnp.log1p(jnp.exp(-jnp.abs(x)))


def _rms_scale(x):
    return lax.rsqrt(jnp.mean(x * x, axis=-1, keepdims=True) + NORM_EPS)


def _split3(x):
    hi = x.astype(BF16)
    r1 = x - hi.astype(F32)
    mid = r1.astype(BF16)
    lo = (r1 - mid.astype(F32)).astype(BF16)
    return hi, mid, lo


def _dot(a, b):
    return jnp.dot(a, b, preferred_element_type=F32)


def _dot_nt(a, b):
    return lax.dot_general(a, b, (((1,), (1,)), ((), ())), preferred_element_type=F32)


def _dot_tn(a, b):
    return lax.dot_general(a, b, (((0,), (0,)), ((), ())), preferred_element_type=F32)


def _resident(shape):
    nd = len(shape)
    return pl.BlockSpec(shape, lambda *_: (0,) * nd, pipeline_mode=pl.Buffered(1))


def _params(*sem):
    return pltpu.CompilerParams(dimension_semantics=sem, vmem_limit_bytes=VMEM_LIMIT_BYTES)


def _ada_kernel(c_ref, w_ref, b_ref, o_ref):
    ca = _silu(c_ref[...]).astype(BF16)
    o_ref[...] = _dot(ca, w_ref[...].astype(BF16)) + b_ref[...]


def _ada_call(c_pad, w, b):
    rows, d = c_pad.shape
    n = w.shape[1]
    tn = 1024
    return pl.pallas_call(
        _ada_kernel,
        grid=(n // tn,),
        in_specs=[
            pl.BlockSpec((rows, d), lambda j: (0, 0)),
            pl.BlockSpec((d, tn), lambda j: (0, j)),
            pl.BlockSpec((1, tn), lambda j: (0, j)),
        ],
        out_specs=pl.BlockSpec((rows, tn), lambda j: (0, j)),
        out_shape=jax.ShapeDtypeStruct((rows, n), F32),
        compiler_params=_params("arbitrary"),
        name="ada_mod",
    )(c_pad, w, b)


_INPROJ_SEGS = (SSD_WIDTH, SSD_CONV_DIM, 2 * GLA_K_WIDTH, GLA_V_WIDTH, GLA_V_WIDTH)
_INPROJ_BIG = sum(_INPROJ_SEGS)
_INPROJ_CHUNK = 512


def _inproj_kernel(x_ref, mod_ref, nw_ref, wbig_ref, wsmall_ref,
                   z_ref, xbc_ref, qk_ref, v_ref, go_ref, small_ref):
    x = x_ref[...]
    xn = (x * _rms_scale(x)) * nw_ref[...]
    h = xn * (1.0 + mod_ref[1:2, :]) + mod_ref[0:1, :]
    hb = h.astype(BF16)
    base = 0
    for ref, width in zip((z_ref, xbc_ref, qk_ref, v_ref, go_ref), _INPROJ_SEGS):
        for c in range(0, width, _INPROJ_CHUNK):
            ref[:, c:c + _INPROJ_CHUNK] = _dot(
                hb, wbig_ref[:, base + c:base + c + _INPROJ_CHUNK]).astype(BF16)
        base += width
    small_ref[...] = _dot(hb, wsmall_ref[...])


def _inproj_call(x, mod, nw, wbig, wsmall, tm):
    b, s, d = x.shape
    row = lambda w: pl.BlockSpec((None, tm, w), lambda i, j: (i, j, 0))
    out_shapes = [jax.ShapeDtypeStruct((b, s, w), BF16) for w in _INPROJ_SEGS]
    out_shapes.append(jax.ShapeDtypeStruct((b, s, SMALL_W), F32))
    return pl.pallas_call(
        _inproj_kernel,
        grid=(b, s // tm),
        in_specs=[
            row(d),
            pl.BlockSpec((None, ADA_CHUNKS, d), lambda i, j: (i, 0, 0)),
            _resident((1, d)),
            _resident(wbig.shape),
            _resident(wsmall.shape),
        ],
        out_specs=[row(w) for w in _INPROJ_SEGS] + [row(SMALL_W)],
        out_shape=out_shapes,
        compiler_params=_params("arbitrary", "arbitrary"),
        name="inproj",
    )(x, mod, nw, wbig, wsmall)


def _ssd_kernel(z_ref, xbc_ref, small_ref, cw_ref, cb_ref, dtb_ref, alog_ref, dexp_ref,
                nw_ref, tri_ref, exp_ref, y_ref, xpad, state):
    q = SSD_CHUNK
    n = SSD_STATE

    @pl.when(pl.program_id(1) == 0)
    def _():
        xpad[0:HALO, :] = jnp.zeros((HALO, SSD_CONV_DIM), F32)
        state[...] = jnp.zeros(state.shape, F32)

    xbc = xbc_ref[...].astype(F32)
    xpad[HALO:HALO + q, :] = xbc
    acc = cb_ref[...]
    for k in range(SSD_CONV):
        off = HALO - (SSD_CONV - 1) + k
        acc = acc + cw_ref[k:k + 1, :] * xpad[off:off + q, :]
    xpad[0:HALO, :] = xbc[q - HALO:q, :]
    xc = _silu(acc)
    xs = xc[:, :SSD_WIDTH]
    bm = xc[:, SSD_WIDTH:SSD_WIDTH + SSD_GROUPS * n].astype(BF16)
    cm = xc[:, SSD_WIDTH + SSD_GROUPS * n:].astype(BF16)

    lane = lax.broadcasted_iota(jnp.int32, (q, SMALL_W), 1)
    head_lane = lane < SSD_HEADS
    dt = _softplus(small_ref[...] + dtb_ref[...])
    a = -jnp.exp(alog_ref[...])
    dt = jnp.where(head_lane, dt, 0.0)
    da = jnp.where(head_lane, dt * a, 0.0)
    cs = _dot(tri_ref[...], jnp.concatenate(_split3(da), axis=0))
    cs_t = cs.T
    cs_e = _dot(jnp.concatenate(_split3(cs), axis=1), exp_ref[...])
    dt_e = _dot(jnp.concatenate(_split3(dt), axis=1), exp_ref[...])
    cs_last = cs_e[q - 1:q, :]

    xdt = xs * dt_e
    xdt_b = xdt.astype(BF16)
    xdec_b = (xdt * jnp.exp(cs_last - cs_e)).astype(BF16)
    decay_in = jnp.exp(cs_e)
    decay_chunk = jnp.exp(cs_last)

    row = lax.broadcasted_iota(jnp.int32, (q, q), 0)
    col = lax.broadcasted_iota(jnp.int32, (q, q), 1)
    causal = row >= col
    first_half = col < SSD_HEAD_DIM

    y_parts = []
    for g in range(SSD_GROUPS):
        cmg = cm[:, g * n:(g + 1) * n]
        bmg = bm[:, g * n:(g + 1) * n]
        gs = slice(g * GROUP_W, (g + 1) * GROUP_W)
        scores = _dot_nt(cmg, bmg)
        y_off = _dot(cmg, state[g].astype(BF16)) * decay_in[:, gs]
        state[g] = decay_chunk[:, gs] * state[g] + _dot_tn(bmg, xdec_b[:, gs])
        diag = []
        for pair in range(HEADS_PER_GROUP // 2):
            h0 = g * HEADS_PER_GROUP + 2 * pair
            ms = []
            for h in (h0, h0 + 1):
                seg = cs[:, h:h + 1] - cs_t[h:h + 1, :]
                decay = jnp.exp(jnp.where(causal, seg, -jnp.inf))
                ms.append((scores * decay).astype(BF16))
            slab = xdt_b[:, h0 * SSD_HEAD_DIM:(h0 + 2) * SSD_HEAD_DIM]
            zero = jnp.zeros_like(slab)
            rhs = jnp.concatenate([jnp.where(first_half, slab, zero),
                                   jnp.where(first_half, zero, slab)], axis=0)
            diag.append(_dot(jnp.concatenate(ms, axis=1), rhs))
        y_parts.append(jnp.concatenate(diag, axis=1) + y_off)
    y = jnp.concatenate(y_parts, axis=1) + xs * dexp_ref[...]
    y = y * _silu(z_ref[...].astype(F32))
    outs = []
    for g in range(SSD_GROUPS):
        yg = y[:, g * GROUP_W:(g + 1) * GROUP_W]
        outs.append(yg * _rms_scale(yg))
    y_ref[...] = (jnp.concatenate(outs, axis=1) * nw_ref[...]).astype(BF16)


def _ssd_call(z, xbc, small, cw, cb, dtb, alog, dexp, nw, tri3, exp3):
    b, s, _ = z.shape
    q = SSD_CHUNK
    row = lambda w: pl.BlockSpec((None, q, w), lambda i, j: (i, j, 0))
    consts = (cw, cb, dtb, alog, dexp, nw, tri3, exp3)
    return pl.pallas_call(
        _ssd_kernel,
        grid=(b, s // q),
        in_specs=[row(SSD_WIDTH), row(SSD_CONV_DIM), row(SMALL_W)] + [_resident(a.shape) for a in consts],
        out_specs=row(SSD_WIDTH),
        out_shape=jax.ShapeDtypeStruct((b, s, SSD_WIDTH), BF16),
        scratch_shapes=[
            pltpu.VMEM((HALO + q, SSD_CONV_DIM), F32),
            pltpu.VMEM((SSD_GROUPS, SSD_STATE, GROUP_W), F32),
        ],
        compiler_params=_params("arbitrary", "arbitrary"),
        name="ssd_mixer",
    )(z, xbc, small, *consts)


_GLA_TILE = 2 * GLA_CHUNK


def _gla_kernel(qk_ref, v_ref, go_ref, small_ref, gw_ref, gb_ref, nw_ref, tri_ref, y_ref, state):
    c = GLA_CHUNK
    dk, dv = GLA_HEAD_K, GLA_HEAD_V

    @pl.when(pl.program_id(1) == 0)
    def _():
        state[...] = jnp.zeros(state.shape, F32)

    row = lax.broadcasted_iota(jnp.int32, (c, c), 0)
    col = lax.broadcasted_iota(jnp.int32, (c, c), 1)
    causal = row >= col
    for ci in range(_GLA_TILE // c):
        r = slice(ci * c, (ci + 1) * c)
        qf = qk_ref[r, 0:GLA_K_WIDTH].astype(F32) * (dk ** -0.5)
        kf = qk_ref[r, GLA_K_WIDTH:2 * GLA_K_WIDTH].astype(F32)
        logit = _dot(small_ref[r, :].astype(BF16), gw_ref[...]) + gb_ref[...]
        lg = -_softplus(-logit) / GLA_GATE_NORM
        bcum = _dot(tri_ref[...], jnp.concatenate(_split3(lg), axis=0))
        blast = bcum[c - 1:c, :]
        qt = (qf * jnp.exp(bcum)).astype(BF16)
        kt = (kf * jnp.exp(-bcum)).astype(BF16)
        kd = (kf * jnp.exp(blast - bcum)).astype(BF16)
        eblast = jnp.exp(blast)
        for h in range(GLA_HEADS):
            ks = slice(h * dk, (h + 1) * dk)
            vs = slice(h * dv, (h + 1) * dv)
            attn = jnp.where(causal, _dot_nt(qt[:, ks], kt[:, ks]), 0.0)
            vh = v_ref[r, vs]
            o = _dot(attn.astype(BF16), vh) + _dot_nt(qt[:, ks], state[h].astype(BF16))
            state[h] = eblast[:, ks] * state[h] + _dot_tn(vh, kd[:, ks])
            o = o * _rms_scale(o) * nw_ref[...]
            o = o * _silu(go_ref[r, vs].astype(F32))
            y_ref[r, vs] = o.astype(BF16)


def _gla_call(qk, v, go, small, gw, gb, nw, tri3):
    b, s, _ = v.shape
    t = _GLA_TILE
    row = lambda w: pl.BlockSpec((None, t, w), lambda i, j: (i, j, 0))
    consts = (gw, gb, nw, tri3)
    return pl.pallas_call(
        _gla_kernel,
        grid=(b, s // t),
        in_specs=[row(2 * GLA_K_WIDTH), row(GLA_V_WIDTH), row(GLA_V_WIDTH), row(SMALL_W)]
        + [_resident(a.shape) for a in consts],
        out_specs=row(GLA_V_WIDTH),
        out_shape=jax.ShapeDtypeStruct((b, s, GLA_V_WIDTH), BF16),
        scratch_shapes=[pltpu.VMEM((GLA_HEADS, GLA_HEAD_V, GLA_HEAD_K), F32)],
        compiler_params=_params("arbitrary", "arbitrary"),
        name="gla_mixer",
    )(qk, v, go, small, *consts)


_UP_CHUNK = 768


def _outproj_kernel(ys_ref, yg_ref, x_ref, mod_ref, npost_ref, npre_ref, wout_ref, wup_ref,
                    x1_ref, u_ref):
    y = _dot(ys_ref[...], wout_ref[0:SSD_WIDTH, :]) + _dot(yg_ref[...], wout_ref[SSD_WIDTH:, :])
    yn = (y * _rms_scale(y)) * npost_ref[...]
    x1 = x_ref[...] + mod_ref[2:3, :] * yn
    x1_ref[...] = x1
    h = ((x1 * _rms_scale(x1)) * npre_ref[...]) * (1.0 + mod_ref[4:5, :]) + mod_ref[3:4, :]
    hb = h.astype(BF16)
    for c in range(0, 2 * FFN_HIDDEN, _UP_CHUNK):
        u_ref[:, c:c + _UP_CHUNK] = _dot(hb, wup_ref[:, c:c + _UP_CHUNK]).astype(BF16)


def _outproj_call(ys, yg, x, mod, npost, npre, wout, wup, tm):
    b, s, d = x.shape
    row = lambda w: pl.BlockSpec((None, tm, w), lambda i, j: (i, j, 0))
    return pl.pallas_call(
        _outproj_kernel,
        grid=(b, s // tm),
        in_specs=[
            row(SSD_WIDTH), row(GLA_V_WIDTH), row(d),
            pl.BlockSpec((None, ADA_CHUNKS, d), lambda i, j: (i, 0, 0)),
            _resident((1, d)), _resident((1, d)),
            _resident(wout.shape), _resident(wup.shape),
        ],
        out_specs=[row(d), row(2 * FFN_HIDDEN)],
        out_shape=[jax.ShapeDtypeStruct((b, s, d), F32),
                   jax.ShapeDtypeStruct((b, s, 2 * FFN_HIDDEN), BF16)],
        compiler_params=_params("arbitrary", "arbitrary"),
        name="outproj_ffn_up",
    )(ys, yg, x, mod, npost, npre, wout, wup)


_GATE_CHUNK = 384


def _ffn_down_kernel(u_ref, halo_ref, x1_ref, mod_ref, cw_ref, cb_ref, npost_ref, wdown_ref,
                     o_ref, upad, gate):
    tm = u_ref.shape[0]
    halo = halo_ref[...].astype(F32)
    upad[0:HALO, :] = jnp.where(pl.program_id(1) == 0, jnp.zeros_like(halo), halo)
    upad[HALO:HALO + tm, :] = u_ref[...].astype(F32)

    def conv(c0):
        cols = slice(c0, c0 + _GATE_CHUNK)
        acc = cb_ref[:, cols]
        for k in range(FFN_CONV):
            off = HALO - (FFN_CONV - 1) + k
            acc = acc + cw_ref[k:k + 1, cols] * upad[off:off + tm, cols]
        return acc

    for c in range(0, FFN_HIDDEN, _GATE_CHUNK):
        gate[:, c:c + _GATE_CHUNK] = (_silu(conv(c)) * conv(FFN_HIDDEN + c)).astype(BF16)
    f = _dot(gate[...], wdown_ref[...])
    fn = (f * _rms_scale(f)) * npost_ref[...]
    o_ref[...] = x1_ref[...] + mod_ref[5:6, :] * fn


def _ffn_down_call(u, x1, mod, cw, cb, npost, wdown, tm):
    b, s, d = x1.shape
    w2 = 2 * FFN_HIDDEN
    row = lambda w: pl.BlockSpec((None, tm, w), lambda i, j: (i, j, 0))
    halo_blocks = tm // HALO
    return pl.pallas_call(
        _ffn_down_kernel,
        grid=(b, s // tm),
        in_specs=[
            row(w2),
            pl.BlockSpec((None, HALO, w2), lambda i, j: (i, jnp.maximum(j * halo_blocks - 1, 0), 0)),
            row(d),
            pl.BlockSpec((None, ADA_CHUNKS, d), lambda i, j: (i, 0, 0)),
            _resident(cw.shape), _resident(cb.shape), _resident((1, d)), _resident(wdown.shape),
        ],
        out_specs=row(d),
        out_shape=jax.ShapeDtypeStruct((b, s, d), F32),
        scratch_shapes=[pltpu.VMEM((HALO + tm, w2), F32), pltpu.VMEM((tm, FFN_HIDDEN), BF16)],
        compiler_params=_params("arbitrary", "arbitrary"),
        name="ffn_down",
    )(u, u, x1, mod, cw, cb, npost, wdown)


def _lane_pad(v, width):
    return jnp.pad(v, ((0, 0), (0, width - v.shape[1])))


def _constants():
    q, c = SSD_CHUNK, GLA_CHUNK
    tri_q = np.tril(np.ones((q, q), np.float32))
    tri_c = np.tril(np.ones((c, c), np.float32))
    expand = np.zeros((SMALL_W, SSD_WIDTH), np.float32)
    for h in range(SSD_HEADS):
        expand[h, h * SSD_HEAD_DIM:(h + 1) * SSD_HEAD_DIM] = 1.0
    return (jnp.asarray(np.tile(tri_q, (1, 3)), BF16),
            jnp.asarray(np.tile(tri_c, (1, 3)), BF16),
            jnp.asarray(np.tile(expand, (3, 1)), BF16))


def kernel(x, c, w_ada, b_ada, norm_mix_pre, norm_mix_post, norm_ffn_pre, norm_ffn_post, w_in, ssd_conv_w, ssd_conv_b, ssd_dt_bias, ssd_a_log, ssd_d, ssd_norm, gla_gate_w, gla_gate_b, gla_norm, w_out, ffn_up, ffn_conv_w, ffn_conv_b, ffn_down):
    bsz, seqlen, d = x.shape
    depth = w_ada.shape[0]
    tri_q3, tri_c3, expand3 = _constants()
    c_pad = jnp.pad(c, ((0, SUBLANES - bsz % SUBLANES), (0, 0))) if bsz % SUBLANES else c

    sizes = [SSD_WIDTH, SSD_CONV_DIM, SSD_HEADS, GLA_K_WIDTH, GLA_K_WIDTH, GLA_V_WIDTH,
             GLA_GATE_RANK, GLA_V_WIDTH]
    o = np.concatenate([[0], np.cumsum(sizes)])

    for i in range(depth):
        mod = _ada_call(c_pad, w_ada[i], b_ada[i][None, :])[:bsz].reshape(bsz, ADA_CHUNKS, d)

        wi = w_in[i]
        wbig = jnp.concatenate([wi[:, o[0]:o[2]], wi[:, o[3]:o[6]], wi[:, o[7]:o[8]]], axis=1).astype(BF16)
        wsmall = jnp.concatenate(
            [wi[:, o[2]:o[3]], wi[:, o[6]:o[7]],
             jnp.zeros((d, SMALL_W - SSD_HEADS - GLA_GATE_RANK), wi.dtype)], axis=1).astype(BF16)
        z, xbc, qk, v, go, small = _inproj_call(x, mod, norm_mix_pre[i][None, :], wbig, wsmall, tm=256)

        y_ssd = _ssd_call(
            z, xbc, small, ssd_conv_w[i], ssd_conv_b[i][None, :],
            _lane_pad(ssd_dt_bias[i][None, :], SMALL_W), _lane_pad(ssd_a_log[i][None, :], SMALL_W),
            jnp.repeat(ssd_d[i], SSD_HEAD_DIM)[None, :], ssd_norm[i][None, :], tri_q3, expand3)

        gw = jnp.zeros((SMALL_W, GLA_K_WIDTH), F32).at[SSD_HEADS:SSD_HEADS + GLA_GATE_RANK].set(
            gla_gate_w[i]).astype(BF16)
        y_gla = _gla_call(qk, v, go, small, gw, gla_gate_b[i][None, :], gla_norm[i][None, :], tri_c3)

        x1, u = _outproj_call(y_ssd, y_gla, x, mod, norm_mix_post[i][None, :], norm_ffn_pre[i][None, :],
                              w_out[i].astype(BF16), ffn_up[i].astype(BF16), tm=256)
        x = _ffn_down_call(u, x1, mod, ffn_conv_w[i], ffn_conv_b[i][None, :],
                           norm_ffn_post[i][None, :], ffn_down[i].astype(BF16), tm=256)
    return x
```

```python
import functools

import numpy as np
import jax
import jax.numpy as jnp
from jax import lax
from jax.experimental import pallas as pl
from jax.experimental.pallas import tpu as pltpu

F32 = jnp.float32
BF16 = jnp.bfloat16

D_MODEL = 1024
MIX_WIDTH = 2 * D_MODEL
SSD_WIDTH = MIX_WIDTH // 2
GLA_V_WIDTH = MIX_WIDTH - SSD_WIDTH
SSD_HEAD_DIM = 64
SSD_HEADS = SSD_WIDTH // SSD_HEAD_DIM
SSD_GROUPS = 2
SSD_STATE = 128
SSD_CONV = 4
SSD_CHUNK = 128
SSD_CONV_DIM = SSD_WIDTH + 2 * SSD_GROUPS * SSD_STATE
GLA_HEADS = 4
GLA_K_WIDTH = GLA_V_WIDTH // 2
GLA_HEAD_K = GLA_K_WIDTH // GLA_HEADS
GLA_HEAD_V = GLA_V_WIDTH // GLA_HEADS
GLA_GATE_RANK = 16
GLA_GATE_NORM = 16.0
GLA_CHUNK = 64
FFN_HIDDEN = int(round(8 * D_MODEL / 3 / 128)) * 128
FFN_CONV = 3
ADA_CHUNKS = 6
NORM_EPS = 1e-6

LANES = 128
SUBLANES = 8
VMEM_LIMIT_BYTES = 56 * 1024 * 1024

SMALL_W = LANES
GROUP_W = SSD_WIDTH // SSD_GROUPS
HEADS_PER_GROUP = SSD_HEADS // SSD_GROUPS
HALO = SUBLANES


def _silu(x):
    return x * jax.nn.sigmoid(x)


def _softplus(x):
    return jnp.maximum(x, 0.0) + jnp.log1p(jnp.exp(-jnp.abs(x)))


def _rms_scale(x):
    return lax.rsqrt(jnp.mean(x * x, axis=-1, keepdims=True) + NORM_EPS)


def _split3(x):
    hi = x.astype(BF16)
    r1 = x - hi.astype(F32)
    mid = r1.astype(BF16)
    lo = (r1 - mid.astype(F32)).astype(BF16)
    return hi, mid, lo


def _dot(a, b):
    return jnp.dot(a, b, preferred_element_type=F32)


def _dot_nt(a, b):
    return lax.dot_general(a, b, (((1,), (1,)), ((), ())), preferred_element_type=F32)


def _dot_tn(a, b):
    return lax.dot_general(a, b, (((0,), (0,)), ((), ())), preferred_element_type=F32)


def _resident(shape):
    nd = len(shape)
    return pl.BlockSpec(shape, lambda *_: (0,) * nd, pipeline_mode=pl.Buffered(1))


def _params(*sem):
    return pltpu.CompilerParams(dimension_semantics=sem, vmem_limit_bytes=VMEM_LIMIT_BYTES)


def _ada_kernel(c_ref, w_ref, b_ref, o_ref):
    ca = _silu(c_ref[...]).astype(BF16)
    o_ref[...] = _dot(ca, w_ref[...].astype(BF16)) + b_ref[...]


def _ada_call(c_pad, w, b):
    rows, d = c_pad.shape
    n = w.shape[1]
    tn = 1024
    return pl.pallas_call(
        _ada_kernel,
        grid=(n // tn,),
        in_specs=[
            pl.BlockSpec((rows, d), lambda j: (0, 0)),
            pl.BlockSpec((d, tn), lambda j: (0, j)),
            pl.BlockSpec((1, tn), lambda j: (0, j)),
        ],
        out_specs=pl.BlockSpec((rows, tn), lambda j: (0, j)),
        out_shape=jax.ShapeDtypeStruct((rows, n), F32),
        compiler_params=_params("arbitrary"),
        name="ada_mod",
    )(c_pad, w, b)


_INPROJ_SEGS = (SSD_WIDTH, SSD_CONV_DIM, 2 * GLA_K_WIDTH, GLA_V_WIDTH, GLA_V_WIDTH)
_INPROJ_BIG = sum(_INPROJ_SEGS)
_INPROJ_CHUNK = 512


def _inproj_kernel(x_ref, mod_ref, nw_ref, wbig_ref, wsmall_ref,
                   z_ref, xbc_ref, qk_ref, v_ref, go_ref, small_ref):
    x = x_ref[...]
    xn = (x * _rms_scale(x)) * nw_ref[...]
    h = xn * (1.0 + mod_ref[1:2, :]) + mod_ref[0:1, :]
    hb = h.astype(BF16)
    base = 0
    for ref, width in zip((z_ref, xbc_ref, qk_ref, v_ref, go_ref), _INPROJ_SEGS):
        for c in range(0, width, _INPROJ_CHUNK):
            ref[:, c:c + _INPROJ_CHUNK] = _dot(
                hb, wbig_ref[:, base + c:base + c + _INPROJ_CHUNK]).astype(BF16)
        base += width
    small_ref[...] = _dot(hb, wsmall_ref[...])


def _inproj_call(x, mod, nw, wbig, wsmall, tm):
    b, s, d = x.shape
    row = lambda w: pl.BlockSpec((None, tm, w), lambda i, j: (i, j, 0))
    out_shapes = [jax.ShapeDtypeStruct((b, s, w), BF16) for w in _INPROJ_SEGS]
    out_shapes.append(jax.ShapeDtypeStruct((b, s, SMALL_W), F32))
    return pl.pallas_call(
        _inproj_kernel,
        grid=(b, s // tm),
        in_specs=[
            row(d),
            pl.BlockSpec((None, ADA_CHUNKS, d), lambda i, j: (i, 0, 0)),
            _resident((1, d)),
            _resident(wbig.shape),
            _resident(wsmall.shape),
        ],
        out_specs=[row(w) for w in _INPROJ_SEGS] + [row(SMALL_W)],
        out_shape=out_shapes,
        compiler_params=_params("arbitrary", "arbitrary"),
        name="inproj",
    )(x, mod, nw, wbig, wsmall)


def _ssd_kernel(z_ref, xbc_ref, small_ref, cw_ref, cb_ref, dtb_ref, alog_ref, dexp_ref,
                nw_ref, tri_ref, exp_ref, y_ref, xpad, state):
    q = SSD_CHUNK
    n = SSD_STATE

    @pl.when(pl.program_id(1) == 0)
    def _():
        xpad[0:HALO, :] = jnp.zeros((HALO, SSD_CONV_DIM), F32)
        state[...] = jnp.zeros(state.shape, F32)

    xbc = xbc_ref[...].astype(F32)
    xpad[HALO:HALO + q, :] = xbc
    acc = cb_ref[...]
    for k in range(SSD_CONV):
        off = HALO - (SSD_CONV - 1) + k
        acc = acc + cw_ref[k:k + 1, :] * xpad[off:off + q, :]
    xpad[0:HALO, :] = xbc[q - HALO:q, :]
    xc = _silu(acc)
    xs = xc[:, :SSD_WIDTH]
    bm = xc[:, SSD_WIDTH:SSD_WIDTH + SSD_GROUPS * n].astype(BF16)
    cm = xc[:, SSD_WIDTH + SSD_GROUPS * n:].astype(BF16)

    lane = lax.broadcasted_iota(jnp.int32, (q, SMALL_W), 1)
    head_lane = lane < SSD_HEADS
    dt = _softplus(small_ref[...] + dtb_ref[...])
    a = -jnp.exp(alog_ref[...])
    dt = jnp.where(head_lane, dt, 0.0)
    da = jnp.where(head_lane, dt * a, 0.0)
    cs = _dot(tri_ref[...], jnp.concatenate(_split3(da), axis=0))
    cs_t = cs.T
    cs_e = _dot(jnp.concatenate(_split3(cs), axis=1), exp_ref[...])
    dt_e = _dot(jnp.concatenate(_split3(dt), axis=1), exp_ref[...])
    cs_last = cs_e[q - 1:q, :]

    xdt = xs * dt_e
    xdt_b = xdt.astype(BF16)
    xdec_b = (xdt * jnp.exp(cs_last - cs_e)).astype(BF16)
    decay_in = jnp.exp(cs_e)
    decay_chunk = jnp.exp(cs_last)

    row = lax.broadcasted_iota(jnp.int32, (q, q), 0)
    col = lax.broadcasted_iota(jnp.int32, (q, q), 1)
    causal = row >= col
    first_half = col < SSD_HEAD_DIM

    y_parts = []
    for g in range(SSD_GROUPS):
        cmg = cm[:, g * n:(g + 1) * n]
        bmg = bm[:, g * n:(g + 1) * n]
        gs = slice(g * GROUP_W, (g + 1) * GROUP_W)
        scores = _dot_nt(cmg, bmg)
        y_off = _dot(cmg, state[g].astype(BF16)) * decay_in[:, gs]
        state[g] = decay_chunk[:, gs] * state[g] + _dot_tn(bmg, xdec_b[:, gs])
        diag = []
        for pair in range(HEADS_PER_GROUP // 2):
            h0 = g * HEADS_PER_GROUP + 2 * pair
            ms = []
            for h in (h0, h0 + 1):
                seg = cs[:, h:h + 1] - cs_t[h:h + 1, :]
                decay = jnp.exp(jnp.where(causal, seg, -jnp.inf))
                ms.append((scores * decay).astype(BF16))
            slab = xdt_b[:, h0 * SSD_HEAD_DIM:(h0 + 2) * SSD_HEAD_DIM]
            zero = jnp.zeros_like(slab)
            rhs = jnp.concatenate([jnp.where(first_half, slab, zero),
                                   jnp.where(first_half, zero, slab)], axis=0)
            diag.append(_dot(jnp.concatenate(ms, axis=1), rhs))
        y_parts.append(jnp.concatenate(diag, axis=1) + y_off)
    y = jnp.concatenate(y_parts, axis=1) + xs * dexp_ref[...]
    y = y * _silu(z_ref[...].astype(F32))
    outs = []
    for g in range(SSD_GROUPS):
        yg = y[:, g * GROUP_W:(g + 1) * GROUP_W]
        outs.append(yg * _rms_scale(yg))
    y_ref[...] = (jnp.concatenate(outs, axis=1) * nw_ref[...]).astype(BF16)


def _ssd_call(z, xbc, small, cw, cb, dtb, alog, dexp, nw, tri3, exp3):
    b, s, _ = z.shape
    q = SSD_CHUNK
    row = lambda w: pl.BlockSpec((None, q, w), lambda i, j: (i, j, 0))
    consts = (cw, cb, dtb, alog, dexp, nw, tri3, exp3)
    return pl.pallas_call(
        _ssd_kernel,
        grid=(b, s // q),
        in_specs=[row(SSD_WIDTH), row(SSD_CONV_DIM), row(SMALL_W)] + [_resident(a.shape) for a in consts],
        out_specs=row(SSD_WIDTH),
        out_shape=jax.ShapeDtypeStruct((b, s, SSD_WIDTH), BF16),
        scratch_shapes=[
            pltpu.VMEM((HALO + q, SSD_CONV_DIM), F32),
            pltpu.VMEM((SSD_GROUPS, SSD_STATE, GROUP_W), F32),
        ],
        compiler_params=_params("arbitrary", "arbitrary"),
        name="ssd_mixer",
    )(z, xbc, small, *consts)


_GLA_TILE = 2 * GLA_CHUNK


def _gla_kernel(qk_ref, v_ref, go_ref, small_ref, gw_ref, gb_ref, nw_ref, tri_ref, y_ref, state):
    c = GLA_CHUNK
    dk, dv = GLA_HEAD_K, GLA_HEAD_V

    @pl.when(pl.program_id(1) == 0)
    def _():
        state[...] = jnp.zeros(state.shape, F32)

    row = lax.broadcasted_iota(jnp.int32, (c, c), 0)
    col = lax.broadcasted_iota(jnp.int32, (c, c), 1)
    causal = row >= col
    for ci in range(_GLA_TILE // c):
        r = slice(ci * c, (ci + 1) * c)
        qf = qk_ref[r, 0:GLA_K_WIDTH].astype(F32) * (dk ** -0.5)
        kf = qk_ref[r, GLA_K_WIDTH:2 * GLA_K_WIDTH].astype(F32)
        logit = _dot(small_ref[r, :].astype(BF16), gw_ref[...]) + gb_ref[...]
        lg = -_softplus(-logit) / GLA_GATE_NORM
        bcum = _dot(tri_ref[...], jnp.concatenate(_split3(lg), axis=0))
        blast = bcum[c - 1:c, :]
        qt = (qf * jnp.exp(bcum)).astype(BF16)
        kt = (kf * jnp.exp(-bcum)).astype(BF16)
        kd = (kf * jnp.exp(blast - bcum)).astype(BF16)
        eblast = jnp.exp(blast)
        for h in range(GLA_HEADS):
            ks = slice(h * dk, (h + 1) * dk)
            vs = slice(h * dv, (h + 1) * dv)
            attn = jnp.where(causal, _dot_nt(qt[:, ks], kt[:, ks]), 0.0)
            vh = v_ref[r, vs]
            o = _dot(attn.astype(BF16), vh) + _dot_nt(qt[:, ks], state[h].astype(BF16))
            state[h] = eblast[:, ks] * state[h] + _dot_tn(vh, kd[:, ks])
            o = o * _rms_scale(o) * nw_ref[...]
            o = o * _silu(go_ref[r, vs].astype(F32))
            y_ref[r, vs] = o.astype(BF16)


def _gla_call(qk, v, go, small, gw, gb, nw, tri3):
    b, s, _ = v.shape
    t = _GLA_TILE
    row = lambda w: pl.BlockSpec((None, t, w), lambda i, j: (i, j, 0))
    consts = (gw, gb, nw, tri3)
    return pl.pallas_call(
        _gla_kernel,
        grid=(b, s // t),
        in_specs=[row(2 * GLA_K_WIDTH), row(GLA_V_WIDTH), row(GLA_V_WIDTH), row(SMALL_W)]
        + [_resident(a.shape) for a in consts],
        out_specs=row(GLA_V_WIDTH),
        out_shape=jax.ShapeDtypeStruct((b, s, GLA_V_WIDTH), BF16),
        scratch_shapes=[pltpu.VMEM((GLA_HEADS, GLA_HEAD_V, GLA_HEAD_K), F32)],
        compiler_params=_params("arbitrary", "arbitrary"),
        name="gla_mixer",
    )(qk, v, go, small, *consts)


_UP_CHUNK = 384


def _ffn_kernel(ys_ref, yg_ref, x_ref, mod_ref, npost1_ref, npre_ref, wout_ref, wup_ref,
                cw_ref, cb_ref, npost2_ref, wdown_ref, o_ref, upad, gate):
    tm = x_ref.shape[0]

    @pl.when(pl.program_id(1) == 0)
    def _():
        upad[0:HALO, :] = jnp.zeros((HALO, upad.shape[1]), F32)

    y = _dot(ys_ref[...], wout_ref[0:SSD_WIDTH, :]) + _dot(yg_ref[...], wout_ref[SSD_WIDTH:, :])
    yn = (y * _rms_scale(y)) * npost1_ref[...]
    x1 = x_ref[...] + mod_ref[2:3, :] * yn
    h = ((x1 * _rms_scale(x1)) * npre_ref[...]) * (1.0 + mod_ref[4:5, :]) + mod_ref[3:4, :]
    hb = h.astype(BF16)

    def conv(c0):
        cols = slice(c0, c0 + _UP_CHUNK)
        upad[HALO:HALO + tm, cols] = _dot(hb, wup_ref[:, cols])
        acc = cb_ref[:, cols]
        for k in range(FFN_CONV):
            off = HALO - (FFN_CONV - 1) + k
            acc = acc + cw_ref[k:k + 1, cols] * upad[off:off + tm, cols]
        upad[0:HALO, cols] = upad[tm:tm + HALO, cols]
        return acc

    for c in range(0, FFN_HIDDEN, _UP_CHUNK):
        gate[:, c:c + _UP_CHUNK] = (_silu(conv(c)) * conv(FFN_HIDDEN + c)).astype(BF16)
    f = _dot(gate[...], wdown_ref[...])
    fn = (f * _rms_scale(f)) * npost2_ref[...]
    o_ref[...] = x1 + mod_ref[5:6, :] * fn


def _ffn_call(ys, yg, x, mod, npost1, npre, wout, wup, cw, cb, npost2, wdown, tm):
    b, s, d = x.shape
    row = lambda w: pl.BlockSpec((None, tm, w), lambda i, j: (i, j, 0))
    consts = (npost1, npre, wout, wup, cw, cb, npost2, wdown)
    return pl.pallas_call(
        _ffn_kernel,
        grid=(b, s // tm),
        in_specs=[row(SSD_WIDTH), row(GLA_V_WIDTH), row(d),
                  pl.BlockSpec((None, ADA_CHUNKS, d), lambda i, j: (i, 0, 0))]
        + [_resident(a.shape) for a in consts],
        out_specs=row(d),
        out_shape=jax.ShapeDtypeStruct((b, s, d), F32),
        scratch_shapes=[pltpu.VMEM((HALO + tm, 2 * FFN_HIDDEN), F32),
                        pltpu.VMEM((tm, FFN_HIDDEN), BF16)],
        compiler_params=_params("arbitrary", "arbitrary"),
        name="channel_mixer",
    )(ys, yg, x, mod, *consts)


def _lane_pad(v, width):
    return jnp.pad(v, ((0, 0), (0, width - v.shape[1])))


def _constants():
    q, c = SSD_CHUNK, GLA_CHUNK
    tri_q = np.tril(np.ones((q, q), np.float32))
    tri_c = np.tril(np.ones((c, c), np.float32))
    expand = np.zeros((SMALL_W, SSD_WIDTH), np.float32)
    for h in range(SSD_HEADS):
        expand[h, h * SSD_HEAD_DIM:(h + 1) * SSD_HEAD_DIM] = 1.0
    return (jnp.asarray(np.tile(tri_q, (1, 3)), BF16),
            jnp.asarray(np.tile(tri_c, (1, 3)), BF16),
            jnp.asarray(np.tile(expand, (3, 1)), BF16))


def kernel(x, c, w_ada, b_ada, norm_mix_pre, norm_mix_post, norm_ffn_pre, norm_ffn_post, w_in, ssd_conv_w, ssd_conv_b, ssd_dt_bias, ssd_a_log, ssd_d, ssd_norm, gla_gate_w, gla_gate_b, gla_norm, w_out, ffn_up, ffn_conv_w, ffn_conv_b, ffn_down):
    bsz, seqlen, d = x.shape
    depth = w_ada.shape[0]
    tri_q3, tri_c3, expand3 = _constants()
    c_pad = jnp.pad(c, ((0, SUBLANES - bsz % SUBLANES), (0, 0))) if bsz % SUBLANES else c

    sizes = [SSD_WIDTH, SSD_CONV_DIM, SSD_HEADS, GLA_K_WIDTH, GLA_K_WIDTH, GLA_V_WIDTH,
             GLA_GATE_RANK, GLA_V_WIDTH]
    o = np.concatenate([[0], np.cumsum(sizes)])

    for i in range(depth):
        mod = _ada_call(c_pad, w_ada[i], b_ada[i][None, :])[:bsz].reshape(bsz, ADA_CHUNKS, d)

        wi = w_in[i]
        wbig = jnp.concatenate([wi[:, o[0]:o[2]], wi[:, o[3]:o[6]], wi[:, o[7]:o[8]]], axis=1).astype(BF16)
        wsmall = jnp.concatenate(
            [wi[:, o[2]:o[3]], wi[:, o[6]:o[7]],
             jnp.zeros((d, SMALL_W - SSD_HEADS - GLA_GATE_RANK), wi.dtype)], axis=1).astype(BF16)
        z, xbc, qk, v, go, small = _inproj_call(x, mod, norm_mix_pre[i][None, :], wbig, wsmall, tm=256)

        y_ssd = _ssd_call(
            z, xbc, small, ssd_conv_w[i], ssd_conv_b[i][None, :],
            _lane_pad(ssd_dt_bias[i][None, :], SMALL_W), _lane_pad(ssd_a_log[i][None, :], SMALL_W),
            jnp.repeat(ssd_d[i], SSD_HEAD_DIM)[None, :], ssd_norm[i][None, :], tri_q3, expand3)

        gw = jnp.zeros((SMALL_W, GLA_K_WIDTH), F32).at[SSD_HEADS:SSD_HEADS + GLA_GATE_RANK].set(
            gla_gate_w[i]).astype(BF16)
        y_gla = _gla_call(qk, v, go, small, gw, gla_gate_b[i][None, :], gla_norm[i][None, :], tri_c3)

        x = _ffn_call(y_ssd, y_gla, x, mod, norm_mix_post[i][None, :], norm_ffn_pre[i][None, :],
                      w_out[i].astype(BF16), ffn_up[i].astype(BF16), ffn_conv_w[i], ffn_conv_b[i][None, :],
                      norm_ffn_post[i][None, :], ffn_down[i].astype(BF16), tm=256)
    return x
```

```python
import numpy as np
import jax
import jax.numpy as jnp
from jax import lax
from jax.experimental import pallas as pl
from jax.experimental.pallas import tpu as pltpu

F32 = jnp.float32
BF16 = jnp.bfloat16

D_MODEL = 1024
MIX_WIDTH = 2 * D_MODEL
SSD_WIDTH = MIX_WIDTH // 2
GLA_V_WIDTH = MIX_WIDTH - SSD_WIDTH
SSD_HEAD_DIM = 64
SSD_HEADS = SSD_WIDTH // SSD_HEAD_DIM
SSD_GROUPS = 2
SSD_STATE = 128
SSD_CONV = 4
SSD_CHUNK = 128
SSD_CONV_DIM = SSD_WIDTH + 2 * SSD_GROUPS * SSD_STATE
GLA_HEADS = 4
GLA_K_WIDTH = GLA_V_WIDTH // 2
GLA_HEAD_K = GLA_K_WIDTH // GLA_HEADS
GLA_HEAD_V = GLA_V_WIDTH // GLA_HEADS
GLA_GATE_RANK = 16
GLA_GATE_NORM = 16.0
GLA_CHUNK = 64
FFN_HIDDEN = int(round(8 * D_MODEL / 3 / 128)) * 128
FFN_CONV = 3
ADA_CHUNKS = 6
NORM_EPS = 1e-6

LANES = 128
SUBLANES = 8
VMEM_LIMIT_BYTES = 56 * 1024 * 1024

SMALL_W = LANES
GROUP_W = SSD_WIDTH // SSD_GROUPS
HEADS_PER_GROUP = SSD_HEADS // SSD_GROUPS
HALO = SUBLANES


def _silu(x):
    return x * jax.nn.sigmoid(x)


def _softplus(x):
    return jnp.maximum(x, 0.0) + jnp.log1p(jnp.exp(-jnp.abs(x)))


def _rms_scale(x):
    return lax.rsqrt(jnp.mean(x * x, axis=-1, keepdims=True) + NORM_EPS)


def _split3(x):
    hi = x.astype(BF16)
    r1 = x - hi.astype(F32)
    mid = r1.astype(BF16)
    lo = (r1 - mid.astype(F32)).astype(BF16)
    return hi, mid, lo


def _dot(a, b):
    return jnp.dot(a, b, preferred_element_type=F32)


def _dot_nt(a, b):
    return lax.dot_general(a, b, (((1,), (1,)), ((), ())), preferred_element_type=F32)


def _dot_tn(a, b):
    return lax.dot_general(a, b, (((0,), (0,)), ((), ())), preferred_element_type=F32)


def _resident(shape):
    nd = len(shape)
    return pl.BlockSpec(shape, lambda *_: (0,) * nd, pipeline_mode=pl.Buffered(1))


def _params(*sem):
    return pltpu.CompilerParams(dimension_semantics=sem, vmem_limit_bytes=VMEM_LIMIT_BYTES)


def _ada_kernel(c_ref, w_ref, b_ref, o_ref):
    ca = _silu(c_ref[...]).astype(BF16)
    o_ref[...] = _dot(ca, w_ref[...].astype(BF16)) + b_ref[...]


def _ada_call(c_pad, w, b):
    rows, d = c_pad.shape
    n = w.shape[1]
    tn = 1024
    return pl.pallas_call(
        _ada_kernel,
        grid=(n // tn,),
        in_specs=[
            pl.BlockSpec((rows, d), lambda j: (0, 0)),
            pl.BlockSpec((d, tn), lambda j: (0, j)),
            pl.BlockSpec((1, tn), lambda j: (0, j)),
        ],
        out_specs=pl.BlockSpec((rows, tn), lambda j: (0, j)),
        out_shape=jax.ShapeDtypeStruct((rows, n), F32),
        compiler_params=_params("arbitrary"),
        name="ada_mod",
    )(c_pad, w, b)


_Z0 = 0
_XBC0 = _Z0 + SSD_WIDTH
_Q0 = _XBC0 + SSD_CONV_DIM
_K0 = _Q0 + GLA_K_WIDTH
_V0 = _K0 + GLA_K_WIDTH
_GO0 = _V0 + GLA_V_WIDTH
_PROJ_W = _GO0 + GLA_V_WIDTH
_PROJ_CHUNK = 512


def _project(hb, w_ref, c0, width):
    parts = [_dot(hb, w_ref[:, c:c + min(_PROJ_CHUNK, c0 + width - c)])
             for c in range(c0, c0 + width, _PROJ_CHUNK)]
    return parts[0] if len(parts) == 1 else jnp.concatenate(parts, axis=1)


def _ssd_chunk(xbc, z, small, cw_ref, cb_ref, dtb_ref, alog_ref, dexp_ref, nw_ref,
               tri_ref, exp_ref, xpad, state):
    q = SSD_CHUNK
    n = SSD_STATE

    xpad[HALO:HALO + q, :] = xbc
    acc = cb_ref[...]
    for k in range(SSD_CONV):
        off = HALO - (SSD_CONV - 1) + k
        acc = acc + cw_ref[k:k + 1, :] * xpad[off:off + q, :]
    xpad[0:HALO, :] = xpad[q:q + HALO, :]
    xc = _silu(acc)
    xs = xc[:, :SSD_WIDTH]
    bm = xc[:, SSD_WIDTH:SSD_WIDTH + SSD_GROUPS * n].astype(BF16)
    cm = xc[:, SSD_WIDTH + SSD_GROUPS * n:].astype(BF16)

    lane = lax.broadcasted_iota(jnp.int32, (q, SMALL_W), 1)
    head_lane = lane < SSD_HEADS
    dt = _softplus(small + dtb_ref[...])
    a = -jnp.exp(alog_ref[...])
    dt = jnp.where(head_lane, dt, 0.0)
    da = jnp.where(head_lane, dt * a, 0.0)
    cs = _dot(tri_ref[...], jnp.concatenate(_split3(da), axis=0))
    cs_t = cs.T
    cs_e = _dot(jnp.concatenate(_split3(cs), axis=1), exp_ref[...])
    dt_e = _dot(jnp.concatenate(_split3(dt), axis=1), exp_ref[...])
    cs_last = cs_e[q - 1:q, :]

    xdt = xs * dt_e
    xdt_b = xdt.astype(BF16)
    xdec_b = (xdt * jnp.exp(cs_last - cs_e)).astype(BF16)
    decay_in = jnp.exp(cs_e)
    decay_chunk = jnp.exp(cs_last)

    row = lax.broadcasted_iota(jnp.int32, (q, q), 0)
    col = lax.broadcasted_iota(jnp.int32, (q, q), 1)
    causal = row >= col
    first_half = col < SSD_HEAD_DIM

    y_parts = []
    for g in range(SSD_GROUPS):
        cmg = cm[:, g * n:(g + 1) * n]
        bmg = bm[:, g * n:(g + 1) * n]
        gs = slice(g * GROUP_W, (g + 1) * GROUP_W)
        scores = _dot_nt(cmg, bmg)
        y_off = _dot(cmg, state[g].astype(BF16)) * decay_in[:, gs]
        state[g] = decay_chunk[:, gs] * state[g] + _dot_tn(bmg, xdec_b[:, gs])
        diag = []
        for pair in range(HEADS_PER_GROUP // 2):
            h0 = g * HEADS_PER_GROUP + 2 * pair
            ms = []
            for h in (h0, h0 + 1):
                seg = cs[:, h:h + 1] - cs_t[h:h + 1, :]
                decay = jnp.exp(jnp.where(causal, seg, -jnp.inf))
                ms.append((scores * decay).astype(BF16))
            slab = xdt_b[:, h0 * SSD_HEAD_DIM:(h0 + 2) * SSD_HEAD_DIM]
            zero = jnp.zeros_like(slab)
            rhs = jnp.concatenate([jnp.where(first_half, slab, zero),
                                   jnp.where(first_half, zero, slab)], axis=0)
            diag.append(_dot(jnp.concatenate(ms, axis=1), rhs))
        y_parts.append(jnp.concatenate(diag, axis=1) + y_off)
    y = jnp.concatenate(y_parts, axis=1) + xs * dexp_ref[...]
    y = y * _silu(z)
    outs = []
    for g in range(SSD_GROUPS):
        yg = y[:, g * GROUP_W:(g + 1) * GROUP_W]
        outs.append(yg * _rms_scale(yg))
    return (jnp.concatenate(outs, axis=1) * nw_ref[...]).astype(BF16)


def _gla_chunk(qf, kf, v, go, small, gw_ref, gb_ref, nw_ref, tri_ref, state, y_ref, r):
    c = GLA_CHUNK
    dk, dv = GLA_HEAD_K, GLA_HEAD_V
    row = lax.broadcasted_iota(jnp.int32, (c, c), 0)
    col = lax.broadcasted_iota(jnp.int32, (c, c), 1)
    causal = row >= col

    qf = qf * (dk ** -0.5)
    logit = _dot(small.astype(BF16), gw_ref[...]) + gb_ref[...]
    lg = -_softplus(-logit) / GLA_GATE_NORM
    bcum = _dot(tri_ref[...], jnp.concatenate(_split3(lg), axis=0))
    blast = bcum[c - 1:c, :]
    qt = (qf * jnp.exp(bcum)).astype(BF16)
    kt = (kf * jnp.exp(-bcum)).astype(BF16)
    kd = (kf * jnp.exp(blast - bcum)).astype(BF16)
    eblast = jnp.exp(blast)
    for h in range(GLA_HEADS):
        ks = slice(h * dk, (h + 1) * dk)
        vs = slice(h * dv, (h + 1) * dv)
        attn = jnp.where(causal, _dot_nt(qt[:, ks], kt[:, ks]), 0.0)
        vh = v[:, vs]
        o = _dot(attn.astype(BF16), vh) + _dot_nt(qt[:, ks], state[h].astype(BF16))
        state[h] = eblast[:, ks] * state[h] + _dot_tn(vh, kd[:, ks])
        o = o * _rms_scale(o) * nw_ref[...]
        o = o * _silu(go[:, vs])
        y_ref[r, vs] = o.astype(BF16)


def _mixer_kernel(x_ref, mod_ref, npre_ref, wbig_ref, wsmall_ref,
                  cw_ref, cb_ref, dtb_ref, alog_ref, dexp_ref, snw_ref, triq_ref, exp_ref,
                  gw_ref, gb_ref, gnw_ref, tric_ref,
                  ys_ref, yg_ref, xpad, sstate, gstate):
    @pl.when(pl.program_id(1) == 0)
    def _():
        xpad[0:HALO, :] = jnp.zeros((HALO, SSD_CONV_DIM), F32)
        sstate[...] = jnp.zeros(sstate.shape, F32)
        gstate[...] = jnp.zeros(gstate.shape, F32)

    x = x_ref[...]
    xn = (x * _rms_scale(x)) * npre_ref[...]
    hb = (xn * (1.0 + mod_ref[1:2, :]) + mod_ref[0:1, :]).astype(BF16)
    small = _dot(hb, wsmall_ref[...])
    xbc = _project(hb, wbig_ref, _XBC0, SSD_CONV_DIM)
    qf = _project(hb, wbig_ref, _Q0, GLA_K_WIDTH)
    kf = _project(hb, wbig_ref, _K0, GLA_K_WIDTH)
    v = _project(hb, wbig_ref, _V0, GLA_V_WIDTH).astype(BF16)
    go = _project(hb, wbig_ref, _GO0, GLA_V_WIDTH)
    z = _project(hb, wbig_ref, _Z0, SSD_WIDTH)

    ys_ref[...] = _ssd_chunk(xbc, z, small, cw_ref, cb_ref, dtb_ref, alog_ref, dexp_ref,
                             snw_ref, triq_ref, exp_ref, xpad, sstate)
    for ci in range(SSD_CHUNK // GLA_CHUNK):
        r = slice(ci * GLA_CHUNK, (ci + 1) * GLA_CHUNK)
        _gla_chunk(qf[r, :], kf[r, :], v[r, :], go[r, :], small[r, :],
                   gw_ref, gb_ref, gnw_ref, tric_ref, gstate, yg_ref, r)


def _mixer_call(x, mod, npre, wbig, wsmall, ssd_consts, gla_consts):
    b, s, d = x.shape
    t = SSD_CHUNK
    row = lambda w: pl.BlockSpec((None, t, w), lambda i, j: (i, j, 0))
    consts = (npre, wbig, wsmall) + tuple(ssd_consts) + tuple(gla_consts)
    return pl.pallas_call(
        _mixer_kernel,
        grid=(b, s // t),
        in_specs=[row(d), pl.BlockSpec((None, ADA_CHUNKS, d), lambda i, j: (i, 0, 0))]
        + [_resident(a.shape) for a in consts],
        out_specs=[row(SSD_WIDTH), row(GLA_V_WIDTH)],
        out_shape=[jax.ShapeDtypeStruct((b, s, SSD_WIDTH), BF16),
                   jax.ShapeDtypeStruct((b, s, GLA_V_WIDTH), BF16)],
        scratch_shapes=[
            pltpu.VMEM((HALO + t, SSD_CONV_DIM), F32),
            pltpu.VMEM((SSD_GROUPS, SSD_STATE, GROUP_W), F32),
            pltpu.VMEM((GLA_HEADS, GLA_HEAD_V, GLA_HEAD_K), F32),
        ],
        compiler_params=_params("arbitrary", "arbitrary"),
        name="token_mixer",
    )(x, mod, *consts)


_UP_CHUNK = 384


def _ffn_kernel(ys_ref, yg_ref, x_ref, mod_ref, npost1_ref, npre_ref, wout_ref, wup_ref,
                cw_ref, cb_ref, npost2_ref, wdown_ref, o_ref, upad, gate):
    tm = x_ref.shape[0]

    @pl.when(pl.program_id(1) == 0)
    def _():
        upad[0:HALO, :] = jnp.zeros((HALO, upad.shape[1]), F32)

    y = _dot(ys_ref[...], wout_ref[0:SSD_WIDTH, :]) + _dot(yg_ref[...], wout_ref[SSD_WIDTH:, :])
    yn = (y * _rms_scale(y)) * npost1_ref[...]
    x1 = x_ref[...] + mod_ref[2:3, :] * yn
    h = ((x1 * _rms_scale(x1)) * npre_ref[...]) * (1.0 + mod_ref[4:5, :]) + mod_ref[3:4, :]
    hb = h.astype(BF16)

    def conv(c0):
        cols = slice(c0, c0 + _UP_CHUNK)
        upad[HALO:HALO + tm, cols] = _dot(hb, wup_ref[:, cols])
        acc = cb_ref[:, cols]
        for k in range(FFN_CONV):
            off = HALO - (FFN_CONV - 1) + k
            acc = acc + cw_ref[k:k + 1, cols] * upad[off:off + tm, cols]
        upad[0:HALO, cols] = upad[tm:tm + HALO, cols]
        return acc

    for c in range(0, FFN_HIDDEN, _UP_CHUNK):
        gate[:, c:c + _UP_CHUNK] = (_silu(conv(c)) * conv(FFN_HIDDEN + c)).astype(BF16)
    f = _dot(gate[...], wdown_ref[...])
    fn = (f * _rms_scale(f)) * npost2_ref[...]
    o_ref[...] = x1 + mod_ref[5:6, :] * fn


def _ffn_call(ys, yg, x, mod, npost1, npre, wout, wup, cw, cb, npost2, wdown, tm):
    b, s, d = x.shape
    row = lambda w: pl.BlockSpec((None, tm, w), lambda i, j: (i, j, 0))
    consts = (npost1, npre, wout, wup, cw, cb, npost2, wdown)
    return pl.pallas_call(
        _ffn_kernel,
        grid=(b, s // tm),
        in_specs=[row(SSD_WIDTH), row(GLA_V_WIDTH), row(d),
                  pl.BlockSpec((None, ADA_CHUNKS, d), lambda i, j: (i, 0, 0))]
        + [_resident(a.shape) for a in consts],
        out_specs=row(d),
        out_shape=jax.ShapeDtypeStruct((b, s, d), F32),
        scratch_shapes=[pltpu.VMEM((HALO + tm, 2 * FFN_HIDDEN), F32),
                        pltpu.VMEM((tm, FFN_HIDDEN), BF16)],
        compiler_params=_params("arbitrary", "arbitrary"),
        name="channel_mixer",
    )(ys, yg, x, mod, *consts)


def _lane_pad(v, width):
    return jnp.pad(v, ((0, 0), (0, width - v.shape[1])))


def _constants():
    q, c = SSD_CHUNK, GLA_CHUNK
    tri_q = np.tril(np.ones((q, q), np.float32))
    tri_c = np.tril(np.ones((c, c), np.float32))
    expand = np.zeros((SMALL_W, SSD_WIDTH), np.float32)
    for h in range(SSD_HEADS):
        expand[h, h * SSD_HEAD_DIM:(h + 1) * SSD_HEAD_DIM] = 1.0
    return (jnp.asarray(np.tile(tri_q, (1, 3)), BF16),
            jnp.asarray(np.tile(tri_c, (1, 3)), BF16),
            jnp.asarray(np.tile(expand, (3, 1)), BF16))


def kernel(x, c, w_ada, b_ada, norm_mix_pre, norm_mix_post, norm_ffn_pre, norm_ffn_post, w_in, ssd_conv_w, ssd_conv_b, ssd_dt_bias, ssd_a_log, ssd_d, ssd_norm, gla_gate_w, gla_gate_b, gla_norm, w_out, ffn_up, ffn_conv_w, ffn_conv_b, ffn_down):
    bsz, seqlen, d = x.shape
    depth = w_ada.shape[0]
    tri_q3, tri_c3, expand3 = _constants()
    c_pad = jnp.pad(c, ((0, SUBLANES - bsz % SUBLANES), (0, 0))) if bsz % SUBLANES else c

    sizes = [SSD_WIDTH, SSD_CONV_DIM, SSD_HEADS, GLA_K_WIDTH, GLA_K_WIDTH, GLA_V_WIDTH,
             GLA_GATE_RANK, GLA_V_WIDTH]
    o = np.concatenate([[0], np.cumsum(sizes)])

    for i in range(depth):
        mod = _ada_call(c_pad, w_ada[i], b_ada[i][None, :])[:bsz].reshape(bsz, ADA_CHUNKS, d)

        wi = w_in[i]
        wbig = jnp.concatenate([wi[:, o[0]:o[2]], wi[:, o[3]:o[6]], wi[:, o[7]:o[8]]], axis=1).astype(BF16)
        wsmall = jnp.concatenate(
            [wi[:, o[2]:o[3]], wi[:, o[6]:o[7]],
             jnp.zeros((d, SMALL_W - SSD_HEADS - GLA_GATE_RANK), wi.dtype)], axis=1).astype(BF16)
        ssd_consts = (ssd_conv_w[i], ssd_conv_b[i][None, :],
                      _lane_pad(ssd_dt_bias[i][None, :], SMALL_W), _lane_pad(ssd_a_log[i][None, :], SMALL_W),
                      jnp.repeat(ssd_d[i], SSD_HEAD_DIM)[None, :], ssd_norm[i][None, :], tri_q3, expand3)
        gw = jnp.zeros((SMALL_W, GLA_K_WIDTH), F32).at[SSD_HEADS:SSD_HEADS + GLA_GATE_RANK].set(
            gla_gate_w[i]).astype(BF16)
        gla_consts = (gw, gla_gate_b[i][None, :], gla_norm[i][None, :], tri_c3)
        y_ssd, y_gla = _mixer_call(x, mod, norm_mix_pre[i][None, :], wbig, wsmall, ssd_consts, gla_consts)

        x = _ffn_call(y_ssd, y_gla, x, mod, norm_mix_post[i][None, :], norm_ffn_pre[i][None, :],
                      w_out[i].astype(BF16), ffn_up[i].astype(BF16), ffn_conv_w[i], ffn_conv_b[i][None, :],
                      norm_ffn_post[i][None, :], ffn_down[i].astype(BF16), tm=256)
    return x
```

```python
import numpy as np
import jax
import jax.numpy as jnp
from jax import lax
from jax.experimental import pallas as pl
from jax.experimental.pallas import tpu as pltpu

F32 = jnp.float32
BF16 = jnp.bfloat16

D_MODEL = 1024
MIX_WIDTH = 2 * D_MODEL
SSD_WIDTH = MIX_WIDTH // 2
GLA_V_WIDTH = MIX_WIDTH - SSD_WIDTH
SSD_HEAD_DIM = 64
SSD_HEADS = SSD_WIDTH // SSD_HEAD_DIM
SSD_GROUPS = 2
SSD_STATE = 128
SSD_CONV = 4
SSD_CHUNK = 128
SSD_CONV_DIM = SSD_WIDTH + 2 * SSD_GROUPS * SSD_STATE
GLA_HEADS = 4
GLA_K_WIDTH = GLA_V_WIDTH // 2
GLA_HEAD_K = GLA_K_WIDTH // GLA_HEADS
GLA_HEAD_V = GLA_V_WIDTH // GLA_HEADS
GLA_GATE_RANK = 16
GLA_GATE_NORM = 16.0
GLA_CHUNK = 64
FFN_HIDDEN = int(round(8 * D_MODEL / 3 / 128)) * 128
FFN_CONV = 3
ADA_CHUNKS = 6
NORM_EPS = 1e-6

LANES = 128
SUBLANES = 8
VMEM_LIMIT_BYTES = 56 * 1024 * 1024

SMALL_W = LANES
GROUP_W = SSD_WIDTH // SSD_GROUPS
HEADS_PER_GROUP = SSD_HEADS // SSD_GROUPS
HALO = SUBLANES


def _silu(x):
    return x * jax.nn.sigmoid(x)


def _softplus(x):
    return jnp.maximum(x, 0.0) + jnp.log(1.0 + jnp.exp(-jnp.abs(x)))


def _rms_scale(x):
    return lax.rsqrt(jnp.mean(x * x, axis=-1, keepdims=True) + NORM_EPS)


def _split3(x):
    hi = x.astype(BF16)
    r1 = x - hi.astype(F32)
    mid = r1.astype(BF16)
    lo = (r1 - mid.astype(F32)).astype(BF16)
    return hi, mid, lo


def _dot(a, b):
    return jnp.dot(a, b, preferred_element_type=F32)


def _dot_nt(a, b):
    return lax.dot_general(a, b, (((1,), (1,)), ((), ())), preferred_element_type=F32)


def _dot_tn(a, b):
    return lax.dot_general(a, b, (((0,), (0,)), ((), ())), preferred_element_type=F32)


def _resident(shape):
    nd = len(shape)
    return pl.BlockSpec(shape, lambda *_: (0,) * nd, pipeline_mode=pl.Buffered(1))


def _params(*sem):
    return pltpu.CompilerParams(dimension_semantics=sem, vmem_limit_bytes=VMEM_LIMIT_BYTES)


def _ada_kernel(c_ref, w_ref, b_ref, o_ref):
    ca = _silu(c_ref[...]).astype(BF16)
    o_ref[...] = _dot(ca, w_ref[...].astype(BF16)) + b_ref[...]


def _ada_call(c_pad, w, b):
    rows, d = c_pad.shape
    n = w.shape[1]
    tn = 1024
    return pl.pallas_call(
        _ada_kernel,
        grid=(n // tn,),
        in_specs=[
            pl.BlockSpec((rows, d), lambda j: (0, 0)),
            pl.BlockSpec((d, tn), lambda j: (0, j)),
            pl.BlockSpec((1, tn), lambda j: (0, j)),
        ],
        out_specs=pl.BlockSpec((rows, tn), lambda j: (0, j)),
        out_shape=jax.ShapeDtypeStruct((rows, n), F32),
        compiler_params=_params("arbitrary"),
        name="ada_mod",
    )(c_pad, w, b)


_Z0 = 0
_XBC0 = _Z0 + SSD_WIDTH
_Q0 = _XBC0 + SSD_CONV_DIM
_K0 = _Q0 + GLA_K_WIDTH
_V0 = _K0 + GLA_K_WIDTH
_GO0 = _V0 + GLA_V_WIDTH
_PROJ_W = _GO0 + GLA_V_WIDTH
_PROJ_CHUNK = 512


def _project(hb_ref, w_ref, c0, width):
    parts = [_dot(hb_ref[...], w_ref[:, c:c + min(_PROJ_CHUNK, c0 + width - c)])
             for c in range(c0, c0 + width, _PROJ_CHUNK)]
    return parts[0] if len(parts) == 1 else jnp.concatenate(parts, axis=1)


def _ssd_stages(env, cw_ref, cb_ref, dtb_ref, alog_ref, dexp_ref, nw_ref, tri_ref, exp_ref,
                xpad, state, y_ref):
    q = SSD_CHUNK
    n = SSD_STATE

    xpad[HALO:HALO + q, :] = env["xbc"]
    acc = cb_ref[...]
    for k in range(SSD_CONV):
        off = HALO - (SSD_CONV - 1) + k
        acc = acc + cw_ref[k:k + 1, :] * xpad[off:off + q, :]
    xpad[0:HALO, :] = xpad[q:q + HALO, :]
    xc = _silu(acc)
    xs = xc[:, :SSD_WIDTH]
    bm = xc[:, SSD_WIDTH:SSD_WIDTH + SSD_GROUPS * n].astype(BF16)
    cm = xc[:, SSD_WIDTH + SSD_GROUPS * n:].astype(BF16)
    yield

    lane = lax.broadcasted_iota(jnp.int32, (q, SMALL_W), 1)
    head_lane = lane < SSD_HEADS
    dt = _softplus(env["small"] + dtb_ref[...])
    a = -jnp.exp(alog_ref[...])
    dt = jnp.where(head_lane, dt, 0.0)
    da = jnp.where(head_lane, dt * a, 0.0)
    cs = _dot(tri_ref[...], jnp.concatenate(_split3(da), axis=0))
    cs_t = cs.T
    cs_e = _dot(jnp.concatenate(_split3(cs), axis=1), exp_ref[...])
    dt_e = _dot(jnp.concatenate(_split3(dt), axis=1), exp_ref[...])
    cs_last = cs_e[q - 1:q, :]

    xdt = xs * dt_e
    xdt_b = xdt.astype(BF16)
    xdec_b = (xdt * jnp.exp(cs_last - cs_e)).astype(BF16)
    decay_in = jnp.exp(cs_e)
    decay_chunk = jnp.exp(cs_last)
    yield

    row = lax.broadcasted_iota(jnp.int32, (q, q), 0)
    col = lax.broadcasted_iota(jnp.int32, (q, q), 1)
    causal = row >= col
    first_half = col < SSD_HEAD_DIM

    y_parts = []
    for g in range(SSD_GROUPS):
        cmg = cm[:, g * n:(g + 1) * n]
        bmg = bm[:, g * n:(g + 1) * n]
        gs = slice(g * GROUP_W, (g + 1) * GROUP_W)
        scores = _dot_nt(cmg, bmg)
        y_off = _dot(cmg, state[g].astype(BF16)) * decay_in[:, gs]
        state[g] = decay_chunk[:, gs] * state[g] + _dot_tn(bmg, xdec_b[:, gs])
        diag = []
        for pair in range(HEADS_PER_GROUP // 2):
            h0 = g * HEADS_PER_GROUP + 2 * pair
            ms = []
            for h in (h0, h0 + 1):
                seg = cs[:, h:h + 1] - cs_t[h:h + 1, :]
                decay = jnp.exp(jnp.where(causal, seg, -jnp.inf))
                ms.append((scores * decay).astype(BF16))
            slab = xdt_b[:, h0 * SSD_HEAD_DIM:(h0 + 2) * SSD_HEAD_DIM]
            zero = jnp.zeros_like(slab)
            rhs = jnp.concatenate([jnp.where(first_half, slab, zero),
                                   jnp.where(first_half, zero, slab)], axis=0)
            diag.append(_dot(jnp.concatenate(ms, axis=1), rhs))
        y_parts.append(jnp.concatenate(diag, axis=1) + y_off)
        yield
    y = jnp.concatenate(y_parts, axis=1) + xs * dexp_ref[...]
    y = y * _silu(env["z"])
    outs = []
    for g in range(SSD_GROUPS):
        yg = y[:, g * GROUP_W:(g + 1) * GROUP_W]
        outs.append(yg * _rms_scale(yg))
    y_ref[...] = (jnp.concatenate(outs, axis=1) * nw_ref[...]).astype(BF16)
    yield


def _gla_stages(env, r, gw_ref, gb_ref, nw_ref, tri_ref, state, y_ref):
    c = GLA_CHUNK
    dk, dv = GLA_HEAD_K, GLA_HEAD_V
    row = lax.broadcasted_iota(jnp.int32, (c, c), 0)
    col = lax.broadcasted_iota(jnp.int32, (c, c), 1)
    causal = row >= col

    qf = env["q"][r, :] * (dk ** -0.5)
    kf = env["k"][r, :]
    logit = _dot(env["small"][r, :].astype(BF16), gw_ref[...]) + gb_ref[...]
    lg = -_softplus(-logit) / GLA_GATE_NORM
    bcum = _dot(tri_ref[...], jnp.concatenate(_split3(lg), axis=0))
    blast = bcum[c - 1:c, :]
    qt = (qf * jnp.exp(bcum)).astype(BF16)
    kt = (kf * jnp.exp(-bcum)).astype(BF16)
    kd = (kf * jnp.exp(blast - bcum)).astype(BF16)
    eblast = jnp.exp(blast)
    yield
    for h in range(GLA_HEADS):
        ks = slice(h * dk, (h + 1) * dk)
        vs = slice(h * dv, (h + 1) * dv)
        attn = jnp.where(causal, _dot_nt(qt[:, ks], kt[:, ks]), 0.0)
        vh = env["v"][r, vs]
        o = _dot(attn.astype(BF16), vh) + _dot_nt(qt[:, ks], state[h].astype(BF16))
        state[h] = eblast[:, ks] * state[h] + _dot_tn(vh, kd[:, ks])
        o = o * _rms_scale(o) * nw_ref[...]
        o = o * _silu(env["go"][r, vs])
        y_ref[r, vs] = o.astype(BF16)
        if h % 2 == 1:
            yield


def _mixer_kernel(x_ref, mod_ref, npre_ref, wbig_ref, wsmall_ref,
                  cw_ref, cb_ref, dtb_ref, alog_ref, dexp_ref, snw_ref, triq_ref, exp_ref,
                  gw_ref, gb_ref, gnw_ref, tric_ref,
                  ys_ref, yg_ref, xpad, sstate, gstate, hb):
    @pl.when(pl.program_id(1) == 0)
    def _():
        xpad[0:HALO, :] = jnp.zeros((HALO, SSD_CONV_DIM), F32)
        sstate[...] = jnp.zeros(sstate.shape, F32)
        gstate[...] = jnp.zeros(gstate.shape, F32)

    x = x_ref[...]
    xn = (x * _rms_scale(x)) * npre_ref[...]
    hb[...] = (xn * (1.0 + mod_ref[1:2, :]) + mod_ref[0:1, :]).astype(BF16)

    env = {"small": _dot(hb[...], wsmall_ref[...]), "xbc": _project(hb, wbig_ref, _XBC0, SSD_CONV_DIM)}
    ssd = _ssd_stages(env, cw_ref, cb_ref, dtb_ref, alog_ref, dexp_ref, snw_ref, triq_ref, exp_ref,
                      xpad, sstate, ys_ref)
    gla = [_gla_stages(env, slice(ci * GLA_CHUNK, (ci + 1) * GLA_CHUNK), gw_ref, gb_ref, gnw_ref,
                       tric_ref, gstate, yg_ref) for ci in range(SSD_CHUNK // GLA_CHUNK)]
    env["q"] = _project(hb, wbig_ref, _Q0, GLA_K_WIDTH)
    env["k"] = _project(hb, wbig_ref, _K0, GLA_K_WIDTH)
    next(ssd)
    env["v"] = _project(hb, wbig_ref, _V0, GLA_V_WIDTH).astype(BF16)
    next(gla[0])
    next(ssd)
    env["go"] = _project(hb, wbig_ref, _GO0, GLA_V_WIDTH)
    next(gla[0])
    next(ssd)
    env["z"] = _project(hb, wbig_ref, _Z0, SSD_WIDTH)
    next(gla[0])
    next(gla[1])
    next(ssd)
    next(gla[1])
    next(ssd)
    next(gla[1])


def _mixer_call(x, mod, npre, wbig, wsmall, ssd_consts, gla_consts):
    b, s, d = x.shape
    t = SSD_CHUNK
    row = lambda w: pl.BlockSpec((None, t, w), lambda i, j: (i, j, 0))
    consts = (npre, wbig, wsmall) + tuple(ssd_consts) + tuple(gla_consts)
    return pl.pallas_call(
        _mixer_kernel,
        grid=(b, s // t),
        in_specs=[row(d), pl.BlockSpec((None, ADA_CHUNKS, d), lambda i, j: (i, 0, 0))]
        + [_resident(a.shape) for a in consts],
        out_specs=[row(SSD_WIDTH), row(GLA_V_WIDTH)],
        out_shape=[jax.ShapeDtypeStruct((b, s, SSD_WIDTH), BF16),
                   jax.ShapeDtypeStruct((b, s, GLA_V_WIDTH), BF16)],
        scratch_shapes=[
            pltpu.VMEM((HALO + t, SSD_CONV_DIM), F32),
            pltpu.VMEM((SSD_GROUPS, SSD_STATE, GROUP_W), F32),
            pltpu.VMEM((GLA_HEADS, GLA_HEAD_V, GLA_HEAD_K), F32),
            pltpu.VMEM((t, d), BF16),
        ],
        compiler_params=_params("arbitrary", "arbitrary"),
        name="token_mixer",
    )(x, mod, *consts)


_UP_CHUNK = 384


def _ffn_kernel(ys_ref, yg_ref, x_ref, mod_ref, npost1_ref, npre_ref, wout_ref, wup_ref,
                cw_ref, cb_ref, npost2_ref, wdown_ref, o_ref, upad, gate):
    tm = x_ref.shape[0]

    @pl.when(pl.program_id(1) == 0)
    def _():
        upad[0:HALO, :] = jnp.zeros((HALO, upad.shape[1]), F32)

    y = _dot(ys_ref[...], wout_ref[0:SSD_WIDTH, :]) + _dot(yg_ref[...], wout_ref[SSD_WIDTH:, :])
    yn = (y * _rms_scale(y)) * npost1_ref[...]
    x1 = x_ref[...] + mod_ref[2:3, :] * yn
    h = ((x1 * _rms_scale(x1)) * npre_ref[...]) * (1.0 + mod_ref[4:5, :]) + mod_ref[3:4, :]
    hb = h.astype(BF16)

    def conv(c0):
        cols = slice(c0, c0 + _UP_CHUNK)
        upad[HALO:HALO + tm, cols] = _dot(hb, wup_ref[:, cols])
        acc = cb_ref[:, cols]
        for k in range(FFN_CONV):
            off = HALO - (FFN_CONV - 1) + k
            acc = acc + cw_ref[k:k + 1, cols] * upad[off:off + tm, cols]
        upad[0:HALO, cols] = upad[tm:tm + HALO, cols]
        return acc

    for c in range(0, FFN_HIDDEN, _UP_CHUNK):
        gate[:, c:c + _UP_CHUNK] = (_silu(conv(c)) * conv(FFN_HIDDEN + c)).astype(BF16)
    f = _dot(gate[...], wdown_ref[...])
    fn = (f * _rms_scale(f)) * npost2_ref[...]
    o_ref[...] = x1 + mod_ref[5:6, :] * fn


def _ffn_call(ys, yg, x, mod, npost1, npre, wout, wup, cw, cb, npost2, wdown, tm):
    b, s, d = x.shape
    row = lambda w: pl.BlockSpec((None, tm, w), lambda i, j: (i, j, 0))
    consts = (npost1, npre, wout, wup, cw, cb, npost2, wdown)
    return pl.pallas_call(
        _ffn_kernel,
        grid=(b, s // tm),
        in_specs=[row(SSD_WIDTH), row(GLA_V_WIDTH), row(d),
                  pl.BlockSpec((None, ADA_CHUNKS, d), lambda i, j: (i, 0, 0))]
        + [_resident(a.shape) for a in consts],
        out_specs=row(d),
        out_shape=jax.ShapeDtypeStruct((b, s, d), F32),
        scratch_shapes=[pltpu.VMEM((HALO + tm, 2 * FFN_HIDDEN), F32),
                        pltpu.VMEM((tm, FFN_HIDDEN), BF16)],
        compiler_params=_params("arbitrary", "arbitrary"),
        name="channel_mixer",
    )(ys, yg, x, mod, *consts)


def _lane_pad(v, width):
    return jnp.pad(v, ((0, 0), (0, width - v.shape[1])))


def _constants():
    q, c = SSD_CHUNK, GLA_CHUNK
    tri_q = np.tril(np.ones((q, q), np.float32))
    tri_c = np.tril(np.ones((c, c), np.float32))
    expand = np.zeros((SMALL_W, SSD_WIDTH), np.float32)
    for h in range(SSD_HEADS):
        expand[h, h * SSD_HEAD_DIM:(h + 1) * SSD_HEAD_DIM] = 1.0
    return (jnp.asarray(np.tile(tri_q, (1, 3)), BF16),
            jnp.asarray(np.tile(tri_c, (1, 3)), BF16),
            jnp.asarray(np.tile(expand, (3, 1)), BF16))


def kernel(x, c, w_ada, b_ada, norm_mix_pre, norm_mix_post, norm_ffn_pre, norm_ffn_post, w_in, ssd_conv_w, ssd_conv_b, ssd_dt_bias, ssd_a_log, ssd_d, ssd_norm, gla_gate_w, gla_gate_b, gla_norm, w_out, ffn_up, ffn_conv_w, ffn_conv_b, ffn_down):
    bsz, seqlen, d = x.shape
    depth = w_ada.shape[0]
    tri_q3, tri_c3, expand3 = _constants()
    c_pad = jnp.pad(c, ((0, SUBLANES - bsz % SUBLANES), (0, 0))) if bsz % SUBLANES else c

    sizes = [SSD_WIDTH, SSD_CONV_DIM, SSD_HEADS, GLA_K_WIDTH, GLA_K_WIDTH, GLA_V_WIDTH,
             GLA_GATE_RANK, GLA_V_WIDTH]
    o = np.concatenate([[0], np.cumsum(sizes)])

    for i in range(depth):
        mod = _ada_call(c_pad, w_ada[i], b_ada[i][None, :])[:bsz].reshape(bsz, ADA_CHUNKS, d)

        wi = w_in[i]
        wbig = jnp.concatenate([wi[:, o[0]:o[2]], wi[:, o[3]:o[6]], wi[:, o[7]:o[8]]], axis=1).astype(BF16)
        wsmall = jnp.concatenate(
            [wi[:, o[2]:o[3]], wi[:, o[6]:o[7]],
             jnp.zeros((d, SMALL_W - SSD_HEADS - GLA_GATE_RANK), wi.dtype)], axis=1).astype(BF16)
        ssd_consts = (ssd_conv_w[i], ssd_conv_b[i][None, :],
                      _lane_pad(ssd_dt_bias[i][None, :], SMALL_W), _lane_pad(ssd_a_log[i][None, :], SMALL_W),
                      jnp.repeat(ssd_d[i], SSD_HEAD_DIM)[None, :], ssd_norm[i][None, :], tri_q3, expand3)
        gw = jnp.zeros((SMALL_W, GLA_K_WIDTH), F32).at[SSD_HEADS:SSD_HEADS + GLA_GATE_RANK].set(
            gla_gate_w[i]).astype(BF16)
        gla_consts = (gw, gla_gate_b[i][None, :], gla_norm[i][None, :], tri_c3)
        y_ssd, y_gla = _mixer_call(x, mod, norm_mix_pre[i][None, :], wbig, wsmall, ssd_consts, gla_consts)

        x = _ffn_call(y_ssd, y_gla, x, mod, norm_mix_post[i][None, :], norm_ffn_pre[i][None, :],
                      w_out[i].astype(BF16), ffn_up[i].astype(BF16), ffn_conv_w[i], ffn_conv_b[i][None, :],
                      norm_ffn_post[i][None, :], ffn_down[i].astype(BF16), tm=256)
    return x
```

```python
import numpy as np
import jax
import jax.numpy as jnp
from jax import lax
from jax.experimental import pallas as pl
from jax.experimental.pallas import tpu as pltpu

F32 = jnp.float32
BF16 = jnp.bfloat16

D_MODEL = 1024
MIX_WIDTH = 2 * D_MODEL
SSD_WIDTH = MIX_WIDTH // 2
GLA_V_WIDTH = MIX_WIDTH - SSD_WIDTH
SSD_HEAD_DIM = 64
SSD_HEADS = SSD_WIDTH // SSD_HEAD_DIM
SSD_GROUPS = 2
SSD_STATE = 128
SSD_CONV = 4
SSD_CHUNK = 128
SSD_CONV_DIM = SSD_WIDTH + 2 * SSD_GROUPS * SSD_STATE
GLA_HEADS = 4
GLA_K_WIDTH = GLA_V_WIDTH // 2
GLA_HEAD_K = GLA_K_WIDTH // GLA_HEADS
GLA_HEAD_V = GLA_V_WIDTH // GLA_HEADS
GLA_GATE_RANK = 16
GLA_GATE_NORM = 16.0
GLA_CHUNK = 64
FFN_HIDDEN = int(round(8 * D_MODEL / 3 / 128)) * 128
FFN_CONV = 3
ADA_CHUNKS = 6
NORM_EPS = 1e-6

LANES = 128
SUBLANES = 8
VMEM_LIMIT_BYTES = 56 * 1024 * 1024

SMALL_W = LANES
GROUP_W = SSD_WIDTH // SSD_GROUPS
HEADS_PER_GROUP = SSD_HEADS // SSD_GROUPS
HALO = SUBLANES


def _silu(x):
    return x * jax.nn.sigmoid(x)


def _softplus(x):
    return jnp.maximum(x, 0.0) + jnp.log(1.0 + jnp.exp(-jnp.abs(x)))


def _rms_scale(x):
    return lax.rsqrt(jnp.mean(x * x, axis=-1, keepdims=True) + NORM_EPS)


def _split3(x):
    hi = x.astype(BF16)
    r1 = x - hi.astype(F32)
    mid = r1.astype(BF16)
    lo = (r1 - mid.astype(F32)).astype(BF16)
    return hi, mid, lo


def _dot(a, b):
    return jnp.dot(a, b, preferred_element_type=F32)


def _dot_nt(a, b):
    return lax.dot_general(a, b, (((1,), (1,)), ((), ())), preferred_element_type=F32)


def _dot_tn(a, b):
    return lax.dot_general(a, b, (((0,), (0,)), ((), ())), preferred_element_type=F32)


def _resident(shape):
    nd = len(shape)
    return pl.BlockSpec(shape, lambda *_: (0,) * nd, pipeline_mode=pl.Buffered(1))


def _params(*sem):
    return pltpu.CompilerParams(dimension_semantics=sem, vmem_limit_bytes=VMEM_LIMIT_BYTES)


def _ada_kernel(c_ref, w_ref, b_ref, o_ref):
    ca = _silu(c_ref[...]).astype(BF16)
    o_ref[...] = _dot(ca, w_ref[...].astype(BF16)) + b_ref[...]


def _ada_call(c_pad, w, b):
    rows, d = c_pad.shape
    n = w.shape[1]
    tn = 1024
    return pl.pallas_call(
        _ada_kernel,
        grid=(n // tn,),
        in_specs=[
            pl.BlockSpec((rows, d), lambda j: (0, 0)),
            pl.BlockSpec((d, tn), lambda j: (0, j)),
            pl.BlockSpec((1, tn), lambda j: (0, j)),
        ],
        out_specs=pl.BlockSpec((rows, tn), lambda j: (0, j)),
        out_shape=jax.ShapeDtypeStruct((rows, n), F32),
        compiler_params=_params("arbitrary"),
        name="ada_mod",
    )(c_pad, w, b)


_Z0 = 0
_XBC0 = _Z0 + SSD_WIDTH
_Q0 = _XBC0 + SSD_CONV_DIM
_K0 = _Q0 + GLA_K_WIDTH
_V0 = _K0 + GLA_K_WIDTH
_GO0 = _V0 + GLA_V_WIDTH
_PROJ_W = _GO0 + GLA_V_WIDTH
_PROJ_CHUNK = 512


def _project(hb, w_ref, c0, width):
    parts = [_dot(hb, w_ref[:, c:c + min(_PROJ_CHUNK, c0 + width - c)])
             for c in range(c0, c0 + width, _PROJ_CHUNK)]
    return parts[0] if len(parts) == 1 else jnp.concatenate(parts, axis=1)


def _ssd_stages(env, cw_ref, cb_ref, dtb_ref, alog_ref, dexp_ref, nw_ref, tri_ref, exp_ref,
                xpad, state, y_ref):
    q = SSD_CHUNK
    n = SSD_STATE

    xpad[HALO:HALO + q, :] = env["xbc"]
    acc = cb_ref[...]
    for k in range(SSD_CONV):
        off = HALO - (SSD_CONV - 1) + k
        acc = acc + cw_ref[k:k + 1, :] * xpad[off:off + q, :]
    xpad[0:HALO, :] = xpad[q:q + HALO, :]
    xc = _silu(acc)
    xs = xc[:, :SSD_WIDTH]
    bm = xc[:, SSD_WIDTH:SSD_WIDTH + SSD_GROUPS * n].astype(BF16)
    cm = xc[:, SSD_WIDTH + SSD_GROUPS * n:].astype(BF16)
    yield

    lane = lax.broadcasted_iota(jnp.int32, (q, SMALL_W), 1)
    head_lane = lane < SSD_HEADS
    dt = _softplus(env["small"] + dtb_ref[...])
    a = -jnp.exp(alog_ref[...])
    dt = jnp.where(head_lane, dt, 0.0)
    da = jnp.where(head_lane, dt * a, 0.0)
    cs = _dot(tri_ref[...], jnp.concatenate(_split3(da), axis=0))
    cs_t = cs.T
    cs_e = _dot(jnp.concatenate(_split3(cs), axis=1), exp_ref[...])
    dt_e = _dot(jnp.concatenate(_split3(dt), axis=1), exp_ref[...])
    cs_last = cs_e[q - 1:q, :]

    xdt = xs * dt_e
    xdt_b = xdt.astype(BF16)
    xdec_b = (xdt * jnp.exp(cs_last - cs_e)).astype(BF16)
    decay_in = jnp.exp(cs_e)
    decay_chunk = jnp.exp(cs_last)
    yield

    row = lax.broadcasted_iota(jnp.int32, (q, q), 0)
    col = lax.broadcasted_iota(jnp.int32, (q, q), 1)
    causal = row >= col
    first_half = col < SSD_HEAD_DIM

    y_parts = []
    for g in range(SSD_GROUPS):
        cmg = cm[:, g * n:(g + 1) * n]
        bmg = bm[:, g * n:(g + 1) * n]
        gs = slice(g * GROUP_W, (g + 1) * GROUP_W)
        scores = _dot_nt(cmg, bmg)
        y_off = _dot(cmg, state[g].astype(BF16)) * decay_in[:, gs]
        state[g] = decay_chunk[:, gs] * state[g] + _dot_tn(bmg, xdec_b[:, gs])
        diag = []
        for pair in range(HEADS_PER_GROUP // 2):
            h0 = g * HEADS_PER_GROUP + 2 * pair
            ms = []
            for h in (h0, h0 + 1):
                seg = cs[:, h:h + 1] - cs_t[h:h + 1, :]
                decay = jnp.exp(jnp.where(causal, seg, -jnp.inf))
                ms.append((scores * decay).astype(BF16))
            slab = xdt_b[:, h0 * SSD_HEAD_DIM:(h0 + 2) * SSD_HEAD_DIM]
            zero = jnp.zeros_like(slab)
            rhs = jnp.concatenate([jnp.where(first_half, slab, zero),
                                   jnp.where(first_half, zero, slab)], axis=0)
            diag.append(_dot(jnp.concatenate(ms, axis=1), rhs))
        y_parts.append(jnp.concatenate(diag, axis=1) + y_off)
        yield
    y = jnp.concatenate(y_parts, axis=1) + xs * dexp_ref[...]
    y = y * _silu(env["z"])
    outs = []
    for g in range(SSD_GROUPS):
        yg = y[:, g * GROUP_W:(g + 1) * GROUP_W]
        outs.append(yg * _rms_scale(yg))
    y_ref[...] = (jnp.concatenate(outs, axis=1) * nw_ref[...]).astype(BF16)
    yield


def _gla_stages(env, r, gw_ref, gb_ref, nw_ref, tri_ref, state, y_ref):
    c = GLA_CHUNK
    dk, dv = GLA_HEAD_K, GLA_HEAD_V
    row = lax.broadcasted_iota(jnp.int32, (c, c), 0)
    col = lax.broadcasted_iota(jnp.int32, (c, c), 1)
    causal = row >= col

    qf = env["q"][r, :] * (dk ** -0.5)
    kf = env["k"][r, :]
    logit = _dot(env["small"][r, :].astype(BF16), gw_ref[...]) + gb_ref[...]
    lg = -_softplus(-logit) / GLA_GATE_NORM
    bcum = _dot(tri_ref[...], jnp.concatenate(_split3(lg), axis=0))
    blast = bcum[c - 1:c, :]
    qt = (qf * jnp.exp(bcum)).astype(BF16)
    kt = (kf * jnp.exp(-bcum)).astype(BF16)
    kd = (kf * jnp.exp(blast - bcum)).astype(BF16)
    eblast = jnp.exp(blast)
    yield
    for h in range(GLA_HEADS):
        ks = slice(h * dk, (h + 1) * dk)
        vs = slice(h * dv, (h + 1) * dv)
        attn = jnp.where(causal, _dot_nt(qt[:, ks], kt[:, ks]), 0.0)
        vh = env["v"][r, vs]
        o = _dot(attn.astype(BF16), vh) + _dot_nt(qt[:, ks], state[h].astype(BF16))
        state[h] = eblast[:, ks] * state[h] + _dot_tn(vh, kd[:, ks])
        o = o * _rms_scale(o) * nw_ref[...]
        o = o * _silu(env["go"][r, vs])
        y_ref[r, vs] = o.astype(BF16)
        if h % 2 == 1:
            yield


def _mixer_kernel(x_ref, mod_ref, npre_ref, wbig_ref, wsmall_ref,
                  cw_ref, cb_ref, dtb_ref, alog_ref, dexp_ref, snw_ref, triq_ref, exp_ref,
                  gw_ref, gb_ref, gnw_ref, tric_ref,
                  ys_ref, yg_ref, xpad, sstate, gstate):
    @pl.when(pl.program_id(1) == 0)
    def _():
        xpad[0:HALO, :] = jnp.zeros((HALO, SSD_CONV_DIM), F32)
        sstate[...] = jnp.zeros(sstate.shape, F32)
        gstate[...] = jnp.zeros(gstate.shape, F32)

    x = x_ref[...]
    xn = (x * _rms_scale(x)) * npre_ref[...]
    hb = (xn * (1.0 + mod_ref[1:2, :]) + mod_ref[0:1, :]).astype(BF16)

    env = {"small": _dot(hb, wsmall_ref[...]), "xbc": _project(hb, wbig_ref, _XBC0, SSD_CONV_DIM)}
    ssd = _ssd_stages(env, cw_ref, cb_ref, dtb_ref, alog_ref, dexp_ref, snw_ref, triq_ref, exp_ref,
                      xpad, sstate, ys_ref)
    gla = [_gla_stages(env, slice(ci * GLA_CHUNK, (ci + 1) * GLA_CHUNK), gw_ref, gb_ref, gnw_ref,
                       tric_ref, gstate, yg_ref) for ci in range(SSD_CHUNK // GLA_CHUNK)]
    env["q"] = _project(hb, wbig_ref, _Q0, GLA_K_WIDTH)
    env["k"] = _project(hb, wbig_ref, _K0, GLA_K_WIDTH)
    next(ssd)
    env["v"] = _project(hb, wbig_ref, _V0, GLA_V_WIDTH).astype(BF16)
    next(gla[0])
    next(ssd)
    env["go"] = _project(hb, wbig_ref, _GO0, GLA_V_WIDTH)
    next(gla[0])
    next(ssd)
    env["z"] = _project(hb, wbig_ref, _Z0, SSD_WIDTH)
    next(gla[0])
    next(gla[1])
    next(ssd)
    next(gla[1])
    next(ssd)
    next(gla[1])


def _mixer_call(x, mod, npre, wbig, wsmall, ssd_consts, gla_consts):
    b, s, d = x.shape
    t = SSD_CHUNK
    row = lambda w: pl.BlockSpec((None, t, w), lambda i, j: (i, j, 0))
    consts = (npre, wbig, wsmall) + tuple(ssd_consts) + tuple(gla_consts)
    return pl.pallas_call(
        _mixer_kernel,
        grid=(b, s // t),
        in_specs=[row(d), pl.BlockSpec((None, ADA_CHUNKS, d), lambda i, j: (i, 0, 0))]
        + [_resident(a.shape) for a in consts],
        out_specs=[row(SSD_WIDTH), row(GLA_V_WIDTH)],
        out_shape=[jax.ShapeDtypeStruct((b, s, SSD_WIDTH), BF16),
                   jax.ShapeDtypeStruct((b, s, GLA_V_WIDTH), BF16)],
        scratch_shapes=[
            pltpu.VMEM((HALO + t, SSD_CONV_DIM), F32),
            pltpu.VMEM((SSD_GROUPS, SSD_STATE, GROUP_W), F32),
            pltpu.VMEM((GLA_HEADS, GLA_HEAD_V, GLA_HEAD_K), F32),
        ],
        compiler_params=_params("arbitrary", "arbitrary"),
        name="token_mixer",
    )(x, mod, *consts)


_UP_CHUNK = 384
_FFN_TM = 512


def _ffn_kernel(ys_ref, yg_ref, x_ref, mod_ref, npost1_ref, npre_ref, wout_ref, wup_ref,
                cw_ref, cb_ref, npost2_ref, wdown_ref, o_ref, upad, gate):
    tm = x_ref.shape[0]

    @pl.when(pl.program_id(1) == 0)
    def _():
        upad[0:HALO, :] = jnp.zeros((HALO, upad.shape[1]), F32)

    y = _dot(ys_ref[...], wout_ref[0:SSD_WIDTH, :]) + _dot(yg_ref[...], wout_ref[SSD_WIDTH:, :])
    yn = (y * _rms_scale(y)) * npost1_ref[...]
    x1 = x_ref[...] + mod_ref[2:3, :] * yn
    h = ((x1 * _rms_scale(x1)) * npre_ref[...]) * (1.0 + mod_ref[4:5, :]) + mod_ref[3:4, :]
    hb = h.astype(BF16)

    for c in range(FFN_HIDDEN // _UP_CHUNK):
        cols = slice(2 * c * _UP_CHUNK, 2 * (c + 1) * _UP_CHUNK)
        upad[HALO:HALO + tm, cols] = _dot(hb, wup_ref[:, cols])
        acc = cb_ref[:, cols]
        for k in range(FFN_CONV):
            off = HALO - (FFN_CONV - 1) + k
            acc = acc + cw_ref[k:k + 1, cols] * upad[off:off + tm, cols]
        upad[0:HALO, cols] = upad[tm:tm + HALO, cols]
        gate[:, c * _UP_CHUNK:(c + 1) * _UP_CHUNK] = (
            _silu(acc[:, :_UP_CHUNK]) * acc[:, _UP_CHUNK:]).astype(BF16)
    f = _dot(gate[...], wdown_ref[...])
    fn = (f * _rms_scale(f)) * npost2_ref[...]
    o_ref[...] = x1 + mod_ref[5:6, :] * fn


def _ffn_call(ys, yg, x, mod, npost1, npre, wout, wup, cw, cb, npost2, wdown):
    b, s, d = x.shape
    tm = _FFN_TM
    row = lambda w: pl.BlockSpec((None, tm, w), lambda i, j: (i, j, 0))
    consts = (npost1, npre, wout, wup, cw, cb, npost2, wdown)
    return pl.pallas_call(
        _ffn_kernel,
        grid=(b, s // tm),
        in_specs=[row(SSD_WIDTH), row(GLA_V_WIDTH), row(d),
                  pl.BlockSpec((None, ADA_CHUNKS, d), lambda i, j: (i, 0, 0))]
        + [_resident(a.shape) for a in consts],
        out_specs=row(d),
        out_shape=jax.ShapeDtypeStruct((b, s, d), F32),
        scratch_shapes=[pltpu.VMEM((HALO + tm, 2 * FFN_HIDDEN), F32),
                        pltpu.VMEM((tm, FFN_HIDDEN), BF16)],
        compiler_params=_params("arbitrary", "arbitrary"),
        name="channel_mixer",
    )(ys, yg, x, mod, *consts)


def _pair_chunks(a):
    lead = a.shape[:-1]
    n = FFN_HIDDEN // _UP_CHUNK
    return a.reshape(*lead, 2, n, _UP_CHUNK).swapaxes(-3, -2).reshape(*lead, 2 * FFN_HIDDEN)


def _lane_pad(v, width):
    return jnp.pad(v, ((0, 0), (0, width - v.shape[1])))


def _constants():
    q, c = SSD_CHUNK, GLA_CHUNK
    tri_q = np.tril(np.ones((q, q), np.float32))
    tri_c = np.tril(np.ones((c, c), np.float32))
    expand = np.zeros((SMALL_W, SSD_WIDTH), np.float32)
    for h in range(SSD_HEADS):
        expand[h, h * SSD_HEAD_DIM:(h + 1) * SSD_HEAD_DIM] = 1.0
    return (jnp.asarray(np.tile(tri_q, (1, 3)), BF16),
            jnp.asarray(np.tile(tri_c, (1, 3)), BF16),
            jnp.asarray(np.tile(expand, (3, 1)), BF16))


def kernel(x, c, w_ada, b_ada, norm_mix_pre, norm_mix_post, norm_ffn_pre, norm_ffn_post, w_in, ssd_conv_w, ssd_conv_b, ssd_dt_bias, ssd_a_log, ssd_d, ssd_norm, gla_gate_w, gla_gate_b, gla_norm, w_out, ffn_up, ffn_conv_w, ffn_conv_b, ffn_down):
    bsz, seqlen, d = x.shape
    depth = w_ada.shape[0]
    tri_q3, tri_c3, expand3 = _constants()
    c_pad = jnp.pad(c, ((0, SUBLANES - bsz % SUBLANES), (0, 0))) if bsz % SUBLANES else c

    sizes = [SSD_WIDTH, SSD_CONV_DIM, SSD_HEADS, GLA_K_WIDTH, GLA_K_WIDTH, GLA_V_WIDTH,
             GLA_GATE_RANK, GLA_V_WIDTH]
    o = np.concatenate([[0], np.cumsum(sizes)])

    for i in range(depth):
        mod = _ada_call(c_pad, w_ada[i], b_ada[i][None, :])[:bsz].reshape(bsz, ADA_CHUNKS, d)

        wi = w_in[i].astype(BF16)
        wbig = jnp.concatenate([wi[:, o[0]:o[2]], wi[:, o[3]:o[6]], wi[:, o[7]:o[8]]], axis=1)
        wsmall = jnp.concatenate(
            [wi[:, o[2]:o[3]], wi[:, o[6]:o[7]],
             jnp.zeros((d, SMALL_W - SSD_HEADS - GLA_GATE_RANK), BF16)], axis=1)
        ssd_consts = (ssd_conv_w[i], ssd_conv_b[i][None, :],
                      _lane_pad(ssd_dt_bias[i][None, :], SMALL_W), _lane_pad(ssd_a_log[i][None, :], SMALL_W),
                      jnp.repeat(ssd_d[i], SSD_HEAD_DIM)[None, :], ssd_norm[i][None, :], tri_q3, expand3)
        gw = jnp.zeros((SMALL_W, GLA_K_WIDTH), F32).at[SSD_HEADS:SSD_HEADS + GLA_GATE_RANK].set(
            gla_gate_w[i]).astype(BF16)
        gla_consts = (gw, gla_gate_b[i][None, :], gla_norm[i][None, :], tri_c3)
        y_ssd, y_gla = _mixer_call(x, mod, norm_mix_pre[i][None, :], wbig, wsmall, ssd_consts, gla_consts)

        x = _ffn_call(y_ssd, y_gla, x, mod, norm_mix_post[i][None, :], norm_ffn_pre[i][None, :],
                      w_out[i].astype(BF16), _pair_chunks(ffn_up[i].astype(BF16)), _pair_chunks(ffn_conv_w[i]),
                      _pair_chunks(ffn_conv_b[i][None, :]), norm_ffn_post[i][None, :], ffn_down[i].astype(BF16))
    return x
```

```python
import numpy as np
import jax
import jax.numpy as jnp
from jax import lax
from jax.experimental import pallas as pl
from jax.experimental.pallas import tpu as pltpu

F32 = jnp.float32
BF16 = jnp.bfloat16

D_MODEL = 1024
MIX_WIDTH = 2 * D_MODEL
SSD_WIDTH = MIX_WIDTH // 2
GLA_V_WIDTH = MIX_WIDTH - SSD_WIDTH
SSD_HEAD_DIM = 64
SSD_HEADS = SSD_WIDTH // SSD_HEAD_DIM
SSD_GROUPS = 2
SSD_STATE = 128
SSD_CONV = 4
SSD_CHUNK = 128
SSD_CONV_DIM = SSD_WIDTH + 2 * SSD_GROUPS * SSD_STATE
GLA_HEADS = 4
GLA_K_WIDTH = GLA_V_WIDTH // 2
GLA_HEAD_K = GLA_K_WIDTH // GLA_HEADS
GLA_HEAD_V = GLA_V_WIDTH // GLA_HEADS
GLA_GATE_RANK = 16
GLA_GATE_NORM = 16.0
GLA_CHUNK = 64
FFN_HIDDEN = int(round(8 * D_MODEL / 3 / 128)) * 128
FFN_CONV = 3
ADA_CHUNKS = 6
NORM_EPS = 1e-6

LANES = 128
SUBLANES = 8
VMEM_LIMIT_BYTES = 58 * 1024 * 1024

SMALL_W = LANES
GROUP_W = SSD_WIDTH // SSD_GROUPS
HEADS_PER_GROUP = SSD_HEADS // SSD_GROUPS
HALO = SUBLANES


def _silu(x):
    return x * jax.nn.sigmoid(x)


def _softplus(x):
    return jnp.maximum(x, 0.0) + jnp.log(1.0 + jnp.exp(-jnp.abs(x)))


def _rms_scale(x):
    return lax.rsqrt(jnp.mean(x * x, axis=-1, keepdims=True) + NORM_EPS)


def _split3(x):
    hi = x.astype(BF16)
    r1 = x - hi.astype(F32)
    mid = r1.astype(BF16)
    lo = (r1 - mid.astype(F32)).astype(BF16)
    return hi, mid, lo


def _dot(a, b):
    return jnp.dot(a, b, preferred_element_type=F32)


def _dot_nt(a, b):
    return lax.dot_general(a, b, (((1,), (1,)), ((), ())), preferred_element_type=F32)


def _dot_tn(a, b):
    return lax.dot_general(a, b, (((0,), (0,)), ((), ())), preferred_element_type=F32)


def _resident(shape):
    nd = len(shape)
    return pl.BlockSpec(shape, lambda *_: (0,) * nd, pipeline_mode=pl.Buffered(1))


def _params(*sem):
    return pltpu.CompilerParams(dimension_semantics=sem, vmem_limit_bytes=VMEM_LIMIT_BYTES)


def _ada_kernel(c_ref, w_ref, b_ref, o_ref):
    ca = _silu(c_ref[...]).astype(BF16)
    o_ref[...] = _dot(ca, w_ref[...].astype(BF16)) + b_ref[...]


def _ada_call(c_pad, w, b):
    rows, d = c_pad.shape
    n = w.shape[1]
    tn = 1024
    return pl.pallas_call(
        _ada_kernel,
        grid=(n // tn,),
        in_specs=[
            pl.BlockSpec((rows, d), lambda j: (0, 0)),
            pl.BlockSpec((d, tn), lambda j: (0, j)),
            pl.BlockSpec((1, tn), lambda j: (0, j)),
        ],
        out_specs=pl.BlockSpec((rows, tn), lambda j: (0, j)),
        out_shape=jax.ShapeDtypeStruct((rows, n), F32),
        compiler_params=_params("arbitrary"),
        name="ada_mod",
    )(c_pad, w, b)


_Z0 = 0
_XBC0 = _Z0 + SSD_WIDTH
_Q0 = _XBC0 + SSD_CONV_DIM
_K0 = _Q0 + GLA_K_WIDTH
_V0 = _K0 + GLA_K_WIDTH
_GO0 = _V0 + GLA_V_WIDTH
_PROJ_W = _GO0 + GLA_V_WIDTH
_PROJ_CHUNK = 512


def _project(hb, w_ref, c0, width):
    parts = [_dot(hb, w_ref[:, c:c + min(_PROJ_CHUNK, c0 + width - c)])
             for c in range(c0, c0 + width, _PROJ_CHUNK)]
    return parts[0] if len(parts) == 1 else jnp.concatenate(parts, axis=1)


def _ssd_stages(env, cw_ref, cb_ref, dtb_ref, alog_ref, dexp_ref, nw_ref, tri_ref, exp_ref,
                xpad, state, y_ref):
    q = SSD_CHUNK
    n = SSD_STATE

    xpad[HALO:HALO + q, :] = env["xbc"]
    acc = cb_ref[...]
    for k in range(SSD_CONV):
        off = HALO - (SSD_CONV - 1) + k
        acc = acc + cw_ref[k:k + 1, :] * xpad[off:off + q, :]
    xpad[0:HALO, :] = xpad[q:q + HALO, :]
    xc = _silu(acc)
    xs = xc[:, :SSD_WIDTH]
    bm = xc[:, SSD_WIDTH:SSD_WIDTH + SSD_GROUPS * n].astype(BF16)
    cm = xc[:, SSD_WIDTH + SSD_GROUPS * n:].astype(BF16)
    yield

    lane = lax.broadcasted_iota(jnp.int32, (q, SMALL_W), 1)
    head_lane = lane < SSD_HEADS
    dt = _softplus(env["small"] + dtb_ref[...])
    a = -jnp.exp(alog_ref[...])
    dt = jnp.where(head_lane, dt, 0.0)
    da = jnp.where(head_lane, dt * a, 0.0)
    cs = _dot(tri_ref[...], jnp.concatenate(_split3(da), axis=0))
    cs_t = cs.T
    cs_e = _dot(jnp.concatenate(_split3(cs), axis=1), exp_ref[...])
    dt_e = _dot(jnp.concatenate(_split3(dt), axis=1), exp_ref[...])
    cs_last = cs_e[q - 1:q, :]

    xdt = xs * dt_e
    xdt_b = xdt.astype(BF16)
    xdec_b = (xdt * jnp.exp(cs_last - cs_e)).astype(BF16)
    decay_in = jnp.exp(cs_e)
    decay_chunk = jnp.exp(cs_last)
    yield

    row = lax.broadcasted_iota(jnp.int32, (q, q), 0)
    col = lax.broadcasted_iota(jnp.int32, (q, q), 1)
    causal = row >= col
    first_half = col < SSD_HEAD_DIM

    y_parts = []
    for g in range(SSD_GROUPS):
        cmg = cm[:, g * n:(g + 1) * n]
        bmg = bm[:, g * n:(g + 1) * n]
        gs = slice(g * GROUP_W, (g + 1) * GROUP_W)
        scores = _dot_nt(cmg, bmg)
        y_off = _dot(cmg, state[g].astype(BF16)) * decay_in[:, gs]
        state[g] = decay_chunk[:, gs] * state[g] + _dot_tn(bmg, xdec_b[:, gs])
        diag = []
        for pair in range(HEADS_PER_GROUP // 2):
            h0 = g * HEADS_PER_GROUP + 2 * pair
            ms = []
            for h in (h0, h0 + 1):
                seg = cs[:, h:h + 1] - cs_t[h:h + 1, :]
                decay = jnp.exp(jnp.where(causal, seg, -jnp.inf))
                ms.append((scores * decay).astype(BF16))
            slab = xdt_b[:, h0 * SSD_HEAD_DIM:(h0 + 2) * SSD_HEAD_DIM]
            zero = jnp.zeros_like(slab)
            rhs = jnp.concatenate([jnp.where(first_half, slab, zero),
                                   jnp.where(first_half, zero, slab)], axis=0)
            diag.append(_dot(jnp.concatenate(ms, axis=1), rhs))
        y_parts.append(jnp.concatenate(diag, axis=1) + y_off)
        yield
    y = jnp.concatenate(y_parts, axis=1) + xs * dexp_ref[...]
    y = y * _silu(env["z"])
    outs = []
    for g in range(SSD_GROUPS):
        yg = y[:, g * GROUP_W:(g + 1) * GROUP_W]
        outs.append(yg * _rms_scale(yg))
    y_ref[...] = (jnp.concatenate(outs, axis=1) * nw_ref[...]).astype(BF16)
    yield


def _gla_stages(env, r, gw_ref, gb_ref, nw_ref, tri_ref, state, y_ref):
    c = GLA_CHUNK
    dk, dv = GLA_HEAD_K, GLA_HEAD_V
    row = lax.broadcasted_iota(jnp.int32, (c, c), 0)
    col = lax.broadcasted_iota(jnp.int32, (c, c), 1)
    causal = row >= col

    qf = env["q"][r, :] * (dk ** -0.5)
    kf = env["k"][r, :]
    logit = _dot(env["small"][r, :].astype(BF16), gw_ref[...]) + gb_ref[...]
    lg = -_softplus(-logit) / GLA_GATE_NORM
    bcum = _dot(tri_ref[...], jnp.concatenate(_split3(lg), axis=0))
    blast = bcum[c - 1:c, :]
    qt = (qf * jnp.exp(bcum)).astype(BF16)
    kt = (kf * jnp.exp(-bcum)).astype(BF16)
    kd = (kf * jnp.exp(blast - bcum)).astype(BF16)
    eblast = jnp.exp(blast)
    yield
    for h in range(GLA_HEADS):
        ks = slice(h * dk, (h + 1) * dk)
        vs = slice(h * dv, (h + 1) * dv)
        attn = jnp.where(causal, _dot_nt(qt[:, ks], kt[:, ks]), 0.0)
        vh = env["v"][r, vs]
        o = _dot(attn.astype(BF16), vh) + _dot_nt(qt[:, ks], state[h].astype(BF16))
        state[h] = eblast[:, ks] * state[h] + _dot_tn(vh, kd[:, ks])
        o = o * _rms_scale(o) * nw_ref[...]
        o = o * _silu(env["go"][r, vs])
        y_ref[r, vs] = o.astype(BF16)
        if h % 2 == 1:
            yield


def _mixer_kernel(x_ref, mod_ref, npre_ref, wbig_ref, wsmall_ref,
                  cw_ref, cb_ref, dtb_ref, alog_ref, dexp_ref, snw_ref, triq_ref, exp_ref,
                  gw_ref, gb_ref, gnw_ref, tric_ref,
                  ys_ref, yg_ref, xpad, sstate, gstate):
    @pl.when(pl.program_id(1) == 0)
    def _():
        xpad[0:HALO, :] = jnp.zeros((HALO, SSD_CONV_DIM), F32)
        sstate[...] = jnp.zeros(sstate.shape, F32)
        gstate[...] = jnp.zeros(gstate.shape, F32)

    x = x_ref[...]
    xn = (x * _rms_scale(x)) * npre_ref[...]
    hb = (xn * (1.0 + mod_ref[1:2, :]) + mod_ref[0:1, :]).astype(BF16)

    env = {"small": _dot(hb, wsmall_ref[...]), "xbc": _project(hb, wbig_ref, _XBC0, SSD_CONV_DIM)}
    ssd = _ssd_stages(env, cw_ref, cb_ref, dtb_ref, alog_ref, dexp_ref, snw_ref, triq_ref, exp_ref,
                      xpad, sstate, ys_ref)
    gla = [_gla_stages(env, slice(ci * GLA_CHUNK, (ci + 1) * GLA_CHUNK), gw_ref, gb_ref, gnw_ref,
                       tric_ref, gstate, yg_ref) for ci in range(SSD_CHUNK // GLA_CHUNK)]
    env["q"] = _project(hb, wbig_ref, _Q0, GLA_K_WIDTH)
    env["k"] = _project(hb, wbig_ref, _K0, GLA_K_WIDTH)
    next(ssd)
    env["v"] = _project(hb, wbig_ref, _V0, GLA_V_WIDTH).astype(BF16)
    next(gla[0])
    next(ssd)
    env["go"] = _project(hb, wbig_ref, _GO0, GLA_V_WIDTH)
    next(gla[0])
    next(ssd)
    env["z"] = _project(hb, wbig_ref, _Z0, SSD_WIDTH)
    next(gla[0])
    next(gla[1])
    next(ssd)
    next(gla[1])
    next(ssd)
    next(gla[1])


def _mixer_call(x, mod, npre, wbig, wsmall, ssd_consts, gla_consts):
    b, s, d = x.shape
    t = SSD_CHUNK
    row = lambda w: pl.BlockSpec((None, t, w), lambda i, j: (i, j, 0))
    consts = (npre, wbig, wsmall) + tuple(ssd_consts) + tuple(gla_consts)
    return pl.pallas_call(
        _mixer_kernel,
        grid=(b, s // t),
        in_specs=[row(d), pl.BlockSpec((None, ADA_CHUNKS, d), lambda i, j: (i, 0, 0))]
        + [_resident(a.shape) for a in consts],
        out_specs=[row(SSD_WIDTH), row(GLA_V_WIDTH)],
        out_shape=[jax.ShapeDtypeStruct((b, s, SSD_WIDTH), BF16),
                   jax.ShapeDtypeStruct((b, s, GLA_V_WIDTH), BF16)],
        scratch_shapes=[
            pltpu.VMEM((HALO + t, SSD_CONV_DIM), F32),
            pltpu.VMEM((SSD_GROUPS, SSD_STATE, GROUP_W), F32),
            pltpu.VMEM((GLA_HEADS, GLA_HEAD_V, GLA_HEAD_K), F32),
        ],
        compiler_params=_params("arbitrary", "arbitrary"),
        name="token_mixer",
    )(x, mod, *consts)


_UP_CHUNK = 384
_UP_DOT = 768
_FFN_TM = 512


def _ffn_kernel(ys_ref, yg_ref, x_ref, mod_ref, npost1_ref, npre_ref, wout_ref, wup_ref,
                cw_ref, cb_ref, npost2_ref, wdown_ref, o_ref, upad, gate):
    tm = x_ref.shape[0]

    @pl.when(pl.program_id(1) == 0)
    def _():
        upad[0:HALO, :] = jnp.zeros((HALO, upad.shape[1]), F32)

    y = _dot(ys_ref[...], wout_ref[0:SSD_WIDTH, :]) + _dot(yg_ref[...], wout_ref[SSD_WIDTH:, :])
    yn = (y * _rms_scale(y)) * npost1_ref[...]
    x1 = x_ref[...] + mod_ref[2:3, :] * yn
    h = ((x1 * _rms_scale(x1)) * npre_ref[...]) * (1.0 + mod_ref[4:5, :]) + mod_ref[3:4, :]
    hb = h.astype(BF16)

    done = set()

    def up(col):
        p = col // _UP_DOT
        if p not in done:
            done.add(p)
            cols = slice(p * _UP_DOT, (p + 1) * _UP_DOT)
            upad[HALO:HALO + tm, cols] = _dot(hb, wup_ref[:, cols])

    def conv(c0):
        cols = slice(c0, c0 + _UP_CHUNK)
        up(c0)
        up(c0 + _UP_CHUNK - 1)
        acc = cb_ref[:, cols]
        for k in range(FFN_CONV):
            off = HALO - (FFN_CONV - 1) + k
            acc = acc + cw_ref[k:k + 1, cols] * upad[off:off + tm, cols]
        upad[0:HALO, cols] = upad[tm:tm + HALO, cols]
        return acc

    for c in range(0, FFN_HIDDEN, _UP_CHUNK):
        gate[:, c:c + _UP_CHUNK] = (_silu(conv(c)) * conv(FFN_HIDDEN + c)).astype(BF16)
    f = _dot(gate[...], wdown_ref[...])
    fn = (f * _rms_scale(f)) * npost2_ref[...]
    o_ref[...] = x1 + mod_ref[5:6, :] * fn


def _ffn_call(ys, yg, x, mod, npost1, npre, wout, wup, cw, cb, npost2, wdown):
    b, s, d = x.shape
    tm = _FFN_TM
    row = lambda w: pl.BlockSpec((None, tm, w), lambda i, j: (i, j, 0))
    consts = (npost1, npre, wout, wup, cw, cb, npost2, wdown)
    return pl.pallas_call(
        _ffn_kernel,
        grid=(b, s // tm),
        in_specs=[row(SSD_WIDTH), row(GLA_V_WIDTH), row(d),
                  pl.BlockSpec((None, ADA_CHUNKS, d), lambda i, j: (i, 0, 0))]
        + [_resident(a.shape) for a in consts],
        out_specs=row(d),
        out_shape=jax.ShapeDtypeStruct((b, s, d), F32),
        scratch_shapes=[pltpu.VMEM((HALO + tm, 2 * FFN_HIDDEN), F32),
                        pltpu.VMEM((tm, FFN_HIDDEN), BF16)],
        compiler_params=_params("arbitrary", "arbitrary"),
        name="channel_mixer",
    )(ys, yg, x, mod, *consts)


def _lane_pad(v, width):
    return jnp.pad(v, ((0, 0), (0, width - v.shape[1])))


def _constants():
    q, c = SSD_CHUNK, GLA_CHUNK
    tri_q = np.tril(np.ones((q, q), np.float32))
    tri_c = np.tril(np.ones((c, c), np.float32))
    expand = np.zeros((SMALL_W, SSD_WIDTH), np.float32)
    for h in range(SSD_HEADS):
        expand[h, h * SSD_HEAD_DIM:(h + 1) * SSD_HEAD_DIM] = 1.0
    return (jnp.asarray(np.tile(tri_q, (1, 3)), BF16),
            jnp.asarray(np.tile(tri_c, (1, 3)), BF16),
            jnp.asarray(np.tile(expand, (3, 1)), BF16))


def kernel(x, c, w_ada, b_ada, norm_mix_pre, norm_mix_post, norm_ffn_pre, norm_ffn_post, w_in, ssd_conv_w, ssd_conv_b, ssd_dt_bias, ssd_a_log, ssd_d, ssd_norm, gla_gate_w, gla_gate_b, gla_norm, w_out, ffn_up, ffn_conv_w, ffn_conv_b, ffn_down):
    bsz, seqlen, d = x.shape
    depth = w_ada.shape[0]
    tri_q3, tri_c3, expand3 = _constants()
    c_pad = jnp.pad(c, ((0, SUBLANES - bsz % SUBLANES), (0, 0))) if bsz % SUBLANES else c

    sizes = [SSD_WIDTH, SSD_CONV_DIM, SSD_HEADS, GLA_K_WIDTH, GLA_K_WIDTH, GLA_V_WIDTH,
             GLA_GATE_RANK, GLA_V_WIDTH]
    o = np.concatenate([[0], np.cumsum(sizes)])

    for i in range(depth):
        mod = _ada_call(c_pad, w_ada[i], b_ada[i][None, :])[:bsz].reshape(bsz, ADA_CHUNKS, d)

        wi = w_in[i].astype(BF16)
        wbig = jnp.concatenate([wi[:, o[0]:o[2]], wi[:, o[3]:o[6]], wi[:, o[7]:o[8]]], axis=1)
        wsmall = jnp.concatenate(
            [wi[:, o[2]:o[3]], wi[:, o[6]:o[7]],
             jnp.zeros((d, SMALL_W - SSD_HEADS - GLA_GATE_RANK), BF16)], axis=1)
        ssd_consts = (ssd_conv_w[i], ssd_conv_b[i][None, :],
                      _lane_pad(ssd_dt_bias[i][None, :], SMALL_W), _lane_pad(ssd_a_log[i][None, :], SMALL_W),
                      jnp.repeat(ssd_d[i], SSD_HEAD_DIM)[None, :], ssd_norm[i][None, :], tri_q3, expand3)
        gw = jnp.zeros((SMALL_W, GLA_K_WIDTH), F32).at[SSD_HEADS:SSD_HEADS + GLA_GATE_RANK].set(
            gla_gate_w[i]).astype(BF16)
        gla_consts = (gw, gla_gate_b[i][None, :], gla_norm[i][None, :], tri_c3)
        y_ssd, y_gla = _mixer_call(x, mod, norm_mix_pre[i][None, :], wbig, wsmall, ssd_consts, gla_consts)

        x = _ffn_call(y_ssd, y_gla, x, mod, norm_mix_post[i][None, :], norm_ffn_pre[i][None, :],
                      w_out[i].astype(BF16), ffn_up[i].astype(BF16), ffn_conv_w[i], ffn_conv_b[i][None, :],
                      norm_ffn_post[i][None, :], ffn_down[i].astype(BF16))
    return x
```

```python
import functools

import numpy as np
import jax
import jax.numpy as jnp
from jax import lax
from jax.experimental import pallas as pl
from jax.experimental.pallas import tpu as pltpu

F32 = jnp.float32
BF16 = jnp.bfloat16

D_MODEL = 1024
MIX_WIDTH = 2 * D_MODEL
SSD_WIDTH = MIX_WIDTH // 2
GLA_V_WIDTH = MIX_WIDTH - SSD_WIDTH
SSD_HEAD_DIM = 64
SSD_HEADS = SSD_WIDTH // SSD_HEAD_DIM
SSD_GROUPS = 2
SSD_STATE = 128
SSD_CONV = 4
SSD_CHUNK = 128
SSD_CONV_DIM = SSD_WIDTH + 2 * SSD_GROUPS * SSD_STATE
GLA_HEADS = 4
GLA_K_WIDTH = GLA_V_WIDTH // 2
GLA_HEAD_K = GLA_K_WIDTH // GLA_HEADS
GLA_HEAD_V = GLA_V_WIDTH // GLA_HEADS
GLA_GATE_RANK = 16
GLA_GATE_NORM = 16.0
GLA_CHUNK = 64
FFN_HIDDEN = int(round(8 * D_MODEL / 3 / 128)) * 128
FFN_CONV = 3
ADA_CHUNKS = 6
NORM_EPS = 1e-6

LANES = 128
SUBLANES = 8
VMEM_LIMIT_BYTES = 58 * 1024 * 1024

SMALL_W = LANES
GROUP_W = SSD_WIDTH // SSD_GROUPS
HEADS_PER_GROUP = SSD_HEADS // SSD_GROUPS
HALO = SUBLANES


def _silu(x):
    return x * jax.nn.sigmoid(x)


def _softplus(x):
    return jnp.maximum(x, 0.0) + jnp.log(1.0 + jnp.exp(-jnp.abs(x)))


def _rms_scale(x):
    return lax.rsqrt(jnp.mean(x * x, axis=-1, keepdims=True) + NORM_EPS)


def _split3(x):
    hi = x.astype(BF16)
    r1 = x - hi.astype(F32)
    mid = r1.astype(BF16)
    lo = (r1 - mid.astype(F32)).astype(BF16)
    return hi, mid, lo


def _dot(a, b):
    return jnp.dot(a, b, preferred_element_type=F32)


def _dot_nt(a, b):
    return lax.dot_general(a, b, (((1,), (1,)), ((), ())), preferred_element_type=F32)


def _dot_tn(a, b):
    return lax.dot_general(a, b, (((0,), (0,)), ((), ())), preferred_element_type=F32)


def _resident(shape):
    nd = len(shape)
    return pl.BlockSpec(shape, lambda *_: (0,) * nd, pipeline_mode=pl.Buffered(1))


def _params(*sem):
    return pltpu.CompilerParams(dimension_semantics=sem, vmem_limit_bytes=VMEM_LIMIT_BYTES)


def _ada_kernel(c_ref, w_ref, b_ref, o_ref):
    ca = _silu(c_ref[...]).astype(BF16)
    o_ref[...] = _dot(ca, w_ref[...].astype(BF16)) + b_ref[...]


def _ada_call(c_pad, w, b):
    rows, d = c_pad.shape
    n = w.shape[1]
    tn = 1024
    return pl.pallas_call(
        _ada_kernel,
        grid=(n // tn,),
        in_specs=[
            pl.BlockSpec((rows, d), lambda j: (0, 0)),
            pl.BlockSpec((d, tn), lambda j: (0, j)),
            pl.BlockSpec((1, tn), lambda j: (0, j)),
        ],
        out_specs=pl.BlockSpec((rows, tn), lambda j: (0, j)),
        out_shape=jax.ShapeDtypeStruct((rows, n), F32),
        compiler_params=_params("arbitrary"),
        name="ada_mod",
    )(c_pad, w, b)


def _reorder_kernel(w_ref, big_ref, small_ref, *, offs):
    w = w_ref[...]
    rows = w.shape[0]
    big = [w[:, offs[0]:offs[2]], w[:, offs[3]:offs[6]], w[:, offs[7]:offs[8]]]
    big_ref[...] = jnp.concatenate(big, axis=1).astype(BF16)
    pad = jnp.zeros((rows, SMALL_W - SSD_HEADS - GLA_GATE_RANK), F32)
    small_ref[...] = jnp.concatenate([w[:, offs[2]:offs[3]], w[:, offs[6]:offs[7]], pad], axis=1).astype(BF16)


def _reorder_call(w, offs):
    d, n_in = w.shape
    tr = 128
    return pl.pallas_call(
        functools.partial(_reorder_kernel, offs=tuple(int(v) for v in offs)),
        grid=(d // tr,),
        in_specs=[pl.BlockSpec((tr, n_in), lambda r: (r, 0))],
        out_specs=[pl.BlockSpec((tr, _PROJ_W), lambda r: (r, 0)), pl.BlockSpec((tr, SMALL_W), lambda r: (r, 0))],
        out_shape=[jax.ShapeDtypeStruct((d, _PROJ_W), BF16), jax.ShapeDtypeStruct((d, SMALL_W), BF16)],
        compiler_params=_params("arbitrary"),
        name="reorder_w_in",
    )(w)


_Z0 = 0
_XBC0 = _Z0 + SSD_WIDTH
_Q0 = _XBC0 + SSD_CONV_DIM
_K0 = _Q0 + GLA_K_WIDTH
_V0 = _K0 + GLA_K_WIDTH
_GO0 = _V0 + GLA_V_WIDTH
_PROJ_W = _GO0 + GLA_V_WIDTH
_PROJ_CHUNK = 512


def _project(hb, w_ref, c0, width):
    parts = [_dot(hb, w_ref[:, c:c + min(_PROJ_CHUNK, c0 + width - c)])
             for c in range(c0, c0 + width, _PROJ_CHUNK)]
    return parts[0] if len(parts) == 1 else jnp.concatenate(parts, axis=1)


def _ssd_stages(env, cw_ref, cb_ref, dtb_ref, alog_ref, dexp_ref, nw_ref, tri_ref, exp_ref,
                xpad, state, y_ref):
    q = SSD_CHUNK
    n = SSD_STATE

    xpad[HALO:HALO + q, :] = env["xbc"]
    acc = cb_ref[...]
    for k in range(SSD_CONV):
        off = HALO - (SSD_CONV - 1) + k
        acc = acc + cw_ref[k:k + 1, :] * xpad[off:off + q, :]
    xpad[0:HALO, :] = xpad[q:q + HALO, :]
    xc = _silu(acc)
    xs = xc[:, :SSD_WIDTH]
    bm = xc[:, SSD_WIDTH:SSD_WIDTH + SSD_GROUPS * n].astype(BF16)
    cm = xc[:, SSD_WIDTH + SSD_GROUPS * n:].astype(BF16)
    yield

    lane = lax.broadcasted_iota(jnp.int32, (q, SMALL_W), 1)
    head_lane = lane < SSD_HEADS
    dt = _softplus(env["small"] + dtb_ref[...])
    a = -jnp.exp(alog_ref[...])
    dt = jnp.where(head_lane, dt, 0.0)
    da = jnp.where(head_lane, dt * a, 0.0)
    cs = _dot(tri_ref[...], jnp.concatenate(_split3(da), axis=0))
    cs_t = cs.T
    cs_e = _dot(jnp.concatenate(_split3(cs), axis=1), exp_ref[...])
    dt_e = _dot(jnp.concatenate(_split3(dt), axis=1), exp_ref[...])
    cs_last = cs_e[q - 1:q, :]

    xdt = xs * dt_e
    xdt_b = xdt.astype(BF16)
    xdec_b = (xdt * jnp.exp(cs_last - cs_e)).astype(BF16)
    decay_in = jnp.exp(cs_e)
    decay_chunk = jnp.exp(cs_last)
    yield

    row = lax.broadcasted_iota(jnp.int32, (q, q), 0)
    col = lax.broadcasted_iota(jnp.int32, (q, q), 1)
    causal = row >= col
    first_half = col < SSD_HEAD_DIM

    y_parts = []
    for g in range(SSD_GROUPS):
        cmg = cm[:, g * n:(g + 1) * n]
        bmg = bm[:, g * n:(g + 1) * n]
        gs = slice(g * GROUP_W, (g + 1) * GROUP_W)
        scores = _dot_nt(cmg, bmg)
        y_off = _dot(cmg, state[g].astype(BF16)) * decay_in[:, gs]
        state[g] = decay_chunk[:, gs] * state[g] + _dot_tn(bmg, xdec_b[:, gs])
        diag = []
        for pair in range(HEADS_PER_GROUP // 2):
            h0 = g * HEADS_PER_GROUP + 2 * pair
            ms = []
            for h in (h0, h0 + 1):
                seg = cs[:, h:h + 1] - cs_t[h:h + 1, :]
                decay = jnp.exp(jnp.where(causal, seg, -jnp.inf))
                ms.append((scores * decay).astype(BF16))
            slab = xdt_b[:, h0 * SSD_HEAD_DIM:(h0 + 2) * SSD_HEAD_DIM]
            zero = jnp.zeros_like(slab)
            rhs = jnp.concatenate([jnp.where(first_half, slab, zero),
                                   jnp.where(first_half, zero, slab)], axis=0)
            diag.append(_dot(jnp.concatenate(ms, axis=1), rhs))
        y_parts.append(jnp.concatenate(diag, axis=1) + y_off)
        yield
    y = jnp.concatenate(y_parts, axis=1) + xs * dexp_ref[...]
    y = y * _silu(env["z"])
    outs = []
    for g in range(SSD_GROUPS):
        yg = y[:, g * GROUP_W:(g + 1) * GROUP_W]
        outs.append(yg * _rms_scale(yg))
    y_ref[...] = (jnp.concatenate(outs, axis=1) * nw_ref[...]).astype(BF16)
    yield


def _gla_stages(env, r, gw_ref, gb_ref, nw_ref, tri_ref, state, y_ref):
    c = GLA_CHUNK
    dk, dv = GLA_HEAD_K, GLA_HEAD_V
    row = lax.broadcasted_iota(jnp.int32, (c, c), 0)
    col = lax.broadcasted_iota(jnp.int32, (c, c), 1)
    causal = row >= col

    qf = env["q"][r, :] * (dk ** -0.5)
    kf = env["k"][r, :]
    logit = _dot(env["small"][r, :].astype(BF16), gw_ref[...]) + gb_ref[...]
    lg = -_softplus(-logit) / GLA_GATE_NORM
    bcum = _dot(tri_ref[...], jnp.concatenate(_split3(lg), axis=0))
    blast = bcum[c - 1:c, :]
    qt = (qf * jnp.exp(bcum)).astype(BF16)
    kt = (kf * jnp.exp(-bcum)).astype(BF16)
    kd = (kf * jnp.exp(blast - bcum)).astype(BF16)
    eblast = jnp.exp(blast)
    yield
    for h in range(GLA_HEADS):
        ks = slice(h * dk, (h + 1) * dk)
        vs = slice(h * dv, (h + 1) * dv)
        attn = jnp.where(causal, _dot_nt(qt[:, ks], kt[:, ks]), 0.0)
        vh = env["v"][r, vs]
        o = _dot(attn.astype(BF16), vh) + _dot_nt(qt[:, ks], state[h].astype(BF16))
        state[h] = eblast[:, ks] * state[h] + _dot_tn(vh, kd[:, ks])
        o = o * _rms_scale(o) * nw_ref[...]
        o = o * _silu(env["go"][r, vs])
        y_ref[r, vs] = o.astype(BF16)
        if h % 2 == 1:
            yield


def _mixer_kernel(x_ref, mod_ref, npre_ref, wbig_ref, wsmall_ref,
                  cw_ref, cb_ref, dtb_ref, alog_ref, dexp_ref, snw_ref, triq_ref, exp_ref,
                  gw_ref, gb_ref, gnw_ref, tric_ref,
                  ys_ref, yg_ref, xpad, sstate, gstate):
    @pl.when(pl.program_id(1) == 0)
    def _():
        xpad[0:HALO, :] = jnp.zeros((HALO, SSD_CONV_DIM), F32)
        sstate[...] = jnp.zeros(sstate.shape, F32)
        gstate[...] = jnp.zeros(gstate.shape, F32)

    x = x_ref[...]
    xn = (x * _rms_scale(x)) * npre_ref[...]
    hb = (xn * (1.0 + mod_ref[1:2, :]) + mod_ref[0:1, :]).astype(BF16)

    env = {"small": _dot(hb, wsmall_ref[...]), "xbc": _project(hb, wbig_ref, _XBC0, SSD_CONV_DIM)}
    ssd = _ssd_stages(env, cw_ref, cb_ref, dtb_ref, alog_ref, dexp_ref, snw_ref, triq_ref, exp_ref,
                      xpad, sstate, ys_ref)
    gla = [_gla_stages(env, slice(ci * GLA_CHUNK, (ci + 1) * GLA_CHUNK), gw_ref, gb_ref, gnw_ref,
                       tric_ref, gstate, yg_ref) for ci in range(SSD_CHUNK // GLA_CHUNK)]
    env["q"] = _project(hb, wbig_ref, _Q0, GLA_K_WIDTH)
    env["k"] = _project(hb, wbig_ref, _K0, GLA_K_WIDTH)
    next(ssd)
    env["v"] = _project(hb, wbig_ref, _V0, GLA_V_WIDTH).astype(BF16)
    next(gla[0])
    next(ssd)
    env["go"] = _project(hb, wbig_ref, _GO0, GLA_V_WIDTH)
    next(gla[0])
    next(ssd)
    env["z"] = _project(hb, wbig_ref, _Z0, SSD_WIDTH)
    next(gla[0])
    next(gla[1])
    next(ssd)
    next(gla[1])
    next(ssd)
    next(gla[1])


def _mixer_call(x, mod, npre, wbig, wsmall, ssd_consts, gla_consts):
    b, s, d = x.shape
    t = SSD_CHUNK
    row = lambda w: pl.BlockSpec((None, t, w), lambda i, j: (i, j, 0))
    consts = (npre, wbig, wsmall) + tuple(ssd_consts) + tuple(gla_consts)
    return pl.pallas_call(
        _mixer_kernel,
        grid=(b, s // t),
        in_specs=[row(d), pl.BlockSpec((None, ADA_CHUNKS, d), lambda i, j: (i, 0, 0))]
        + [_resident(a.shape) for a in consts],
        out_specs=[row(SSD_WIDTH), row(GLA_V_WIDTH)],
        out_shape=[jax.ShapeDtypeStruct((b, s, SSD_WIDTH), BF16),
                   jax.ShapeDtypeStruct((b, s, GLA_V_WIDTH), BF16)],
        scratch_shapes=[
            pltpu.VMEM((HALO + t, SSD_CONV_DIM), F32),
            pltpu.VMEM((SSD_GROUPS, SSD_STATE, GROUP_W), F32),
            pltpu.VMEM((GLA_HEADS, GLA_HEAD_V, GLA_HEAD_K), F32),
        ],
        compiler_params=_params("arbitrary", "arbitrary"),
        name="token_mixer",
    )(x, mod, *consts)


_UP_CHUNK = 384
_UP_DOT = 768
_FFN_TM = 512


def _ffn_kernel(ys_ref, yg_ref, x_ref, mod_ref, npost1_ref, npre_ref, wout_ref, wup_ref,
                cw_ref, cb_ref, npost2_ref, wdown_ref, o_ref, upad, gate):
    tm = x_ref.shape[0]

    @pl.when(pl.program_id(1) == 0)
    def _():
        upad[0:HALO, :] = jnp.zeros((HALO, upad.shape[1]), F32)

    y = _dot(ys_ref[...], wout_ref[0:SSD_WIDTH, :]) + _dot(yg_ref[...], wout_ref[SSD_WIDTH:, :])
    yn = (y * _rms_scale(y)) * npost1_ref[...]
    x1 = x_ref[...] + mod_ref[2:3, :] * yn
    h = ((x1 * _rms_scale(x1)) * npre_ref[...]) * (1.0 + mod_ref[4:5, :]) + mod_ref[3:4, :]
    hb = h.astype(BF16)

    done = set()

    def up(col):
        p = col // _UP_DOT
        if p not in done:
            done.add(p)
            cols = slice(p * _UP_DOT, (p + 1) * _UP_DOT)
            upad[HALO:HALO + tm, cols] = _dot(hb, wup_ref[:, cols])

    def conv(c0):
        cols = slice(c0, c0 + _UP_CHUNK)
        up(c0)
        up(c0 + _UP_CHUNK - 1)
        acc = cb_ref[:, cols]
        for k in range(FFN_CONV):
            off = HALO - (FFN_CONV - 1) + k
            acc = acc + cw_ref[k:k + 1, cols] * upad[off:off + tm, cols]
        upad[0:HALO, cols] = upad[tm:tm + HALO, cols]
        return acc

    for c in range(0, FFN_HIDDEN, _UP_CHUNK):
        gate[:, c:c + _UP_CHUNK] = (_silu(conv(c)) * conv(FFN_HIDDEN + c)).astype(BF16)
    f = _dot(gate[...], wdown_ref[...])
    fn = (f * _rms_scale(f)) * npost2_ref[...]
    o_ref[...] = x1 + mod_ref[5:6, :] * fn


def _ffn_call(ys, yg, x, mod, npost1, npre, wout, wup, cw, cb, npost2, wdown):
    b, s, d = x.shape
    tm = _FFN_TM
    row = lambda w: pl.BlockSpec((None, tm, w), lambda i, j: (i, j, 0))
    consts = (npost1, npre, wout, wup, cw, cb, npost2, wdown)
    return pl.pallas_call(
        _ffn_kernel,
        grid=(b, s // tm),
        in_specs=[row(SSD_WIDTH), row(GLA_V_WIDTH), row(d),
                  pl.BlockSpec((None, ADA_CHUNKS, d), lambda i, j: (i, 0, 0))]
        + [_resident(a.shape) for a in consts],
        out_specs=row(d),
        out_shape=jax.ShapeDtypeStruct((b, s, d), F32),
        scratch_shapes=[pltpu.VMEM((HALO + tm, 2 * FFN_HIDDEN), F32),
                        pltpu.VMEM((tm, FFN_HIDDEN), BF16)],
        compiler_params=_params("arbitrary", "arbitrary"),
        name="channel_mixer",
    )(ys, yg, x, mod, *consts)


def _lane_pad(v, width):
    return jnp.pad(v, ((0, 0), (0, width - v.shape[1])))


def _constants():
    q, c = SSD_CHUNK, GLA_CHUNK
    tri_q = np.tril(np.ones((q, q), np.float32))
    tri_c = np.tril(np.ones((c, c), np.float32))
    expand = np.zeros((SMALL_W, SSD_WIDTH), np.float32)
    for h in range(SSD_HEADS):
        expand[h, h * SSD_HEAD_DIM:(h + 1) * SSD_HEAD_DIM] = 1.0
    return (jnp.asarray(np.tile(tri_q, (1, 3)), BF16),
            jnp.asarray(np.tile(tri_c, (1, 3)), BF16),
            jnp.asarray(np.tile(expand, (3, 1)), BF16))


def kernel(x, c, w_ada, b_ada, norm_mix_pre, norm_mix_post, norm_ffn_pre, norm_ffn_post, w_in, ssd_conv_w, ssd_conv_b, ssd_dt_bias, ssd_a_log, ssd_d, ssd_norm, gla_gate_w, gla_gate_b, gla_norm, w_out, ffn_up, ffn_conv_w, ffn_conv_b, ffn_down):
    bsz, seqlen, d = x.shape
    depth = w_ada.shape[0]
    tri_q3, tri_c3, expand3 = _constants()
    c_pad = jnp.pad(c, ((0, SUBLANES - bsz % SUBLANES), (0, 0))) if bsz % SUBLANES else c

    sizes = [SSD_WIDTH, SSD_CONV_DIM, SSD_HEADS, GLA_K_WIDTH, GLA_K_WIDTH, GLA_V_WIDTH,
             GLA_GATE_RANK, GLA_V_WIDTH]
    o = np.concatenate([[0], np.cumsum(sizes)])

    for i in range(depth):
        mod = _ada_call(c_pad, w_ada[i], b_ada[i][None, :])[:bsz].reshape(bsz, ADA_CHUNKS, d)

        wbig, wsmall = _reorder_call(w_in[i], o)
        ssd_consts = (ssd_conv_w[i], ssd_conv_b[i][None, :],
                      _lane_pad(ssd_dt_bias[i][None, :], SMALL_W), _lane_pad(ssd_a_log[i][None, :], SMALL_W),
                      jnp.repeat(ssd_d[i], SSD_HEAD_DIM)[None, :], ssd_norm[i][None, :], tri_q3, expand3)
        gw = jnp.zeros((SMALL_W, GLA_K_WIDTH), F32).at[SSD_HEADS:SSD_HEADS + GLA_GATE_RANK].set(
            gla_gate_w[i]).astype(BF16)
        gla_consts = (gw, gla_gate_b[i][None, :], gla_norm[i][None, :], tri_c3)
        y_ssd, y_gla = _mixer_call(x, mod, norm_mix_pre[i][None, :], wbig, wsmall, ssd_consts, gla_consts)

        x = _ffn_call(y_ssd, y_gla, x, mod, norm_mix_post[i][None, :], norm_ffn_pre[i][None, :],
                      w_out[i].astype(BF16), ffn_up[i].astype(BF16), ffn_conv_w[i], ffn_conv_b[i][None, :],
                      norm_ffn_post[i][None, :], ffn_down[i].astype(BF16))
    return x
```

```python
import functools

import numpy as np
import jax
import jax.numpy as jnp
from jax import lax
from jax.experimental import pallas as pl
from jax.experimental.pallas import tpu as pltpu

F32 = jnp.float32
BF16 = jnp.bfloat16

D_MODEL = 1024
MIX_WIDTH = 2 * D_MODEL
SSD_WIDTH = MIX_WIDTH // 2
GLA_V_WIDTH = MIX_WIDTH - SSD_WIDTH
SSD_HEAD_DIM = 64
SSD_HEADS = SSD_WIDTH // SSD_HEAD_DIM
SSD_GROUPS = 2
SSD_STATE = 128
SSD_CONV = 4
SSD_CHUNK = 128
SSD_CONV_DIM = SSD_WIDTH + 2 * SSD_GROUPS * SSD_STATE
GLA_HEADS = 4
GLA_K_WIDTH = GLA_V_WIDTH // 2
GLA_HEAD_K = GLA_K_WIDTH // GLA_HEADS
GLA_HEAD_V = GLA_V_WIDTH // GLA_HEADS
GLA_GATE_RANK = 16
GLA_GATE_NORM = 16.0
GLA_CHUNK = 64
FFN_HIDDEN = int(round(8 * D_MODEL / 3 / 128)) * 128
FFN_CONV = 3
ADA_CHUNKS = 6
NORM_EPS = 1e-6

LANES = 128
SUBLANES = 8
VMEM_LIMIT_BYTES = 58 * 1024 * 1024

SMALL_W = LANES
GROUP_W = SSD_WIDTH // SSD_GROUPS
HEADS_PER_GROUP = SSD_HEADS // SSD_GROUPS
HALO = SUBLANES


def _silu(x):
    return x * jax.nn.sigmoid(x)


def _softplus(x):
    return jnp.maximum(x, 0.0) + jnp.log(1.0 + jnp.exp(-jnp.abs(x)))


def _rms_scale(x):
    return lax.rsqrt(jnp.mean(x * x, axis=-1, keepdims=True) + NORM_EPS)


def _split3(x):
    hi = x.astype(BF16)
    r1 = x - hi.astype(F32)
    mid = r1.astype(BF16)
    lo = (r1 - mid.astype(F32)).astype(BF16)
    return hi, mid, lo


def _dot(a, b):
    return jnp.dot(a, b, preferred_element_type=F32)


def _dot_nt(a, b):
    return lax.dot_general(a, b, (((1,), (1,)), ((), ())), preferred_element_type=F32)


def _dot_tn(a, b):
    return lax.dot_general(a, b, (((0,), (0,)), ((), ())), preferred_element_type=F32)


def _resident(shape):
    nd = len(shape)
    return pl.BlockSpec(shape, lambda *_: (0,) * nd, pipeline_mode=pl.Buffered(1))


def _params(*sem):
    return pltpu.CompilerParams(dimension_semantics=sem, vmem_limit_bytes=VMEM_LIMIT_BYTES)


def _ada_kernel(c_ref, w_ref, b_ref, o_ref):
    ca = _silu(c_ref[...]).astype(BF16)
    o_ref[...] = _dot(ca, w_ref[...].astype(BF16)) + b_ref[...]


def _ada_call(c_pad, w, b):
    rows, d = c_pad.shape
    n = w.shape[1]
    tn = 1024
    return pl.pallas_call(
        _ada_kernel,
        grid=(n // tn,),
        in_specs=[
            pl.BlockSpec((rows, d), lambda j: (0, 0)),
            pl.BlockSpec((d, tn), lambda j: (0, j)),
            pl.BlockSpec((1, tn), lambda j: (0, j)),
        ],
        out_specs=pl.BlockSpec((rows, tn), lambda j: (0, j)),
        out_shape=jax.ShapeDtypeStruct((rows, n), F32),
        compiler_params=_params("arbitrary"),
        name="ada_mod",
    )(c_pad, w, b)


def _reorder_kernel(wt_ref, big_ref, small_ref, *, offs):
    dst = pl.program_id(0) * _PROJ_CHUNK
    skip_dt = offs[3] - offs[2]
    skip_glr = offs[7] - offs[6]
    src = dst + jnp.where(dst >= offs[2], skip_dt, 0) + jnp.where(dst >= offs[6] - skip_dt, skip_glr, 0)
    rows = wt_ref[pl.ds(pl.multiple_of(src, 2 * SUBLANES), _PROJ_CHUNK), :]
    big_ref[...] = rows.T.astype(BF16)

    @pl.when(pl.program_id(0) == 0)
    def _():
        d = wt_ref.shape[1]
        pad = jnp.zeros((SMALL_W - skip_dt - skip_glr, d), F32)
        small = jnp.concatenate([wt_ref[offs[2]:offs[3], :], wt_ref[offs[6]:offs[7], :], pad], axis=0)
        small_ref[...] = small.T.astype(BF16)


def _reorder_call(wt, offs):
    n_in, d = wt.shape
    return pl.pallas_call(
        functools.partial(_reorder_kernel, offs=tuple(int(v) for v in offs)),
        grid=(_PROJ_W // _PROJ_CHUNK,),
        in_specs=[_resident((n_in, d))],
        out_specs=[pl.BlockSpec((d, _PROJ_CHUNK), lambda j: (0, j)), pl.BlockSpec((d, SMALL_W), lambda j: (0, 0))],
        out_shape=[jax.ShapeDtypeStruct((d, _PROJ_W), BF16), jax.ShapeDtypeStruct((d, SMALL_W), BF16)],
        compiler_params=_params("arbitrary"),
        name="reorder_w_in",
    )(wt)


_Z0 = 0
_XBC0 = _Z0 + SSD_WIDTH
_Q0 = _XBC0 + SSD_CONV_DIM
_K0 = _Q0 + GLA_K_WIDTH
_V0 = _K0 + GLA_K_WIDTH
_GO0 = _V0 + GLA_V_WIDTH
_PROJ_W = _GO0 + GLA_V_WIDTH
_PROJ_CHUNK = 512


def _project(hb, w_ref, c0, width):
    parts = [_dot(hb, w_ref[:, c:c + min(_PROJ_CHUNK, c0 + width - c)])
             for c in range(c0, c0 + width, _PROJ_CHUNK)]
    return parts[0] if len(parts) == 1 else jnp.concatenate(parts, axis=1)


def _ssd_stages(env, cw_ref, cb_ref, dtb_ref, alog_ref, dexp_ref, nw_ref, tri_ref, exp_ref,
                xpad, state, y_ref):
    q = SSD_CHUNK
    n = SSD_STATE

    xpad[HALO:HALO + q, :] = env["xbc"]
    acc = cb_ref[...]
    for k in range(SSD_CONV):
        off = HALO - (SSD_CONV - 1) + k
        acc = acc + cw_ref[k:k + 1, :] * xpad[off:off + q, :]
    xpad[0:HALO, :] = xpad[q:q + HALO, :]
    xc = _silu(acc)
    xs = xc[:, :SSD_WIDTH]
    bm = xc[:, SSD_WIDTH:SSD_WIDTH + SSD_GROUPS * n].astype(BF16)
    cm = xc[:, SSD_WIDTH + SSD_GROUPS * n:].astype(BF16)
    yield

    lane = lax.broadcasted_iota(jnp.int32, (q, SMALL_W), 1)
    head_lane = lane < SSD_HEADS
    dt = _softplus(env["small"] + dtb_ref[...])
    a = -jnp.exp(alog_ref[...])
    dt = jnp.where(head_lane, dt, 0.0)
    da = jnp.where(head_lane, dt * a, 0.0)
    cs = _dot(tri_ref[...], jnp.concatenate(_split3(da), axis=0))
    cs_t = cs.T
    cs_e = _dot(jnp.concatenate(_split3(cs), axis=1), exp_ref[...])
    dt_e = _dot(jnp.concatenate(_split3(dt), axis=1), exp_ref[...])
    cs_last = cs_e[q - 1:q, :]

    xdt = xs * dt_e
    xdt_b = xdt.astype(BF16)
    xdec_b = (xdt * jnp.exp(cs_last - cs_e)).astype(BF16)
    decay_in = jnp.exp(cs_e)
    decay_chunk = jnp.exp(cs_last)
    yield

    row = lax.broadcasted_iota(jnp.int32, (q, q), 0)
    col = lax.broadcasted_iota(jnp.int32, (q, q), 1)
    causal = row >= col
    first_half = col < SSD_HEAD_DIM

    y_parts = []
    for g in range(SSD_GROUPS):
        cmg = cm[:, g * n:(g + 1) * n]
        bmg = bm[:, g * n:(g + 1) * n]
        gs = slice(g * GROUP_W, (g + 1) * GROUP_W)
        scores = _dot_nt(cmg, bmg)
        y_off = _dot(cmg, state[g].astype(BF16)) * decay_in[:, gs]
        state[g] = decay_chunk[:, gs] * state[g] + _dot_tn(bmg, xdec_b[:, gs])
        diag = []
        for pair in range(HEADS_PER_GROUP // 2):
            h0 = g * HEADS_PER_GROUP + 2 * pair
            ms = []
            for h in (h0, h0 + 1):
                seg = cs[:, h:h + 1] - cs_t[h:h + 1, :]
                decay = jnp.exp(jnp.where(causal, seg, -jnp.inf))
                ms.append((scores * decay).astype(BF16))
            slab = xdt_b[:, h0 * SSD_HEAD_DIM:(h0 + 2) * SSD_HEAD_DIM]
            zero = jnp.zeros_like(slab)
            rhs = jnp.concatenate([jnp.where(first_half, slab, zero),
                                   jnp.where(first_half, zero, slab)], axis=0)
            diag.append(_dot(jnp.concatenate(ms, axis=1), rhs))
        y_parts.append(jnp.concatenate(diag, axis=1) + y_off)
        yield
    y = jnp.concatenate(y_parts, axis=1) + xs * dexp_ref[...]
    y = y * _silu(env["z"])
    outs = []
    for g in range(SSD_GROUPS):
        yg = y[:, g * GROUP_W:(g + 1) * GROUP_W]
        outs.append(yg * _rms_scale(yg))
    y_ref[...] = (jnp.concatenate(outs, axis=1) * nw_ref[...]).astype(BF16)
    yield


def _gla_stages(env, r, gw_ref, gb_ref, nw_ref, tri_ref, state, y_ref):
    c = GLA_CHUNK
    dk, dv = GLA_HEAD_K, GLA_HEAD_V
    row = lax.broadcasted_iota(jnp.int32, (c, c), 0)
    col = lax.broadcasted_iota(jnp.int32, (c, c), 1)
    causal = row >= col

    qf = env["q"][r, :] * (dk ** -0.5)
    kf = env["k"][r, :]
    logit = _dot(env["small"][r, :].astype(BF16), gw_ref[...]) + gb_ref[...]
    lg = -_softplus(-logit) / GLA_GATE_NORM
    bcum = _dot(tri_ref[...], jnp.concatenate(_split3(lg), axis=0))
    blast = bcum[c - 1:c, :]
    qt = (qf * jnp.exp(bcum)).astype(BF16)
    kt = (kf * jnp.exp(-bcum)).astype(BF16)
    kd = (kf * jnp.exp(blast - bcum)).astype(BF16)
    eblast = jnp.exp(blast)
    yield
    for h in range(GLA_HEADS):
        ks = slice(h * dk, (h + 1) * dk)
        vs = slice(h * dv, (h + 1) * dv)
        attn = jnp.where(causal, _dot_nt(qt[:, ks], kt[:, ks]), 0.0)
        vh = env["v"][r, vs]
        o = _dot(attn.astype(BF16), vh) + _dot_nt(qt[:, ks], state[h].astype(BF16))
        state[h] = eblast[:, ks] * state[h] + _dot_tn(vh, kd[:, ks])
        o = o * _rms_scale(o) * nw_ref[...]
        o = o * _silu(env["go"][r, vs])
        y_ref[r, vs] = o.astype(BF16)
        if h % 2 == 1:
            yield


def _mixer_kernel(x_ref, mod_ref, npre_ref, wbig_ref, wsmall_ref,
                  cw_ref, cb_ref, dtb_ref, alog_ref, dexp_ref, snw_ref, triq_ref, exp_ref,
                  gw_ref, gb_ref, gnw_ref, tric_ref,
                  ys_ref, yg_ref, xpad, sstate, gstate):
    @pl.when(pl.program_id(1) == 0)
    def _():
        xpad[0:HALO, :] = jnp.zeros((HALO, SSD_CONV_DIM), F32)
        sstate[...] = jnp.zeros(sstate.shape, F32)
        gstate[...] = jnp.zeros(gstate.shape, F32)

    x = x_ref[...]
    xn = (x * _rms_scale(x)) * npre_ref[...]
    hb = (xn * (1.0 + mod_ref[1:2, :]) + mod_ref[0:1, :]).astype(BF16)

    env = {"small": _dot(hb, wsmall_ref[...]), "xbc": _project(hb, wbig_ref, _XBC0, SSD_CONV_DIM)}
    ssd = _ssd_stages(env, cw_ref, cb_ref, dtb_ref, alog_ref, dexp_ref, snw_ref, triq_ref, exp_ref,
                      xpad, sstate, ys_ref)
    gla = [_gla_stages(env, slice(ci * GLA_CHUNK, (ci + 1) * GLA_CHUNK), gw_ref, gb_ref, gnw_ref,
                       tric_ref, gstate, yg_ref) for ci in range(SSD_CHUNK // GLA_CHUNK)]
    env["q"] = _project(hb, wbig_ref, _Q0, GLA_K_WIDTH)
    env["k"] = _project(hb, wbig_ref, _K0, GLA_K_WIDTH)
    next(ssd)
    env["v"] = _project(hb, wbig_ref, _V0, GLA_V_WIDTH).astype(BF16)
    next(gla[0])
    next(ssd)
    env["go"] = _project(hb, wbig_ref, _GO0, GLA_V_WIDTH)
    next(gla[0])
    next(ssd)
    env["z"] = _project(hb, wbig_ref, _Z0, SSD_WIDTH)
    next(gla[0])
    next(gla[1])
    next(ssd)
    next(gla[1])
    next(ssd)
    next(gla[1])


def _mixer_call(x, mod, npre, wbig, wsmall, ssd_consts, gla_consts):
    b, s, d = x.shape
    t = SSD_CHUNK
    row = lambda w: pl.BlockSpec((None, t, w), lambda i, j: (i, j, 0))
    consts = (npre, wbig, wsmall) + tuple(ssd_consts) + tuple(gla_consts)
    return pl.pallas_call(
        _mixer_kernel,
        grid=(b, s // t),
        in_specs=[row(d), pl.BlockSpec((None, ADA_CHUNKS, d), lambda i, j: (i, 0, 0))]
        + [_resident(a.shape) for a in consts],
        out_specs=[row(SSD_WIDTH), row(GLA_V_WIDTH)],
        out_shape=[jax.ShapeDtypeStruct((b, s, SSD_WIDTH), BF16),
                   jax.ShapeDtypeStruct((b, s, GLA_V_WIDTH), BF16)],
        scratch_shapes=[
            pltpu.VMEM((HALO + t, SSD_CONV_DIM), F32),
            pltpu.VMEM((SSD_GROUPS, SSD_STATE, GROUP_W), F32),
            pltpu.VMEM((GLA_HEADS, GLA_HEAD_V, GLA_HEAD_K), F32),
        ],
        compiler_params=_params("arbitrary", "arbitrary"),
        name="token_mixer",
    )(x, mod, *consts)


_UP_CHUNK = 384
_UP_DOT = 768
_FFN_TM = 512


def _ffn_kernel(ys_ref, yg_ref, x_ref, mod_ref, npost1_ref, npre_ref, wout_ref, wup_ref,
                cw_ref, cb_ref, npost2_ref, wdown_ref, o_ref, upad, gate):
    tm = x_ref.shape[0]

    @pl.when(pl.program_id(1) == 0)
    def _():
        upad[0:HALO, :] = jnp.zeros((HALO, upad.shape[1]), F32)

    y = _dot(ys_ref[...], wout_ref[0:SSD_WIDTH, :]) + _dot(yg_ref[...], wout_ref[SSD_WIDTH:, :])
    yn = (y * _rms_scale(y)) * npost1_ref[...]
    x1 = x_ref[...] + mod_ref[2:3, :] * yn
    h = ((x1 * _rms_scale(x1)) * npre_ref[...]) * (1.0 + mod_ref[4:5, :]) + mod_ref[3:4, :]
    hb = h.astype(BF16)

    done = set()

    def up(col):
        p = col // _UP_DOT
        if p not in done:
            done.add(p)
            cols = slice(p * _UP_DOT, (p + 1) * _UP_DOT)
            upad[HALO:HALO + tm, cols] = _dot(hb, wup_ref[:, cols])

    def conv(c0):
        cols = slice(c0, c0 + _UP_CHUNK)
        up(c0)
        up(c0 + _UP_CHUNK - 1)
        acc = cb_ref[:, cols]
        for k in range(FFN_CONV):
            off = HALO - (FFN_CONV - 1) + k
            acc = acc + cw_ref[k:k + 1, cols] * upad[off:off + tm, cols]
        upad[0:HALO, cols] = upad[tm:tm + HALO, cols]
        return acc

    for c in range(0, FFN_HIDDEN, _UP_CHUNK):
        gate[:, c:c + _UP_CHUNK] = (_silu(conv(c)) * conv(FFN_HIDDEN + c)).astype(BF16)
    f = _dot(gate[...], wdown_ref[...])
    fn = (f * _rms_scale(f)) * npost2_ref[...]
    o_ref[...] = x1 + mod_ref[5:6, :] * fn


def _ffn_call(ys, yg, x, mod, npost1, npre, wout, wup, cw, cb, npost2, wdown):
    b, s, d = x.shape
    tm = _FFN_TM
    row = lambda w: pl.BlockSpec((None, tm, w), lambda i, j: (i, j, 0))
    consts = (npost1, npre, wout, wup, cw, cb, npost2, wdown)
    return pl.pallas_call(
        _ffn_kernel,
        grid=(b, s // tm),
        in_specs=[row(SSD_WIDTH), row(GLA_V_WIDTH), row(d),
                  pl.BlockSpec((None, ADA_CHUNKS, d), lambda i, j: (i, 0, 0))]
        + [_resident(a.shape) for a in consts],
        out_specs=row(d),
        out_shape=jax.ShapeDtypeStruct((b, s, d), F32),
        scratch_shapes=[pltpu.VMEM((HALO + tm, 2 * FFN_HIDDEN), F32),
                        pltpu.VMEM((tm, FFN_HIDDEN), BF16)],
        compiler_params=_params("arbitrary", "arbitrary"),
        name="channel_mixer",
    )(ys, yg, x, mod, *consts)


def _lane_pad(v, width):
    return jnp.pad(v, ((0, 0), (0, width - v.shape[1])))


def _constants():
    q, c = SSD_CHUNK, GLA_CHUNK
    tri_q = np.tril(np.ones((q, q), np.float32))
    tri_c = np.tril(np.ones((c, c), np.float32))
    expand = np.zeros((SMALL_W, SSD_WIDTH), np.float32)
    for h in range(SSD_HEADS):
        expand[h, h * SSD_HEAD_DIM:(h + 1) * SSD_HEAD_DIM] = 1.0
    return (jnp.asarray(np.tile(tri_q, (1, 3)), BF16),
            jnp.asarray(np.tile(tri_c, (1, 3)), BF16),
            jnp.asarray(np.tile(expand, (3, 1)), BF16))


def kernel(x, c, w_ada, b_ada, norm_mix_pre, norm_mix_post, norm_ffn_pre, norm_ffn_post, w_in, ssd_conv_w, ssd_conv_b, ssd_dt_bias, ssd_a_log, ssd_d, ssd_norm, gla_gate_w, gla_gate_b, gla_norm, w_out, ffn_up, ffn_conv_w, ffn_conv_b, ffn_down):
    bsz, seqlen, d = x.shape
    depth = w_ada.shape[0]
    tri_q3, tri_c3, expand3 = _constants()
    c_pad = jnp.pad(c, ((0, SUBLANES - bsz % SUBLANES), (0, 0))) if bsz % SUBLANES else c

    sizes = [SSD_WIDTH, SSD_CONV_DIM, SSD_HEADS, GLA_K_WIDTH, GLA_K_WIDTH, GLA_V_WIDTH,
             GLA_GATE_RANK, GLA_V_WIDTH]
    o = np.concatenate([[0], np.cumsum(sizes)])

    for i in range(depth):
        mod = _ada_call(c_pad, w_ada[i], b_ada[i][None, :])[:bsz].reshape(bsz, ADA_CHUNKS, d)

        wbig, wsmall = _reorder_call(jnp.swapaxes(w_in[i], 0, 1), o)
        ssd_consts = (ssd_conv_w[i], ssd_conv_b[i][None, :],
                      _lane_pad(ssd_dt_bias[i][None, :], SMALL_W), _lane_pad(ssd_a_log[i][None, :], SMALL_W),
                      jnp.repeat(ssd_d[i], SSD_HEAD_DIM)[None, :], ssd_norm[i][None, :], tri_q3, expand3)
        gw = jnp.zeros((SMALL_W, GLA_K_WIDTH), F32).at[SSD_HEADS:SSD_HEADS + GLA_GATE_RANK].set(
            gla_gate_w[i]).astype(BF16)
        gla_consts = (gw, gla_gate_b[i][None, :], gla_norm[i][None, :], tri_c3)
        y_ssd, y_gla = _mixer_call(x, mod, norm_mix_pre[i][None, :], wbig, wsmall, ssd_consts, gla_consts)

        x = _ffn_call(y_ssd, y_gla, x, mod, norm_mix_post[i][None, :], norm_ffn_pre[i][None, :],
                      w_out[i].astype(BF16), ffn_up[i].astype(BF16), ffn_conv_w[i], ffn_conv_b[i][None, :],
                      norm_ffn_post[i][None, :], ffn_down[i].astype(BF16))
    return x
```

```python
import functools

import numpy as np
import jax
import jax.numpy as jnp
from jax import lax
from jax.experimental import pallas as pl
from jax.experimental.pallas import tpu as pltpu

F32 = jnp.float32
BF16 = jnp.bfloat16

D_MODEL = 1024
MIX_WIDTH = 2 * D_MODEL
SSD_WIDTH = MIX_WIDTH // 2
GLA_V_WIDTH = MIX_WIDTH - SSD_WIDTH
SSD_HEAD_DIM = 64
SSD_HEADS = SSD_WIDTH // SSD_HEAD_DIM
SSD_GROUPS = 2
SSD_STATE = 128
SSD_CONV = 4
SSD_CHUNK = 128
SSD_CONV_DIM = SSD_WIDTH + 2 * SSD_GROUPS * SSD_STATE
GLA_HEADS = 4
GLA_K_WIDTH = GLA_V_WIDTH // 2
GLA_HEAD_K = GLA_K_WIDTH // GLA_HEADS
GLA_HEAD_V = GLA_V_WIDTH // GLA_HEADS
GLA_GATE_RANK = 16
GLA_GATE_NORM = 16.0
GLA_CHUNK = 64
FFN_HIDDEN = int(round(8 * D_MODEL / 3 / 128)) * 128
FFN_CONV = 3
ADA_CHUNKS = 6
NORM_EPS = 1e-6

LANES = 128
SUBLANES = 8
VMEM_LIMIT_BYTES = 58 * 1024 * 1024

SMALL_W = LANES
GROUP_W = SSD_WIDTH // SSD_GROUPS
HEADS_PER_GROUP = SSD_HEADS // SSD_GROUPS
HALO = SUBLANES


def _silu(x):
    return x * jax.nn.sigmoid(x)


def _softplus(x):
    return jnp.maximum(x, 0.0) + jnp.log(1.0 + jnp.exp(-jnp.abs(x)))


def _rms_scale(x):
    return lax.rsqrt(jnp.mean(x * x, axis=-1, keepdims=True) + NORM_EPS)


def _split3(x):
    hi = x.astype(BF16)
    r1 = x - hi.astype(F32)
    mid = r1.astype(BF16)
    lo = (r1 - mid.astype(F32)).astype(BF16)
    return hi, mid, lo


def _dot(a, b):
    return jnp.dot(a, b, preferred_element_type=F32)


def _dot_nt(a, b):
    return lax.dot_general(a, b, (((1,), (1,)), ((), ())), preferred_element_type=F32)


def _dot_tn(a, b):
    return lax.dot_general(a, b, (((0,), (0,)), ((), ())), preferred_element_type=F32)


def _resident(shape):
    nd = len(shape)
    return pl.BlockSpec(shape, lambda *_: (0,) * nd, pipeline_mode=pl.Buffered(1))


def _params(*sem):
    return pltpu.CompilerParams(dimension_semantics=sem, vmem_limit_bytes=VMEM_LIMIT_BYTES)


def _ada_kernel(c_ref, w_ref, b_ref, o_ref):
    ca = _silu(c_ref[...]).astype(BF16)
    o_ref[...] = _dot(ca, w_ref[...].astype(BF16)) + b_ref[...]


def _ada_call(c_pad, w, b):
    rows, d = c_pad.shape
    n = w.shape[1]
    tn = 1024
    return pl.pallas_call(
        _ada_kernel,
        grid=(n // tn,),
        in_specs=[
            pl.BlockSpec((rows, d), lambda j: (0, 0)),
            pl.BlockSpec((d, tn), lambda j: (0, j)),
            pl.BlockSpec((1, tn), lambda j: (0, j)),
        ],
        out_specs=pl.BlockSpec((rows, tn), lambda j: (0, j)),
        out_shape=jax.ShapeDtypeStruct((rows, n), F32),
        compiler_params=_params("arbitrary"),
        name="ada_mod",
    )(c_pad, w, b)


def _reorder_kernel(wt_ref, big_ref, small_ref, *, offs):
    dst = pl.program_id(0) * _PROJ_CHUNK
    skip_dt = offs[3] - offs[2]
    skip_glr = offs[7] - offs[6]
    src = dst + jnp.where(dst >= offs[2], skip_dt, 0) + jnp.where(dst >= offs[6] - skip_dt, skip_glr, 0)
    rows = wt_ref[pl.ds(pl.multiple_of(src, 2 * SUBLANES), _PROJ_CHUNK), :]
    big_ref[...] = rows.T.astype(BF16)

    @pl.when(pl.program_id(0) == 0)
    def _():
        d = wt_ref.shape[1]
        pad = jnp.zeros((SMALL_W - skip_dt - skip_glr, d), F32)
        small = jnp.concatenate([wt_ref[offs[2]:offs[3], :], wt_ref[offs[6]:offs[7], :], pad], axis=0)
        small_ref[...] = small.T.astype(BF16)


def _reorder_call(wt, offs):
    n_in, d = wt.shape
    return pl.pallas_call(
        functools.partial(_reorder_kernel, offs=tuple(int(v) for v in offs)),
        grid=(_PROJ_W // _PROJ_CHUNK,),
        in_specs=[_resident((n_in, d))],
        out_specs=[pl.BlockSpec((d, _PROJ_CHUNK), lambda j: (0, j)), pl.BlockSpec((d, SMALL_W), lambda j: (0, 0))],
        out_shape=[jax.ShapeDtypeStruct((d, _PROJ_W), BF16), jax.ShapeDtypeStruct((d, SMALL_W), BF16)],
        compiler_params=_params("arbitrary"),
        name="reorder_w_in",
    )(wt)


_Z0 = 0
_XBC0 = _Z0 + SSD_WIDTH
_Q0 = _XBC0 + SSD_CONV_DIM
_K0 = _Q0 + GLA_K_WIDTH
_V0 = _K0 + GLA_K_WIDTH
_GO0 = _V0 + GLA_V_WIDTH
_PROJ_W = _GO0 + GLA_V_WIDTH
_PROJ_CHUNK = 512
_MIX_TILE = 2 * SSD_CHUNK


def _project(hb, w_ref, c0, width):
    parts = [_dot(hb, w_ref[:, c:c + min(_PROJ_CHUNK, c0 + width - c)])
             for c in range(c0, c0 + width, _PROJ_CHUNK)]
    return parts[0] if len(parts) == 1 else jnp.concatenate(parts, axis=1)


def _ssd_stages(env, rows, cw_ref, cb_ref, dtb_ref, alog_ref, dexp_ref, nw_ref, tri_ref, exp_ref,
                xpad, state, y_ref):
    q = SSD_CHUNK
    n = SSD_STATE

    xpad[HALO:HALO + q, :] = env["xbc"]
    acc = cb_ref[...]
    for k in range(SSD_CONV):
        off = HALO - (SSD_CONV - 1) + k
        acc = acc + cw_ref[k:k + 1, :] * xpad[off:off + q, :]
    xpad[0:HALO, :] = xpad[q:q + HALO, :]
    xc = _silu(acc)
    xs = xc[:, :SSD_WIDTH]
    bm = xc[:, SSD_WIDTH:SSD_WIDTH + SSD_GROUPS * n].astype(BF16)
    cm = xc[:, SSD_WIDTH + SSD_GROUPS * n:].astype(BF16)
    yield

    lane = lax.broadcasted_iota(jnp.int32, (q, SMALL_W), 1)
    head_lane = lane < SSD_HEADS
    dt = _softplus(env["small"] + dtb_ref[...])
    a = -jnp.exp(alog_ref[...])
    dt = jnp.where(head_lane, dt, 0.0)
    da = jnp.where(head_lane, dt * a, 0.0)
    cs = _dot(tri_ref[...], jnp.concatenate(_split3(da), axis=0))
    cs_t = cs.T
    cs_e = _dot(jnp.concatenate(_split3(cs), axis=1), exp_ref[...])
    dt_e = _dot(jnp.concatenate(_split3(dt), axis=1), exp_ref[...])
    cs_last = cs_e[q - 1:q, :]

    xdt = xs * dt_e
    xdt_b = xdt.astype(BF16)
    xdec_b = (xdt * jnp.exp(cs_last - cs_e)).astype(BF16)
    decay_in = jnp.exp(cs_e)
    decay_chunk = jnp.exp(cs_last)
    yield

    row = lax.broadcasted_iota(jnp.int32, (q, q), 0)
    col = lax.broadcasted_iota(jnp.int32, (q, q), 1)
    causal = row >= col
    first_half = col < SSD_HEAD_DIM

    y_parts = []
    for g in range(SSD_GROUPS):
        cmg = cm[:, g * n:(g + 1) * n]
        bmg = bm[:, g * n:(g + 1) * n]
        gs = slice(g * GROUP_W, (g + 1) * GROUP_W)
        scores = _dot_nt(cmg, bmg)
        y_off = _dot(cmg, state[g].astype(BF16)) * decay_in[:, gs]
        state[g] = decay_chunk[:, gs] * state[g] + _dot_tn(bmg, xdec_b[:, gs])
        diag = []
        for pair in range(HEADS_PER_GROUP // 2):
            h0 = g * HEADS_PER_GROUP + 2 * pair
            ms = []
            for h in (h0, h0 + 1):
                seg = cs[:, h:h + 1] - cs_t[h:h + 1, :]
                decay = jnp.exp(jnp.where(causal, seg, -jnp.inf))
                ms.append((scores * decay).astype(BF16))
            slab = xdt_b[:, h0 * SSD_HEAD_DIM:(h0 + 2) * SSD_HEAD_DIM]
            zero = jnp.zeros_like(slab)
            rhs = jnp.concatenate([jnp.where(first_half, slab, zero),
                                   jnp.where(first_half, zero, slab)], axis=0)
            diag.append(_dot(jnp.concatenate(ms, axis=1), rhs))
        y_parts.append(jnp.concatenate(diag, axis=1) + y_off)
        yield
    y = jnp.concatenate(y_parts, axis=1) + xs * dexp_ref[...]
    y = y * _silu(env["z"])
    outs = []
    for g in range(SSD_GROUPS):
        yg = y[:, g * GROUP_W:(g + 1) * GROUP_W]
        outs.append(yg * _rms_scale(yg))
    y_ref[rows, :] = (jnp.concatenate(outs, axis=1) * nw_ref[...]).astype(BF16)
    yield


def _gla_stages(env, r, rows, gw_ref, gb_ref, nw_ref, tri_ref, state, y_ref):
    c = GLA_CHUNK
    dk, dv = GLA_HEAD_K, GLA_HEAD_V
    row = lax.broadcasted_iota(jnp.int32, (c, c), 0)
    col = lax.broadcasted_iota(jnp.int32, (c, c), 1)
    causal = row >= col

    qf = env["q"][r, :] * (dk ** -0.5)
    kf = env["k"][r, :]
    logit = _dot(env["small"][r, :].astype(BF16), gw_ref[...]) + gb_ref[...]
    lg = -_softplus(-logit) / GLA_GATE_NORM
    bcum = _dot(tri_ref[...], jnp.concatenate(_split3(lg), axis=0))
    blast = bcum[c - 1:c, :]
    qt = (qf * jnp.exp(bcum)).astype(BF16)
    kt = (kf * jnp.exp(-bcum)).astype(BF16)
    kd = (kf * jnp.exp(blast - bcum)).astype(BF16)
    eblast = jnp.exp(blast)
    yield
    for h in range(GLA_HEADS):
        ks = slice(h * dk, (h + 1) * dk)
        vs = slice(h * dv, (h + 1) * dv)
        attn = jnp.where(causal, _dot_nt(qt[:, ks], kt[:, ks]), 0.0)
        vh = env["v"][r, vs]
        o = _dot(attn.astype(BF16), vh) + _dot_nt(qt[:, ks], state[h].astype(BF16))
        state[h] = eblast[:, ks] * state[h] + _dot_tn(vh, kd[:, ks])
        o = o * _rms_scale(o) * nw_ref[...]
        o = o * _silu(env["go"][r, vs])
        y_ref[rows, vs] = o.astype(BF16)
        if h % 2 == 1:
            yield


def _mixer_kernel(x_ref, mod_ref, npre_ref, wbig_ref, wsmall_ref,
                  cw_ref, cb_ref, dtb_ref, alog_ref, dexp_ref, snw_ref, triq_ref, exp_ref,
                  gw_ref, gb_ref, gnw_ref, tric_ref,
                  ys_ref, yg_ref, xpad, sstate, gstate):
    @pl.when(pl.program_id(1) == 0)
    def _():
        xpad[0:HALO, :] = jnp.zeros((HALO, SSD_CONV_DIM), F32)
        sstate[...] = jnp.zeros(sstate.shape, F32)
        gstate[...] = jnp.zeros(gstate.shape, F32)

    for r0 in range(0, x_ref.shape[0], SSD_CHUNK):
        rows = slice(r0, r0 + SSD_CHUNK)
        x = x_ref[rows, :]
        xn = (x * _rms_scale(x)) * npre_ref[...]
        hb = (xn * (1.0 + mod_ref[1:2, :]) + mod_ref[0:1, :]).astype(BF16)

        env = {"small": _dot(hb, wsmall_ref[...]), "xbc": _project(hb, wbig_ref, _XBC0, SSD_CONV_DIM)}
        ssd = _ssd_stages(env, rows, cw_ref, cb_ref, dtb_ref, alog_ref, dexp_ref, snw_ref, triq_ref, exp_ref,
                          xpad, sstate, ys_ref)
        gla = [_gla_stages(env, slice(ci * GLA_CHUNK, (ci + 1) * GLA_CHUNK),
                           slice(r0 + ci * GLA_CHUNK, r0 + (ci + 1) * GLA_CHUNK), gw_ref, gb_ref, gnw_ref,
                           tric_ref, gstate, yg_ref) for ci in range(SSD_CHUNK // GLA_CHUNK)]
        env["q"] = _project(hb, wbig_ref, _Q0, GLA_K_WIDTH)
        env["k"] = _project(hb, wbig_ref, _K0, GLA_K_WIDTH)
        next(ssd)
        env["v"] = _project(hb, wbig_ref, _V0, GLA_V_WIDTH).astype(BF16)
        next(gla[0])
        next(ssd)
        env["go"] = _project(hb, wbig_ref, _GO0, GLA_V_WIDTH)
        next(gla[0])
        next(ssd)
        env["z"] = _project(hb, wbig_ref, _Z0, SSD_WIDTH)
        next(gla[0])
        next(gla[1])
        next(ssd)
        next(gla[1])
        next(ssd)
        next(gla[1])


def _mixer_call(x, mod, npre, wbig, wsmall, ssd_consts, gla_consts):
    b, s, d = x.shape
    t = _MIX_TILE
    row = lambda w: pl.BlockSpec((None, t, w), lambda i, j: (i, j, 0))
    consts = (npre, wbig, wsmall) + tuple(ssd_consts) + tuple(gla_consts)
    return pl.pallas_call(
        _mixer_kernel,
        grid=(b, s // t),
        in_specs=[row(d), pl.BlockSpec((None, ADA_CHUNKS, d), lambda i, j: (i, 0, 0))]
        + [_resident(a.shape) for a in consts],
        out_specs=[row(SSD_WIDTH), row(GLA_V_WIDTH)],
        out_shape=[jax.ShapeDtypeStruct((b, s, SSD_WIDTH), BF16),
                   jax.ShapeDtypeStruct((b, s, GLA_V_WIDTH), BF16)],
        scratch_shapes=[
            pltpu.VMEM((HALO + SSD_CHUNK, SSD_CONV_DIM), F32),
            pltpu.VMEM((SSD_GROUPS, SSD_STATE, GROUP_W), F32),
            pltpu.VMEM((GLA_HEADS, GLA_HEAD_V, GLA_HEAD_K), F32),
        ],
        compiler_params=_params("arbitrary", "arbitrary"),
        name="token_mixer",
    )(x, mod, *consts)


_UP_CHUNK = 384
_UP_DOT = 768
_FFN_TM = 512


def _ffn_kernel(ys_ref, yg_ref, x_ref, mod_ref, npost1_ref, npre_ref, wout_ref, wup_ref,
                cw_ref, cb_ref, npost2_ref, wdown_ref, o_ref, upad, gate):
    tm = x_ref.shape[0]

    @pl.when(pl.program_id(1) == 0)
    def _():
        upad[0:HALO, :] = jnp.zeros((HALO, upad.shape[1]), F32)

    y = _dot(ys_ref[...], wout_ref[0:SSD_WIDTH, :]) + _dot(yg_ref[...], wout_ref[SSD_WIDTH:, :])
    yn = (y * _rms_scale(y)) * npost1_ref[...]
    x1 = x_ref[...] + mod_ref[2:3, :] * yn
    h = ((x1 * _rms_scale(x1)) * npre_ref[...]) * (1.0 + mod_ref[4:5, :]) + mod_ref[3:4, :]
    hb = h.astype(BF16)

    done = set()

    def up(col):
        p = col // _UP_DOT
        if p not in done:
            done.add(p)
            cols = slice(p * _UP_DOT, (p + 1) * _UP_DOT)
            upad[HALO:HALO + tm, cols] = _dot(hb, wup_ref[:, cols])

    def conv(c0):
        cols = slice(c0, c0 + _UP_CHUNK)
        up(c0)
        up(c0 + _UP_CHUNK - 1)
        acc = cb_ref[:, cols]
        for k in range(FFN_CONV):
            off = HALO - (FFN_CONV - 1) + k
            acc = acc + cw_ref[k:k + 1, cols] * upad[off:off + tm, cols]
        upad[0:HALO, cols] = upad[tm:tm + HALO, cols]
        return acc

    for c in range(0, FFN_HIDDEN, _UP_CHUNK):
        gate[:, c:c + _UP_CHUNK] = (_silu(conv(c)) * conv(FFN_HIDDEN + c)).astype(BF16)
    f = _dot(gate[...], wdown_ref[...])
    fn = (f * _rms_scale(f)) * npost2_ref[...]
    o_ref[...] = x1 + mod_ref[5:6, :] * fn


def _ffn_call(ys, yg, x, mod, npost1, npre, wout, wup, cw, cb, npost2, wdown):
    b, s, d = x.shape
    tm = _FFN_TM
    row = lambda w: pl.BlockSpec((None, tm, w), lambda i, j: (i, j, 0))
    consts = (npost1, npre, wout, wup, cw, cb, npost2, wdown)
    return pl.pallas_call(
        _ffn_kernel,
        grid=(b, s // tm),
        in_specs=[row(SSD_WIDTH), row(GLA_V_WIDTH), row(d),
                  pl.BlockSpec((None, ADA_CHUNKS, d), lambda i, j: (i, 0, 0))]
        + [_resident(a.shape) for a in consts],
        out_specs=row(d),
        out_shape=jax.ShapeDtypeStruct((b, s, d), F32),
        scratch_shapes=[pltpu.VMEM((HALO + tm, 2 * FFN_HIDDEN), F32),
                        pltpu.VMEM((tm, FFN_HIDDEN), BF16)],
        compiler_params=_params("arbitrary", "arbitrary"),
        name="channel_mixer",
    )(ys, yg, x, mod, *consts)


def _lane_pad(v, width):
    return jnp.pad(v, ((0, 0), (0, width - v.shape[1])))


def _constants():
    q, c = SSD_CHUNK, GLA_CHUNK
    tri_q = np.tril(np.ones((q, q), np.float32))
    tri_c = np.tril(np.ones((c, c), np.float32))
    expand = np.zeros((SMALL_W, SSD_WIDTH), np.float32)
    for h in range(SSD_HEADS):
        expand[h, h * SSD_HEAD_DIM:(h + 1) * SSD_HEAD_DIM] = 1.0
    return (jnp.asarray(np.tile(tri_q, (1, 3)), BF16),
            jnp.asarray(np.tile(tri_c, (1, 3)), BF16),
            jnp.asarray(np.tile(expand, (3, 1)), BF16))


def kernel(x, c, w_ada, b_ada, norm_mix_pre, norm_mix_post, norm_ffn_pre, norm_ffn_post, w_in, ssd_conv_w, ssd_conv_b, ssd_dt_bias, ssd_a_log, ssd_d, ssd_norm, gla_gate_w, gla_gate_b, gla_norm, w_out, ffn_up, ffn_conv_w, ffn_conv_b, ffn_down):
    bsz, seqlen, d = x.shape
    depth = w_ada.shape[0]
    tri_q3, tri_c3, expand3 = _constants()
    c_pad = jnp.pad(c, ((0, SUBLANES - bsz % SUBLANES), (0, 0))) if bsz % SUBLANES else c

    sizes = [SSD_WIDTH, SSD_CONV_DIM, SSD_HEADS, GLA_K_WIDTH, GLA_K_WIDTH, GLA_V_WIDTH,
             GLA_GATE_RANK, GLA_V_WIDTH]
    o = np.concatenate([[0], np.cumsum(sizes)])

    for i in range(depth):
        mod = _ada_call(c_pad, w_ada[i], b_ada[i][None, :])[:bsz].reshape(bsz, ADA_CHUNKS, d)

        wbig, wsmall = _reorder_call(jnp.swapaxes(w_in[i], 0, 1), o)
        ssd_consts = (ssd_conv_w[i], ssd_conv_b[i][None, :],
                      _lane_pad(ssd_dt_bias[i][None, :], SMALL_W), _lane_pad(ssd_a_log[i][None, :], SMALL_W),
                      jnp.repeat(ssd_d[i], SSD_HEAD_DIM)[None, :], ssd_norm[i][None, :], tri_q3, expand3)
        gw = jnp.zeros((SMALL_W, GLA_K_WIDTH), F32).at[SSD_HEADS:SSD_HEADS + GLA_GATE_RANK].set(
            gla_gate_w[i]).astype(BF16)
        gla_consts = (gw, gla_gate_b[i][None, :], gla_norm[i][None, :], tri_c3)
        y_ssd, y_gla = _mixer_call(x, mod, norm_mix_pre[i][None, :], wbig, wsmall, ssd_consts, gla_consts)

        x = _ffn_call(y_ssd, y_gla, x, mod, norm_mix_post[i][None, :], norm_ffn_pre[i][None, :],
                      w_out[i].astype(BF16), ffn_up[i].astype(BF16), ffn_conv_w[i], ffn_conv_b[i][None, :],
                      norm_ffn_post[i][None, :], ffn_down[i].astype(BF16))
    return x
```

```python
import functools

import numpy as np
import jax
import jax.numpy as jnp
from jax import lax
from jax.experimental import pallas as pl
from jax.experimental.pallas import tpu as pltpu

F32 = jnp.float32
BF16 = jnp.bfloat16

D_MODEL = 1024
MIX_WIDTH = 2 * D_MODEL
SSD_WIDTH = MIX_WIDTH // 2
GLA_V_WIDTH = MIX_WIDTH - SSD_WIDTH
SSD_HEAD_DIM = 64
SSD_HEADS = SSD_WIDTH // SSD_HEAD_DIM
SSD_GROUPS = 2
SSD_STATE = 128
SSD_CONV = 4
SSD_CHUNK = 128
SSD_CONV_DIM = SSD_WIDTH + 2 * SSD_GROUPS * SSD_STATE
GLA_HEADS = 4
GLA_K_WIDTH = GLA_V_WIDTH // 2
GLA_HEAD_K = GLA_K_WIDTH // GLA_HEADS
GLA_HEAD_V = GLA_V_WIDTH // GLA_HEADS
GLA_GATE_RANK = 16
GLA_GATE_NORM = 16.0
GLA_CHUNK = 64
FFN_HIDDEN = int(round(8 * D_MODEL / 3 / 128)) * 128
FFN_CONV = 3
ADA_CHUNKS = 6
NORM_EPS = 1e-6

LANES = 128
SUBLANES = 8
VMEM_LIMIT_BYTES = 58 * 1024 * 1024

SMALL_W = LANES
GROUP_W = SSD_WIDTH // SSD_GROUPS
HEADS_PER_GROUP = SSD_HEADS // SSD_GROUPS
HALO = SUBLANES


def _silu(x):
    return x * jax.nn.sigmoid(x)


def _softplus(x):
    return jnp.maximum(x, 0.0) + jnp.log(1.0 + jnp.exp(-jnp.abs(x)))


def _rms_scale(x):
    return lax.rsqrt(jnp.mean(x * x, axis=-1, keepdims=True) + NORM_EPS)


def _split3(x):
    hi = x.astype(BF16)
    r1 = x - hi.astype(F32)
    mid = r1.astype(BF16)
    lo = (r1 - mid.astype(F32)).astype(BF16)
    return hi, mid, lo


def _dot(a, b):
    return jnp.dot(a, b, preferred_element_type=F32)


def _dot_nt(a, b):
    return lax.dot_general(a, b, (((1,), (1,)), ((), ())), preferred_element_type=F32)


def _dot_tn(a, b):
    return lax.dot_general(a, b, (((0,), (0,)), ((), ())), preferred_element_type=F32)


def _resident(shape):
    nd = len(shape)
    return pl.BlockSpec(shape, lambda *_: (0,) * nd, pipeline_mode=pl.Buffered(1))


def _params(*sem):
    return pltpu.CompilerParams(dimension_semantics=sem, vmem_limit_bytes=VMEM_LIMIT_BYTES)


def _ada_kernel(c_ref, w_ref, b_ref, o_ref):
    ca = _silu(c_ref[...]).astype(BF16)
    o_ref[...] = _dot(ca, w_ref[...].astype(BF16)) + b_ref[...]


def _ada_call(c_pad, w, b):
    rows, d = c_pad.shape
    n = w.shape[1]
    tn = 1024
    return pl.pallas_call(
        _ada_kernel,
        grid=(n // tn,),
        in_specs=[
            pl.BlockSpec((rows, d), lambda j: (0, 0)),
            pl.BlockSpec((d, tn), lambda j: (0, j)),
            pl.BlockSpec((1, tn), lambda j: (0, j)),
        ],
        out_specs=pl.BlockSpec((rows, tn), lambda j: (0, j)),
        out_shape=jax.ShapeDtypeStruct((rows, n), F32),
        compiler_params=_params("arbitrary"),
        name="ada_mod",
    )(c_pad, w, b)


def _reorder_kernel(wt_ref, big_ref, small_ref, *, offs):
    dst = pl.program_id(0) * _PROJ_CHUNK
    skip_dt = offs[3] - offs[2]
    skip_glr = offs[7] - offs[6]
    src = dst + jnp.where(dst >= offs[2], skip_dt, 0) + jnp.where(dst >= offs[6] - skip_dt, skip_glr, 0)
    rows = wt_ref[pl.ds(pl.multiple_of(src, 2 * SUBLANES), _PROJ_CHUNK), :]
    big_ref[...] = rows.T.astype(BF16)

    @pl.when(pl.program_id(0) == 0)
    def _():
        d = wt_ref.shape[1]
        pad = jnp.zeros((SMALL_W - skip_dt - skip_glr, d), F32)
        small = jnp.concatenate([wt_ref[offs[2]:offs[3], :], wt_ref[offs[6]:offs[7], :], pad], axis=0)
        small_ref[...] = small.T.astype(BF16)


def _reorder_call(wt, offs):
    n_in, d = wt.shape
    return pl.pallas_call(
        functools.partial(_reorder_kernel, offs=tuple(int(v) for v in offs)),
        grid=(_PROJ_W // _PROJ_CHUNK,),
        in_specs=[_resident((n_in, d))],
        out_specs=[pl.BlockSpec((d, _PROJ_CHUNK), lambda j: (0, j)), pl.BlockSpec((d, SMALL_W), lambda j: (0, 0))],
        out_shape=[jax.ShapeDtypeStruct((d, _PROJ_W), BF16), jax.ShapeDtypeStruct((d, SMALL_W), BF16)],
        compiler_params=_params("arbitrary"),
        name="reorder_w_in",
    )(wt)


_Z0 = 0
_XBC0 = _Z0 + SSD_WIDTH
_Q0 = _XBC0 + SSD_CONV_DIM
_K0 = _Q0 + GLA_K_WIDTH
_V0 = _K0 + GLA_K_WIDTH
_GO0 = _V0 + GLA_V_WIDTH
_PROJ_W = _GO0 + GLA_V_WIDTH
_PROJ_CHUNK = 512
_MIX_TILE = SSD_CHUNK


def _project(hb, w_ref, c0, width):
    parts = [_dot(hb, w_ref[:, c:c + min(_PROJ_CHUNK, c0 + width - c)])
             for c in range(c0, c0 + width, _PROJ_CHUNK)]
    return parts[0] if len(parts) == 1 else jnp.concatenate(parts, axis=1)


def _ssd_stages(env, rows, cw_ref, cb_ref, dtb_ref, alog_ref, dexp_ref, nw_ref, tri_ref, exp_ref,
                xpad, state, y_ref):
    q = SSD_CHUNK
    n = SSD_STATE

    xpad[HALO:HALO + q, :] = env["xbc"]
    acc = cb_ref[...]
    for k in range(SSD_CONV):
        off = HALO - (SSD_CONV - 1) + k
        acc = acc + cw_ref[k:k + 1, :] * xpad[off:off + q, :]
    xpad[0:HALO, :] = xpad[q:q + HALO, :]
    xc = _silu(acc)
    xs = xc[:, :SSD_WIDTH]
    bm = xc[:, SSD_WIDTH:SSD_WIDTH + SSD_GROUPS * n].astype(BF16)
    cm = xc[:, SSD_WIDTH + SSD_GROUPS * n:].astype(BF16)
    yield

    lane = lax.broadcasted_iota(jnp.int32, (q, SMALL_W), 1)
    head_lane = lane < SSD_HEADS
    dt = _softplus(env["small"] + dtb_ref[...])
    a = -jnp.exp(alog_ref[...])
    dt = jnp.where(head_lane, dt, 0.0)
    da = jnp.where(head_lane, dt * a, 0.0)
    cs = _dot(tri_ref[...], jnp.concatenate(_split3(da), axis=0))
    cs_t = cs.T
    cs_e = _dot(jnp.concatenate(_split3(cs), axis=1), exp_ref[...])
    dt_e = _dot(jnp.concatenate(_split3(dt), axis=1), exp_ref[...])
    cs_last = cs_e[q - 1:q, :]

    xdt = xs * dt_e
    xdt_b = xdt.astype(BF16)
    xdec_b = (xdt * jnp.exp(cs_last - cs_e)).astype(BF16)
    decay_in = jnp.exp(cs_e)
    decay_chunk = jnp.exp(cs_last)
    yield

    row = lax.broadcasted_iota(jnp.int32, (q, q), 0)
    col = lax.broadcasted_iota(jnp.int32, (q, q), 1)
    causal = row >= col
    first_half = col < SSD_HEAD_DIM

    y_parts = []
    for g in range(SSD_GROUPS):
        cmg = cm[:, g * n:(g + 1) * n]
        bmg = bm[:, g * n:(g + 1) * n]
        gs = slice(g * GROUP_W, (g + 1) * GROUP_W)
        scores = _dot_nt(cmg, bmg)
        y_off = _dot(cmg, state[g].astype(BF16)) * decay_in[:, gs]
        state[g] = decay_chunk[:, gs] * state[g] + _dot_tn(bmg, xdec_b[:, gs])
        diag = []
        for pair in range(HEADS_PER_GROUP // 2):
            h0 = g * HEADS_PER_GROUP + 2 * pair
            ms = []
            for h in (h0, h0 + 1):
                seg = cs[:, h:h + 1] - cs_t[h:h + 1, :]
                decay = jnp.exp(jnp.where(causal, seg, -jnp.inf))
                ms.append((scores * decay).astype(BF16))
            slab = xdt_b[:, h0 * SSD_HEAD_DIM:(h0 + 2) * SSD_HEAD_DIM]
            zero = jnp.zeros_like(slab)
            rhs = jnp.concatenate([jnp.where(first_half, slab, zero),
                                   jnp.where(first_half, zero, slab)], axis=0)
            diag.append(_dot(jnp.concatenate(ms, axis=1), rhs))
        y_parts.append(jnp.concatenate(diag, axis=1) + y_off)
        yield
    y = jnp.concatenate(y_parts, axis=1) + xs * dexp_ref[...]
    y = y * _silu(env["z"])
    outs = []
    for g in range(SSD_GROUPS):
        yg = y[:, g * GROUP_W:(g + 1) * GROUP_W]
        outs.append(yg * _rms_scale(yg))
    y_ref[rows, :] = (jnp.concatenate(outs, axis=1) * nw_ref[...]).astype(BF16)
    yield


def _gla_stages(env, r, rows, gw_ref, gb_ref, nw_ref, tri_ref, state, y_ref):
    c = GLA_CHUNK
    dk, dv = GLA_HEAD_K, GLA_HEAD_V
    row = lax.broadcasted_iota(jnp.int32, (c, c), 0)
    col = lax.broadcasted_iota(jnp.int32, (c, c), 1)
    causal = row >= col

    qf = env["q"][r, :] * (dk ** -0.5)
    kf = env["k"][r, :]
    logit = _dot(env["small"][r, :].astype(BF16), gw_ref[...]) + gb_ref[...]
    lg = -_softplus(-logit) / GLA_GATE_NORM
    bcum = _dot(tri_ref[...], jnp.concatenate(_split3(lg), axis=0))
    blast = bcum[c - 1:c, :]
    qt = (qf * jnp.exp(bcum)).astype(BF16)
    kt = (kf * jnp.exp(-bcum)).astype(BF16)
    kd = (kf * jnp.exp(blast - bcum)).astype(BF16)
    eblast = jnp.exp(blast)
    yield
    for h in range(GLA_HEADS):
        ks = slice(h * dk, (h + 1) * dk)
        vs = slice(h * dv, (h + 1) * dv)
        attn = jnp.where(causal, _dot_nt(qt[:, ks], kt[:, ks]), 0.0)
        vh = env["v"][r, vs]
        o = _dot(attn.astype(BF16), vh) + _dot_nt(qt[:, ks], state[h].astype(BF16))
        state[h] = eblast[:, ks] * state[h] + _dot_tn(vh, kd[:, ks])
        o = o * _rms_scale(o) * nw_ref[...]
        o = o * _silu(env["go"][r, vs])
        y_ref[rows, vs] = o.astype(BF16)
        if h % 2 == 1:
            yield


def _mixer_kernel(x_ref, mod_ref, npre_ref, wbig_ref, wsmall_ref,
                  cw_ref, cb_ref, dtb_ref, alog_ref, dexp_ref, snw_ref, triq_ref, exp_ref,
                  gw_ref, gb_ref, gnw_ref, tric_ref,
                  ys_ref, yg_ref, xpad, sstate, gstate):
    @pl.when(pl.program_id(1) == 0)
    def _():
        xpad[0:HALO, :] = jnp.zeros((HALO, SSD_CONV_DIM), F32)
        sstate[...] = jnp.zeros(sstate.shape, F32)
        gstate[...] = jnp.zeros(gstate.shape, F32)

    for r0 in range(0, x_ref.shape[0], SSD_CHUNK):
        rows = slice(r0, r0 + SSD_CHUNK)
        x = x_ref[rows, :]
        xn = (x * _rms_scale(x)) * npre_ref[...]
        hb = (xn * (1.0 + mod_ref[1:2, :]) + mod_ref[0:1, :]).astype(BF16)

        env = {"small": _dot(hb, wsmall_ref[...]), "xbc": _project(hb, wbig_ref, _XBC0, SSD_CONV_DIM)}
        ssd = _ssd_stages(env, rows, cw_ref, cb_ref, dtb_ref, alog_ref, dexp_ref, snw_ref, triq_ref, exp_ref,
                          xpad, sstate, ys_ref)
        gla = [_gla_stages(env, slice(ci * GLA_CHUNK, (ci + 1) * GLA_CHUNK),
                           slice(r0 + ci * GLA_CHUNK, r0 + (ci + 1) * GLA_CHUNK), gw_ref, gb_ref, gnw_ref,
                           tric_ref, gstate, yg_ref) for ci in range(SSD_CHUNK // GLA_CHUNK)]
        env["q"] = _project(hb, wbig_ref, _Q0, GLA_K_WIDTH)
        env["k"] = _project(hb, wbig_ref, _K0, GLA_K_WIDTH)
        next(ssd)
        env["v"] = _project(hb, wbig_ref, _V0, GLA_V_WIDTH).astype(BF16)
        next(gla[0])
        next(ssd)
        env["go"] = _project(hb, wbig_ref, _GO0, GLA_V_WIDTH)
        next(gla[0])
        next(ssd)
        env["z"] = _project(hb, wbig_ref, _Z0, SSD_WIDTH)
        next(gla[0])
        next(gla[1])
        next(ssd)
        next(gla[1])
        next(ssd)
        next(gla[1])


def _mixer_call(x, mod, npre, wbig, wsmall, ssd_consts, gla_consts):
    b, s, d = x.shape
    t = _MIX_TILE
    row = lambda w: pl.BlockSpec((None, t, w), lambda i, j: (i, j, 0))
    consts = (npre, wbig, wsmall) + tuple(ssd_consts) + tuple(gla_consts)
    return pl.pallas_call(
        _mixer_kernel,
        grid=(b, s // t),
        in_specs=[row(d), pl.BlockSpec((None, ADA_CHUNKS, d), lambda i, j: (i, 0, 0))]
        + [_resident(a.shape) for a in consts],
        out_specs=[row(SSD_WIDTH), row(GLA_V_WIDTH)],
        out_shape=[jax.ShapeDtypeStruct((b, s, SSD_WIDTH), BF16),
                   jax.ShapeDtypeStruct((b, s, GLA_V_WIDTH), BF16)],
        scratch_shapes=[
            pltpu.VMEM((HALO + SSD_CHUNK, SSD_CONV_DIM), F32),
            pltpu.VMEM((SSD_GROUPS, SSD_STATE, GROUP_W), F32),
            pltpu.VMEM((GLA_HEADS, GLA_HEAD_V, GLA_HEAD_K), F32),
        ],
        compiler_params=_params("arbitrary", "arbitrary"),
        name="token_mixer",
    )(x, mod, *consts)


_UP_CHUNK = 384
_UP_DOT = 768
_FFN_TM = 256


def _ffn_kernel(ys_ref, yg_ref, x_ref, mod_ref, npost1_ref, npre_ref, wout_ref, wup_ref,
                cw_ref, cb_ref, npost2_ref, wdown_ref, o_ref, upad, gate):
    tm = x_ref.shape[0]

    @pl.when(pl.program_id(1) == 0)
    def _():
        upad[0:HALO, :] = jnp.zeros((HALO, upad.shape[1]), F32)

    y = _dot(ys_ref[...], wout_ref[0:SSD_WIDTH, :]) + _dot(yg_ref[...], wout_ref[SSD_WIDTH:, :])
    yn = (y * _rms_scale(y)) * npost1_ref[...]
    x1 = x_ref[...] + mod_ref[2:3, :] * yn
    h = ((x1 * _rms_scale(x1)) * npre_ref[...]) * (1.0 + mod_ref[4:5, :]) + mod_ref[3:4, :]
    hb = h.astype(BF16)

    done = set()

    def up(col):
        p = col // _UP_DOT
        if p not in done:
            done.add(p)
            cols = slice(p * _UP_DOT, (p + 1) * _UP_DOT)
            upad[HALO:HALO + tm, cols] = _dot(hb, wup_ref[:, cols])

    def conv(c0):
        cols = slice(c0, c0 + _UP_CHUNK)
        up(c0)
        up(c0 + _UP_CHUNK - 1)
        acc = cb_ref[:, cols]
        for k in range(FFN_CONV):
            off = HALO - (FFN_CONV - 1) + k
            acc = acc + cw_ref[k:k + 1, cols] * upad[off:off + tm, cols]
        upad[0:HALO, cols] = upad[tm:tm + HALO, cols]
        return acc

    for c in range(0, FFN_HIDDEN, _UP_CHUNK):
        gate[:, c:c + _UP_CHUNK] = (_silu(conv(c)) * conv(FFN_HIDDEN + c)).astype(BF16)
    f = _dot(gate[...], wdown_ref[...])
    fn = (f * _rms_scale(f)) * npost2_ref[...]
    o_ref[...] = x1 + mod_ref[5:6, :] * fn


def _ffn_call(ys, yg, x, mod, npost1, npre, wout, wup, cw, cb, npost2, wdown):
    b, s, d = x.shape
    tm = _FFN_TM
    row = lambda w: pl.BlockSpec((None, tm, w), lambda i, j: (i, j, 0))
    consts = (npost1, npre, wout, wup, cw, cb, npost2, wdown)
    return pl.pallas_call(
        _ffn_kernel,
        grid=(b, s // tm),
        in_specs=[row(SSD_WIDTH), row(GLA_V_WIDTH), row(d),
                  pl.BlockSpec((None, ADA_CHUNKS, d), lambda i, j: (i, 0, 0))]
        + [_resident(a.shape) for a in consts],
        out_specs=row(d),
        out_shape=jax.ShapeDtypeStruct((b, s, d), F32),
        scratch_shapes=[pltpu.VMEM((HALO + tm, 2 * FFN_HIDDEN), F32),
                        pltpu.VMEM((tm, FFN_HIDDEN), BF16)],
        compiler_params=_params("arbitrary", "arbitrary"),
        name="channel_mixer",
    )(ys, yg, x, mod, *consts)


def _lane_pad(v, width):
    return jnp.pad(v, ((0, 0), (0, width - v.shape[1])))


def _constants():
    q, c = SSD_CHUNK, GLA_CHUNK
    tri_q = np.tril(np.ones((q, q), np.float32))
    tri_c = np.tril(np.ones((c, c), np.float32))
    expand = np.zeros((SMALL_W, SSD_WIDTH), np.float32)
    for h in range(SSD_HEADS):
        expand[h, h * SSD_HEAD_DIM:(h + 1) * SSD_HEAD_DIM] = 1.0
    return (jnp.asarray(np.tile(tri_q, (1, 3)), BF16),
            jnp.asarray(np.tile(tri_c, (1, 3)), BF16),
            jnp.asarray(np.tile(expand, (3, 1)), BF16))


def kernel(x, c, w_ada, b_ada, norm_mix_pre, norm_mix_post, norm_ffn_pre, norm_ffn_post, w_in, ssd_conv_w, ssd_conv_b, ssd_dt_bias, ssd_a_log, ssd_d, ssd_norm, gla_gate_w, gla_gate_b, gla_norm, w_out, ffn_up, ffn_conv_w, ffn_conv_b, ffn_down):
    bsz, seqlen, d = x.shape
    depth = w_ada.shape[0]
    tri_q3, tri_c3, expand3 = _constants()
    c_pad = jnp.pad(c, ((0, SUBLANES - bsz % SUBLANES), (0, 0))) if bsz % SUBLANES else c

    sizes = [SSD_WIDTH, SSD_CONV_DIM, SSD_HEADS, GLA_K_WIDTH, GLA_K_WIDTH, GLA_V_WIDTH,
             GLA_GATE_RANK, GLA_V_WIDTH]
    o = np.concatenate([[0], np.cumsum(sizes)])

    for i in range(depth):
        mod = _ada_call(c_pad, w_ada[i], b_ada[i][None, :])[:bsz].reshape(bsz, ADA_CHUNKS, d)

        wbig, wsmall = _reorder_call(jnp.swapaxes(w_in[i], 0, 1), o)
        ssd_consts = (ssd_conv_w[i], ssd_conv_b[i][None, :],
                      _lane_pad(ssd_dt_bias[i][None, :], SMALL_W), _lane_pad(ssd_a_log[i][None, :], SMALL_W),
                      jnp.repeat(ssd_d[i], SSD_HEAD_DIM)[None, :], ssd_norm[i][None, :], tri_q3, expand3)
        gw = jnp.zeros((SMALL_W, GLA_K_WIDTH), F32).at[SSD_HEADS:SSD_HEADS + GLA_GATE_RANK].set(
            gla_gate_w[i]).astype(BF16)
        gla_consts = (gw, gla_gate_b[i][None, :], gla_norm[i][None, :], tri_c3)
        y_ssd, y_gla = _mixer_call(x, mod, norm_mix_pre[i][None, :], wbig, wsmall, ssd_consts, gla_consts)

        x = _ffn_call(y_ssd, y_gla, x, mod, norm_mix_post[i][None, :], norm_ffn_pre[i][None, :],
                      w_out[i].astype(BF16), ffn_up[i].astype(BF16), ffn_conv_w[i], ffn_conv_b[i][None, :],
                      norm_ffn_post[i][None, :], ffn_down[i].astype(BF16))
    return x
```

```python
import functools

import numpy as np
import jax
import jax.numpy as jnp
from jax import lax
from jax.experimental import pallas as pl
from jax.experimental.pallas import tpu as pltpu

F32 = jnp.float32
BF16 = jnp.bfloat16

D_MODEL = 1024
MIX_WIDTH = 2 * D_MODEL
SSD_WIDTH = MIX_WIDTH // 2
GLA_V_WIDTH = MIX_WIDTH - SSD_WIDTH
SSD_HEAD_DIM = 64
SSD_HEADS = SSD_WIDTH // SSD_HEAD_DIM
SSD_GROUPS = 2
SSD_STATE = 128
SSD_CONV = 4
SSD_CHUNK = 128
SSD_CONV_DIM = SSD_WIDTH + 2 * SSD_GROUPS * SSD_STATE
GLA_HEADS = 4
GLA_K_WIDTH = GLA_V_WIDTH // 2
GLA_HEAD_K = GLA_K_WIDTH // GLA_HEADS
GLA_HEAD_V = GLA_V_WIDTH // GLA_HEADS
GLA_GATE_RANK = 16
GLA_GATE_NORM = 16.0
GLA_CHUNK = 64
FFN_HIDDEN = int(round(8 * D_MODEL / 3 / 128)) * 128
FFN_CONV = 3
ADA_CHUNKS = 6
NORM_EPS = 1e-6
_LOG2E = float(np.log2(np.e))

LANES = 128
SUBLANES = 8
VMEM_LIMIT_BYTES = 58 * 1024 * 1024

SMALL_W = LANES
GROUP_W = SSD_WIDTH // SSD_GROUPS
HEADS_PER_GROUP = SSD_HEADS // SSD_GROUPS
HALO = SUBLANES


def _silu(x):
    return x * jax.nn.sigmoid(x)


def _softplus(x):
    return jnp.maximum(x, 0.0) + jnp.log(1.0 + jnp.exp(-jnp.abs(x)))


def _rms_scale(x):
    return lax.rsqrt(jnp.mean(x * x, axis=-1, keepdims=True) + NORM_EPS)


def _split3(x):
    hi = x.astype(BF16)
    r1 = x - hi.astype(F32)
    mid = r1.astype(BF16)
    lo = (r1 - mid.astype(F32)).astype(BF16)
    return hi, mid, lo


def _dot(a, b):
    return jnp.dot(a, b, preferred_element_type=F32)


def _dot_nt(a, b):
    return lax.dot_general(a, b, (((1,), (1,)), ((), ())), preferred_element_type=F32)


def _dot_tn(a, b):
    return lax.dot_general(a, b, (((0,), (0,)), ((), ())), preferred_element_type=F32)


def _resident(shape):
    nd = len(shape)
    return pl.BlockSpec(shape, lambda *_: (0,) * nd, pipeline_mode=pl.Buffered(1))


def _params(*sem):
    return pltpu.CompilerParams(dimension_semantics=sem, vmem_limit_bytes=VMEM_LIMIT_BYTES)


def _ada_kernel(c_ref, w_ref, b_ref, o_ref):
    ca = _silu(c_ref[...]).astype(BF16)
    o_ref[...] = _dot(ca, w_ref[...].astype(BF16)) + b_ref[...]


def _ada_call(c_pad, w, b):
    rows, d = c_pad.shape
    n = w.shape[1]
    tn = 1024
    return pl.pallas_call(
        _ada_kernel,
        grid=(n // tn,),
        in_specs=[
            pl.BlockSpec((rows, d), lambda j: (0, 0)),
            pl.BlockSpec((d, tn), lambda j: (0, j)),
            pl.BlockSpec((1, tn), lambda j: (0, j)),
        ],
        out_specs=pl.BlockSpec((rows, tn), lambda j: (0, j)),
        out_shape=jax.ShapeDtypeStruct((rows, n), F32),
        compiler_params=_params("arbitrary"),
        name="ada_mod",
    )(c_pad, w, b)


def _reorder_kernel(wt_ref, big_ref, small_ref, *, offs):
    dst = pl.program_id(0) * _PROJ_CHUNK
    skip_dt = offs[3] - offs[2]
    skip_glr = offs[7] - offs[6]
    src = dst + jnp.where(dst >= offs[2], skip_dt, 0) + jnp.where(dst >= offs[6] - skip_dt, skip_glr, 0)
    rows = wt_ref[pl.ds(pl.multiple_of(src, 2 * SUBLANES), _PROJ_CHUNK), :]
    big_ref[...] = rows.T.astype(BF16)

    @pl.when(pl.program_id(0) == 0)
    def _():
        d = wt_ref.shape[1]
        pad = jnp.zeros((SMALL_W - skip_dt - skip_glr, d), F32)
        small = jnp.concatenate([wt_ref[offs[2]:offs[3], :], wt_ref[offs[6]:offs[7], :], pad], axis=0)
        small_ref[...] = small.T.astype(BF16)


def _reorder_call(wt, offs):
    n_in, d = wt.shape
    return pl.pallas_call(
        functools.partial(_reorder_kernel, offs=tuple(int(v) for v in offs)),
        grid=(_PROJ_W // _PROJ_CHUNK,),
        in_specs=[_resident((n_in, d))],
        out_specs=[pl.BlockSpec((d, _PROJ_CHUNK), lambda j: (0, j)), pl.BlockSpec((d, SMALL_W), lambda j: (0, 0))],
        out_shape=[jax.ShapeDtypeStruct((d, _PROJ_W), BF16), jax.ShapeDtypeStruct((d, SMALL_W), BF16)],
        compiler_params=_params("arbitrary"),
        name="reorder_w_in",
    )(wt)


_Z0 = 0
_XBC0 = _Z0 + SSD_WIDTH
_Q0 = _XBC0 + SSD_CONV_DIM
_K0 = _Q0 + GLA_K_WIDTH
_V0 = _K0 + GLA_K_WIDTH
_GO0 = _V0 + GLA_V_WIDTH
_PROJ_W = _GO0 + GLA_V_WIDTH
_PROJ_CHUNK = 512
_MIX_TILE = SSD_CHUNK


def _project(hb, w_ref, c0, width):
    parts = [_dot(hb, w_ref[:, c:c + min(_PROJ_CHUNK, c0 + width - c)])
             for c in range(c0, c0 + width, _PROJ_CHUNK)]
    return parts[0] if len(parts) == 1 else jnp.concatenate(parts, axis=1)


def _ssd_stages(env, rows, cw_ref, cb_ref, dtb_ref, alog_ref, dexp_ref, nw_ref, tri_ref, exp_ref,
                xpad, state, y_ref):
    q = SSD_CHUNK
    n = SSD_STATE

    xpad[HALO:HALO + q, :] = env["xbc"]
    acc = cb_ref[...]
    for k in range(SSD_CONV):
        off = HALO - (SSD_CONV - 1) + k
        acc = acc + cw_ref[k:k + 1, :] * xpad[off:off + q, :]
    xpad[0:HALO, :] = xpad[q:q + HALO, :]
    xc = _silu(acc)
    xs = xc[:, :SSD_WIDTH]
    bm = xc[:, SSD_WIDTH:SSD_WIDTH + SSD_GROUPS * n].astype(BF16)
    cm = xc[:, SSD_WIDTH + SSD_GROUPS * n:].astype(BF16)
    yield

    lane = lax.broadcasted_iota(jnp.int32, (q, SMALL_W), 1)
    head_lane = lane < SSD_HEADS
    dt = _softplus(env["small"] + dtb_ref[...])
    a = -jnp.exp(alog_ref[...]) * _LOG2E
    dt = jnp.where(head_lane, dt, 0.0)
    da = jnp.where(head_lane, dt * a, 0.0)
    cs = _dot(tri_ref[...], jnp.concatenate(_split3(da), axis=0))
    cs_t = cs.T
    cs_e = _dot(jnp.concatenate(_split3(cs), axis=1), exp_ref[...])
    dt_e = _dot(jnp.concatenate(_split3(dt), axis=1), exp_ref[...])
    cs_last = cs_e[q - 1:q, :]

    xdt = xs * dt_e
    xdt_b = xdt.astype(BF16)
    xdec_b = (xdt * jnp.exp2(cs_last - cs_e)).astype(BF16)
    decay_in = jnp.exp2(cs_e)
    decay_chunk = jnp.exp2(cs_last)
    yield

    row = lax.broadcasted_iota(jnp.int32, (q, q), 0)
    col = lax.broadcasted_iota(jnp.int32, (q, q), 1)
    causal = row >= col
    first_half = col < SSD_HEAD_DIM

    y_parts = []
    for g in range(SSD_GROUPS):
        cmg = cm[:, g * n:(g + 1) * n]
        bmg = bm[:, g * n:(g + 1) * n]
        gs = slice(g * GROUP_W, (g + 1) * GROUP_W)
        scores = _dot_nt(cmg, bmg)
        y_off = _dot(cmg, state[g].astype(BF16)) * decay_in[:, gs]
        state[g] = decay_chunk[:, gs] * state[g] + _dot_tn(bmg, xdec_b[:, gs])
        diag = []
        for pair in range(HEADS_PER_GROUP // 2):
            h0 = g * HEADS_PER_GROUP + 2 * pair
            ms = []
            for h in (h0, h0 + 1):
                seg = cs[:, h:h + 1] - cs_t[h:h + 1, :]
                decay = jnp.exp2(jnp.where(causal, seg, -jnp.inf))
                ms.append((scores * decay).astype(BF16))
            slab = xdt_b[:, h0 * SSD_HEAD_DIM:(h0 + 2) * SSD_HEAD_DIM]
            zero = jnp.zeros_like(slab)
            rhs = jnp.concatenate([jnp.where(first_half, slab, zero),
                                   jnp.where(first_half, zero, slab)], axis=0)
            diag.append(_dot(jnp.concatenate(ms, axis=1), rhs))
        y_parts.append(jnp.concatenate(diag, axis=1) + y_off)
        yield
    y = jnp.concatenate(y_parts, axis=1) + xs * dexp_ref[...]
    y = y * _silu(env["z"])
    outs = []
    for g in range(SSD_GROUPS):
        yg = y[:, g * GROUP_W:(g + 1) * GROUP_W]
        outs.append(yg * _rms_scale(yg))
    y_ref[rows, :] = (jnp.concatenate(outs, axis=1) * nw_ref[...]).astype(BF16)
    yield


def _gla_stages(env, r, rows, gw_ref, gb_ref, nw_ref, tri_ref, state, y_ref):
    c = GLA_CHUNK
    dk, dv = GLA_HEAD_K, GLA_HEAD_V
    row = lax.broadcasted_iota(jnp.int32, (c, c), 0)
    col = lax.broadcasted_iota(jnp.int32, (c, c), 1)
    causal = row >= col

    qf = env["q"][r, :] * (dk ** -0.5)
    kf = env["k"][r, :]
    logit = _dot(env["small"][r, :].astype(BF16), gw_ref[...]) + gb_ref[...]
    lg = -_softplus(-logit) * (_LOG2E / GLA_GATE_NORM)
    bcum = _dot(tri_ref[...], jnp.concatenate(_split3(lg), axis=0))
    blast = bcum[c - 1:c, :]
    qt = (qf * jnp.exp2(bcum)).astype(BF16)
    kt = (kf * jnp.exp2(-bcum)).astype(BF16)
    kd = (kf * jnp.exp2(blast - bcum)).astype(BF16)
    eblast = jnp.exp2(blast)
    yield
    for h in range(GLA_HEADS):
        ks = slice(h * dk, (h + 1) * dk)
        vs = slice(h * dv, (h + 1) * dv)
        attn = jnp.where(causal, _dot_nt(qt[:, ks], kt[:, ks]), 0.0)
        vh = env["v"][r, vs]
        o = _dot(attn.astype(BF16), vh) + _dot_nt(qt[:, ks], state[h].astype(BF16))
        state[h] = eblast[:, ks] * state[h] + _dot_tn(vh, kd[:, ks])
        o = o * _rms_scale(o) * nw_ref[...]
        o = o * _silu(env["go"][r, vs])
        y_ref[rows, vs] = o.astype(BF16)
        if h % 2 == 1:
            yield


def _mixer_kernel(x_ref, mod_ref, npre_ref, wbig_ref, wsmall_ref,
                  cw_ref, cb_ref, dtb_ref, alog_ref, dexp_ref, snw_ref, triq_ref, exp_ref,
                  gw_ref, gb_ref, gnw_ref, tric_ref,
                  ys_ref, yg_ref, xpad, sstate, gstate):
    @pl.when(pl.program_id(1) == 0)
    def _():
        xpad[0:HALO, :] = jnp.zeros((HALO, SSD_CONV_DIM), F32)
        sstate[...] = jnp.zeros(sstate.shape, F32)
        gstate[...] = jnp.zeros(gstate.shape, F32)

    for r0 in range(0, x_ref.shape[0], SSD_CHUNK):
        rows = slice(r0, r0 + SSD_CHUNK)
        x = x_ref[rows, :]
        xn = (x * _rms_scale(x)) * npre_ref[...]
        hb = (xn * (1.0 + mod_ref[1:2, :]) + mod_ref[0:1, :]).astype(BF16)

        env = {"small": _dot(hb, wsmall_ref[...]), "xbc": _project(hb, wbig_ref, _XBC0, SSD_CONV_DIM)}
        ssd = _ssd_stages(env, rows, cw_ref, cb_ref, dtb_ref, alog_ref, dexp_ref, snw_ref, triq_ref, exp_ref,
                          xpad, sstate, ys_ref)
        gla = [_gla_stages(env, slice(ci * GLA_CHUNK, (ci + 1) * GLA_CHUNK),
                           slice(r0 + ci * GLA_CHUNK, r0 + (ci + 1) * GLA_CHUNK), gw_ref, gb_ref, gnw_ref,
                           tric_ref, gstate, yg_ref) for ci in range(SSD_CHUNK // GLA_CHUNK)]
        env["q"] = _project(hb, wbig_ref, _Q0, GLA_K_WIDTH)
        env["k"] = _project(hb, wbig_ref, _K0, GLA_K_WIDTH)
        next(ssd)
        env["v"] = _project(hb, wbig_ref, _V0, GLA_V_WIDTH).astype(BF16)
        next(gla[0])
        next(ssd)
        env["go"] = _project(hb, wbig_ref, _GO0, GLA_V_WIDTH)
        next(gla[0])
        next(ssd)
        env["z"] = _project(hb, wbig_ref, _Z0, SSD_WIDTH)
        next(gla[0])
        next(gla[1])
        next(ssd)
        next(gla[1])
        next(ssd)
        next(gla[1])


def _mixer_call(x, mod, npre, wbig, wsmall, ssd_consts, gla_consts):
    b, s, d = x.shape
    t = _MIX_TILE
    row = lambda w: pl.BlockSpec((None, t, w), lambda i, j: (i, j, 0))
    consts = (npre, wbig, wsmall) + tuple(ssd_consts) + tuple(gla_consts)
    return pl.pallas_call(
        _mixer_kernel,
        grid=(b, s // t),
        in_specs=[row(d), pl.BlockSpec((None, ADA_CHUNKS, d), lambda i, j: (i, 0, 0))]
        + [_resident(a.shape) for a in consts],
        out_specs=[row(SSD_WIDTH), row(GLA_V_WIDTH)],
        out_shape=[jax.ShapeDtypeStruct((b, s, SSD_WIDTH), BF16),
                   jax.ShapeDtypeStruct((b, s, GLA_V_WIDTH), BF16)],
        scratch_shapes=[
            pltpu.VMEM((HALO + SSD_CHUNK, SSD_CONV_DIM), F32),
            pltpu.VMEM((SSD_GROUPS, SSD_STATE, GROUP_W), F32),
            pltpu.VMEM((GLA_HEADS, GLA_HEAD_V, GLA_HEAD_K), F32),
        ],
        compiler_params=_params("arbitrary", "arbitrary"),
        name="token_mixer",
    )(x, mod, *consts)


_UP_CHUNK = 384
_UP_DOT = 768
_FFN_TM = 256


def _ffn_kernel(ys_ref, yg_ref, x_ref, mod_ref, npost1_ref, npre_ref, wout_ref, wup_ref,
                cw_ref, cb_ref, npost2_ref, wdown_ref, o_ref, upad, gate):
    tm = x_ref.shape[0]

    @pl.when(pl.program_id(1) == 0)
    def _():
        upad[0:HALO, :] = jnp.zeros((HALO, upad.shape[1]), F32)

    y = _dot(ys_ref[...], wout_ref[0:SSD_WIDTH, :]) + _dot(yg_ref[...], wout_ref[SSD_WIDTH:, :])
    yn = (y * _rms_scale(y)) * npost1_ref[...]
    x1 = x_ref[...] + mod_ref[2:3, :] * yn
    h = ((x1 * _rms_scale(x1)) * npre_ref[...]) * (1.0 + mod_ref[4:5, :]) + mod_ref[3:4, :]
    hb = h.astype(BF16)

    done = set()

    def up(col):
        p = col // _UP_DOT
        if p not in done:
            done.add(p)
            cols = slice(p * _UP_DOT, (p + 1) * _UP_DOT)
            upad[HALO:HALO + tm, cols] = _dot(hb, wup_ref[:, cols])

    def conv(c0):
        cols = slice(c0, c0 + _UP_CHUNK)
        up(c0)
        up(c0 + _UP_CHUNK - 1)
        acc = cb_ref[:, cols]
        for k in range(FFN_CONV):
            off = HALO - (FFN_CONV - 1) + k
            acc = acc + cw_ref[k:k + 1, cols] * upad[off:off + tm, cols]
        upad[0:HALO, cols] = upad[tm:tm + HALO, cols]
        return acc

    for c in range(0, FFN_HIDDEN, _UP_CHUNK):
        gate[:, c:c + _UP_CHUNK] = (_silu(conv(c)) * conv(FFN_HIDDEN + c)).astype(BF16)
    f = _dot(gate[...], wdown_ref[...])
    fn = (f * _rms_scale(f)) * npost2_ref[...]
    o_ref[...] = x1 + mod_ref[5:6, :] * fn


def _ffn_call(ys, yg, x, mod, npost1, npre, wout, wup, cw, cb, npost2, wdown):
    b, s, d = x.shape
    tm = _FFN_TM
    row = lambda w: pl.BlockSpec((None, tm, w), lambda i, j: (i, j, 0))
    consts = (npost1, npre, wout, wup, cw, cb, npost2, wdown)
    return pl.pallas_call(
        _ffn_kernel,
        grid=(b, s // tm),
        in_specs=[row(SSD_WIDTH), row(GLA_V_WIDTH), row(d),
                  pl.BlockSpec((None, ADA_CHUNKS, d), lambda i, j: (i, 0, 0))]
        + [_resident(a.shape) for a in consts],
        out_specs=row(d),
        out_shape=jax.ShapeDtypeStruct((b, s, d), F32),
        scratch_shapes=[pltpu.VMEM((HALO + tm, 2 * FFN_HIDDEN), F32),
                        pltpu.VMEM((tm, FFN_HIDDEN), BF16)],
        compiler_params=_params("arbitrary", "arbitrary"),
        name="channel_mixer",
    )(ys, yg, x, mod, *consts)


def _lane_pad(v, width):
    return jnp.pad(v, ((0, 0), (0, width - v.shape[1])))


def _constants():
    q, c = SSD_CHUNK, GLA_CHUNK
    tri_q = np.tril(np.ones((q, q), np.float32))
    tri_c = np.tril(np.ones((c, c), np.float32))
    expand = np.zeros((SMALL_W, SSD_WIDTH), np.float32)
    for h in range(SSD_HEADS):
        expand[h, h * SSD_HEAD_DIM:(h + 1) * SSD_HEAD_DIM] = 1.0
    return (jnp.asarray(np.tile(tri_q, (1, 3)), BF16),
            jnp.asarray(np.tile(tri_c, (1, 3)), BF16),
            jnp.asarray(np.tile(expand, (3, 1)), BF16))


def kernel(x, c, w_ada, b_ada, norm_mix_pre, norm_mix_post, norm_ffn_pre, norm_ffn_post, w_in, ssd_conv_w, ssd_conv_b, ssd_dt_bias, ssd_a_log, ssd_d, ssd_norm, gla_gate_w, gla_gate_b, gla_norm, w_out, ffn_up, ffn_conv_w, ffn_conv_b, ffn_down):
    bsz, seqlen, d = x.shape
    depth = w_ada.shape[0]
    tri_q3, tri_c3, expand3 = _constants()
    c_pad = jnp.pad(c, ((0, SUBLANES - bsz % SUBLANES), (0, 0))) if bsz % SUBLANES else c

    sizes = [SSD_WIDTH, SSD_CONV_DIM, SSD_HEADS, GLA_K_WIDTH, GLA_K_WIDTH, GLA_V_WIDTH,
             GLA_GATE_RANK, GLA_V_WIDTH]
    o = np.concatenate([[0], np.cumsum(sizes)])

    for i in range(depth):
        mod = _ada_call(c_pad, w_ada[i], b_ada[i][None, :])[:bsz].reshape(bsz, ADA_CHUNKS, d)

        wbig, wsmall = _reorder_call(jnp.swapaxes(w_in[i], 0, 1), o)
        ssd_consts = (ssd_conv_w[i], ssd_conv_b[i][None, :],
                      _lane_pad(ssd_dt_bias[i][None, :], SMALL_W), _lane_pad(ssd_a_log[i][None, :], SMALL_W),
                      jnp.repeat(ssd_d[i], SSD_HEAD_DIM)[None, :], ssd_norm[i][None, :], tri_q3, expand3)
        gw = jnp.zeros((SMALL_W, GLA_K_WIDTH), F32).at[SSD_HEADS:SSD_HEADS + GLA_GATE_RANK].set(
            gla_gate_w[i]).astype(BF16)
        gla_consts = (gw, gla_gate_b[i][None, :], gla_norm[i][None, :], tri_c3)
        y_ssd, y_gla = _mixer_call(x, mod, norm_mix_pre[i][None, :], wbig, wsmall, ssd_consts, gla_consts)

        x = _ffn_call(y_ssd, y_gla, x, mod, norm_mix_post[i][None, :], norm_ffn_pre[i][None, :],
                      w_out[i].astype(BF16), ffn_up[i].astype(BF16), ffn_conv_w[i], ffn_conv_b[i][None, :],
                      norm_ffn_post[i][None, :], ffn_down[i].astype(BF16))
    return x
```

```python
import functools

import numpy as np
import jax
import jax.numpy as jnp
from jax import lax
from jax.experimental import pallas as pl
from jax.experimental.pallas import tpu as pltpu

F32 = jnp.float32
BF16 = jnp.bfloat16

D_MODEL = 1024
MIX_WIDTH = 2 * D_MODEL
SSD_WIDTH = MIX_WIDTH // 2
GLA_V_WIDTH = MIX_WIDTH - SSD_WIDTH
SSD_HEAD_DIM = 64
SSD_HEADS = SSD_WIDTH // SSD_HEAD_DIM
SSD_GROUPS = 2
SSD_STATE = 128
SSD_CONV = 4
SSD_CHUNK = 128
SSD_CONV_DIM = SSD_WIDTH + 2 * SSD_GROUPS * SSD_STATE
GLA_HEADS = 4
GLA_K_WIDTH = GLA_V_WIDTH // 2
GLA_HEAD_K = GLA_K_WIDTH // GLA_HEADS
GLA_HEAD_V = GLA_V_WIDTH // GLA_HEADS
GLA_GATE_RANK = 16
GLA_GATE_NORM = 16.0
GLA_CHUNK = 64
FFN_HIDDEN = int(round(8 * D_MODEL / 3 / 128)) * 128
FFN_CONV = 3
ADA_CHUNKS = 6
NORM_EPS = 1e-6
_LOG2E = float(np.log2(np.e))

LANES = 128
SUBLANES = 8
VMEM_LIMIT_BYTES = 58 * 1024 * 1024

SMALL_W = LANES
GROUP_W = SSD_WIDTH // SSD_GROUPS
HEADS_PER_GROUP = SSD_HEADS // SSD_GROUPS
HALO = SUBLANES


def _silu(x):
    return x * jax.nn.sigmoid(x)


def _softplus(x):
    return jnp.maximum(x, 0.0) + jnp.log(1.0 + jnp.exp(-jnp.abs(x)))


def _rms_scale(x):
    return lax.rsqrt(jnp.mean(x * x, axis=-1, keepdims=True) + NORM_EPS)


def _split3(x):
    hi = x.astype(BF16)
    r1 = x - hi.astype(F32)
    mid = r1.astype(BF16)
    lo = (r1 - mid.astype(F32)).astype(BF16)
    return hi, mid, lo


def _dot(a, b):
    return jnp.dot(a, b, preferred_element_type=F32)


def _dot_nt(a, b):
    return lax.dot_general(a, b, (((1,), (1,)), ((), ())), preferred_element_type=F32)


def _dot_tn(a, b):
    return lax.dot_general(a, b, (((0,), (0,)), ((), ())), preferred_element_type=F32)


def _resident(shape):
    nd = len(shape)
    return pl.BlockSpec(shape, lambda *_: (0,) * nd, pipeline_mode=pl.Buffered(1))


def _params(*sem):
    return pltpu.CompilerParams(dimension_semantics=sem, vmem_limit_bytes=VMEM_LIMIT_BYTES)


def _ada_kernel(c_ref, w_ref, b_ref, o_ref):
    ca = _silu(c_ref[...]).astype(BF16)
    o_ref[...] = _dot(ca, w_ref[...].astype(BF16)) + b_ref[...]


def _ada_call(c_pad, w, b):
    rows, d = c_pad.shape
    n = w.shape[1]
    tn = 1024
    return pl.pallas_call(
        _ada_kernel,
        grid=(n // tn,),
        in_specs=[
            pl.BlockSpec((rows, d), lambda j: (0, 0)),
            pl.BlockSpec((d, tn), lambda j: (0, j)),
            pl.BlockSpec((1, tn), lambda j: (0, j)),
        ],
        out_specs=pl.BlockSpec((rows, tn), lambda j: (0, j)),
        out_shape=jax.ShapeDtypeStruct((rows, n), F32),
        compiler_params=_params("arbitrary"),
        name="ada_mod",
    )(c_pad, w, b)


def _reorder_kernel(wt_ref, big_ref, small_ref, *, offs):
    dst = pl.program_id(0) * _PROJ_CHUNK
    skip_dt = offs[3] - offs[2]
    skip_glr = offs[7] - offs[6]
    src = dst + jnp.where(dst >= offs[2], skip_dt, 0) + jnp.where(dst >= offs[6] - skip_dt, skip_glr, 0)
    rows = wt_ref[pl.ds(pl.multiple_of(src, 2 * SUBLANES), _PROJ_CHUNK), :]
    big_ref[...] = rows.T.astype(BF16)

    @pl.when(pl.program_id(0) == 0)
    def _():
        d = wt_ref.shape[1]
        pad = jnp.zeros((SMALL_W - skip_dt - skip_glr, d), F32)
        small = jnp.concatenate([wt_ref[offs[2]:offs[3], :], wt_ref[offs[6]:offs[7], :], pad], axis=0)
        small_ref[...] = small.T.astype(BF16)


def _reorder_call(wt, offs):
    n_in, d = wt.shape
    return pl.pallas_call(
        functools.partial(_reorder_kernel, offs=tuple(int(v) for v in offs)),
        grid=(_PROJ_W // _PROJ_CHUNK,),
        in_specs=[_resident((n_in, d))],
        out_specs=[pl.BlockSpec((d, _PROJ_CHUNK), lambda j: (0, j)), pl.BlockSpec((d, SMALL_W), lambda j: (0, 0))],
        out_shape=[jax.ShapeDtypeStruct((d, _PROJ_W), BF16), jax.ShapeDtypeStruct((d, SMALL_W), BF16)],
        compiler_params=_params("arbitrary"),
        name="reorder_w_in",
    )(wt)


_Z0 = 0
_XBC0 = _Z0 + SSD_WIDTH
_Q0 = _XBC0 + SSD_CONV_DIM
_K0 = _Q0 + GLA_K_WIDTH
_V0 = _K0 + GLA_K_WIDTH
_GO0 = _V0 + GLA_V_WIDTH
_PROJ_W = _GO0 + GLA_V_WIDTH
_PROJ_CHUNK = 512
_MIX_TILE = SSD_CHUNK


def _project(hb, w_ref, c0, width):
    parts = [_dot(hb, w_ref[:, c:c + min(_PROJ_CHUNK, c0 + width - c)])
             for c in range(c0, c0 + width, _PROJ_CHUNK)]
    return parts[0] if len(parts) == 1 else jnp.concatenate(parts, axis=1)


def _ssd_stages(env, rows, cw_ref, cb_ref, dtb_ref, alog_ref, dexp_ref, nw_ref, tri_ref, exp_ref,
                xpad, state, y_ref):
    q = SSD_CHUNK
    n = SSD_STATE

    xpad[HALO:HALO + q, :] = env["xbc"]
    acc = cb_ref[...]
    for k in range(SSD_CONV):
        off = HALO - (SSD_CONV - 1) + k
        acc = acc + cw_ref[k:k + 1, :] * xpad[off:off + q, :]
    xpad[0:HALO, :] = xpad[q:q + HALO, :]
    xc = _silu(acc)
    xs = xc[:, :SSD_WIDTH]
    bm = xc[:, SSD_WIDTH:SSD_WIDTH + SSD_GROUPS * n].astype(BF16)
    cm = xc[:, SSD_WIDTH + SSD_GROUPS * n:].astype(BF16)
    yield

    lane = lax.broadcasted_iota(jnp.int32, (q, SMALL_W), 1)
    head_lane = lane < SSD_HEADS
    dt = _softplus(env["small"] + dtb_ref[...])
    a = -jnp.exp(alog_ref[...]) * _LOG2E
    dt = jnp.where(head_lane, dt, 0.0)
    da = jnp.where(head_lane, dt * a, 0.0)
    cs = _dot(tri_ref[...], jnp.concatenate(_split3(da), axis=0))
    cs_t = cs.T
    cs_e = _dot(jnp.concatenate(_split3(cs), axis=1), exp_ref[...])
    dt_e = _dot(jnp.concatenate(_split3(dt), axis=1), exp_ref[...])
    cs_last = cs_e[q - 1:q, :]

    xdt = xs * dt_e
    xdt_b = xdt.astype(BF16)
    xdec_b = (xdt * jnp.exp2(cs_last - cs_e)).astype(BF16)
    decay_in = jnp.exp2(cs_e)
    decay_chunk = jnp.exp2(cs_last)
    yield

    row = lax.broadcasted_iota(jnp.int32, (q, q), 0)
    col = lax.broadcasted_iota(jnp.int32, (q, q), 1)
    causal = row >= col
    first_half = col < SSD_HEAD_DIM

    y_parts = []
    for g in range(SSD_GROUPS):
        cmg = cm[:, g * n:(g + 1) * n]
        bmg = bm[:, g * n:(g + 1) * n]
        gs = slice(g * GROUP_W, (g + 1) * GROUP_W)
        scores = _dot_nt(cmg, bmg)
        y_off = _dot(cmg, state[g].astype(BF16)) * decay_in[:, gs]
        state[g] = decay_chunk[:, gs] * state[g] + _dot_tn(bmg, xdec_b[:, gs])
        diag = []
        for pair in range(HEADS_PER_GROUP // 2):
            h0 = g * HEADS_PER_GROUP + 2 * pair
            ms = []
            for h in (h0, h0 + 1):
                seg = cs[:, h:h + 1] - cs_t[h:h + 1, :]
                decay = jnp.exp2(jnp.where(causal, seg, -jnp.inf))
                ms.append((scores * decay).astype(BF16))
            slab = xdt_b[:, h0 * SSD_HEAD_DIM:(h0 + 2) * SSD_HEAD_DIM]
            zero = jnp.zeros_like(slab)
            rhs = jnp.concatenate([jnp.where(first_half, slab, zero),
                                   jnp.where(first_half, zero, slab)], axis=0)
            diag.append(_dot(jnp.concatenate(ms, axis=1), rhs))
        y_parts.append(jnp.concatenate(diag, axis=1) + y_off)
        yield
    y = jnp.concatenate(y_parts, axis=1) + xs * dexp_ref[...]
    y = y * _silu(env["z"])
    outs = []
    for g in range(SSD_GROUPS):
        yg = y[:, g * GROUP_W:(g + 1) * GROUP_W]
        outs.append(yg * _rms_scale(yg))
    y_ref[rows, :] = (jnp.concatenate(outs, axis=1) * nw_ref[...]).astype(BF16)
    yield


def _gla_stages(env, r, rows, gw_ref, gb_ref, nw_ref, tri_ref, state, y_ref):
    c = GLA_CHUNK
    dk, dv = GLA_HEAD_K, GLA_HEAD_V
    row = lax.broadcasted_iota(jnp.int32, (c, c), 0)
    col = lax.broadcasted_iota(jnp.int32, (c, c), 1)
    causal = row >= col

    qf = env["q"][r, :] * (dk ** -0.5)
    kf = env["k"][r, :]
    logit = _dot(env["small"][r, :].astype(BF16), gw_ref[...]) + gb_ref[...]
    lg = -_softplus(-logit) * (_LOG2E / GLA_GATE_NORM)
    bcum = _dot(tri_ref[...], jnp.concatenate(_split3(lg), axis=0))
    blast = bcum[c - 1:c, :]
    qt = (qf * jnp.exp2(bcum)).astype(BF16)
    kt = (kf * jnp.exp2(-bcum)).astype(BF16)
    kd = (kf * jnp.exp2(blast - bcum)).astype(BF16)
    eblast = jnp.exp2(blast)
    yield
    for h in range(GLA_HEADS):
        ks = slice(h * dk, (h + 1) * dk)
        vs = slice(h * dv, (h + 1) * dv)
        attn = jnp.where(causal, _dot_nt(qt[:, ks], kt[:, ks]), 0.0)
        vh = env["v"][r, vs]
        o = _dot(attn.astype(BF16), vh) + _dot_nt(qt[:, ks], state[h].astype(BF16))
        state[h] = eblast[:, ks] * state[h] + _dot_tn(vh, kd[:, ks])
        o = o * _rms_scale(o) * nw_ref[...]
        o = o * _silu(env["go"][r, vs])
        y_ref[rows, vs] = o.astype(BF16)
        if h % 2 == 1:
            yield


def _mixer_kernel(x_ref, mod_ref, npre_ref, wbig_ref, wsmall_ref,
                  cw_ref, cb_ref, dtb_ref, alog_ref, dexp_ref, snw_ref, triq_ref, exp_ref,
                  gw_ref, gb_ref, gnw_ref, tric_ref,
                  ys_ref, yg_ref, xpad, sstate, gstate):
    @pl.when(pl.program_id(1) == 0)
    def _():
        xpad[0:HALO, :] = jnp.zeros((HALO, SSD_CONV_DIM), F32)
        sstate[...] = jnp.zeros(sstate.shape, F32)
        gstate[...] = jnp.zeros(gstate.shape, F32)

    for r0 in range(0, x_ref.shape[0], SSD_CHUNK):
        rows = slice(r0, r0 + SSD_CHUNK)
        x = x_ref[rows, :]
        xn = (x * _rms_scale(x)) * npre_ref[...]
        hb = (xn * (1.0 + mod_ref[1:2, :]) + mod_ref[0:1, :]).astype(BF16)

        env = {"small": _dot(hb, wsmall_ref[...]), "xbc": _project(hb, wbig_ref, _XBC0, SSD_CONV_DIM)}
        ssd = _ssd_stages(env, rows, cw_ref, cb_ref, dtb_ref, alog_ref, dexp_ref, snw_ref, triq_ref, exp_ref,
                          xpad, sstate, ys_ref)
        gla = [_gla_stages(env, slice(ci * GLA_CHUNK, (ci + 1) * GLA_CHUNK),
                           slice(r0 + ci * GLA_CHUNK, r0 + (ci + 1) * GLA_CHUNK), gw_ref, gb_ref, gnw_ref,
                           tric_ref, gstate, yg_ref) for ci in range(SSD_CHUNK // GLA_CHUNK)]
        env["q"] = _project(hb, wbig_ref, _Q0, GLA_K_WIDTH)
        env["k"] = _project(hb, wbig_ref, _K0, GLA_K_WIDTH)
        next(ssd)
        env["v"] = _project(hb, wbig_ref, _V0, GLA_V_WIDTH).astype(BF16)
        next(gla[0])
        next(ssd)
        env["go"] = _project(hb, wbig_ref, _GO0, GLA_V_WIDTH)
        next(gla[0])
        next(ssd)
        env["z"] = _project(hb, wbig_ref, _Z0, SSD_WIDTH)
        next(gla[0])
        next(gla[1])
        next(ssd)
        next(gla[1])
        next(ssd)
        next(gla[1])


def _mixer_call(x, mod, npre, wbig, wsmall, ssd_consts, gla_consts):
    b, s, d = x.shape
    t = _MIX_TILE
    row = lambda w: pl.BlockSpec((None, t, w), lambda i, j: (i, j, 0))
    consts = (npre, wbig, wsmall) + tuple(ssd_consts) + tuple(gla_consts)
    return pl.pallas_call(
        _mixer_kernel,
        grid=(b, s // t),
        in_specs=[row(d), pl.BlockSpec((None, ADA_CHUNKS, d), lambda i, j: (i, 0, 0))]
        + [_resident(a.shape) for a in consts],
        out_specs=[row(SSD_WIDTH), row(GLA_V_WIDTH)],
        out_shape=[jax.ShapeDtypeStruct((b, s, SSD_WIDTH), BF16),
                   jax.ShapeDtypeStruct((b, s, GLA_V_WIDTH), BF16)],
        scratch_shapes=[
            pltpu.VMEM((HALO + SSD_CHUNK, SSD_CONV_DIM), F32),
            pltpu.VMEM((SSD_GROUPS, SSD_STATE, GROUP_W), F32),
            pltpu.VMEM((GLA_HEADS, GLA_HEAD_V, GLA_HEAD_K), F32),
        ],
        compiler_params=_params("arbitrary", "arbitrary"),
        name="token_mixer",
    )(x, mod, *consts)


_UP_CHUNK = 384
_UP_DOT = 768
_FFN_TM = 256
_CAST_ROWS = 128


def _cast_weight(w_hbm, dst, stage, sem):
    rows = stage.shape[1]
    n = w_hbm.shape[0] // rows

    def copy(c):
        return pltpu.make_async_copy(w_hbm.at[pl.ds(c * rows, rows), :], stage.at[c % 2], sem.at[c % 2])

    copy(0).start()
    for c in range(n):
        if c + 1 < n:
            copy(c + 1).start()
        copy(c).wait()
        dst[c * rows:(c + 1) * rows, :] = stage[c % 2].astype(BF16)


def _ffn_kernel(ys_ref, yg_ref, x_ref, mod_ref, npost1_ref, npre_ref, wout_hbm, wup_hbm,
                cw_ref, cb_ref, npost2_ref, wdown_hbm, o_ref, upad, gate,
                wout_ref, wup_ref, wdown_ref, st_out, st_up, st_down, sem):
    tm = x_ref.shape[0]

    @pl.when((pl.program_id(0) == 0) & (pl.program_id(1) == 0))
    def _():
        _cast_weight(wout_hbm, wout_ref, st_out, sem.at[0])
        _cast_weight(wup_hbm, wup_ref, st_up, sem.at[1])
        _cast_weight(wdown_hbm, wdown_ref, st_down, sem.at[2])

    @pl.when(pl.program_id(1) == 0)
    def _():
        upad[0:HALO, :] = jnp.zeros((HALO, upad.shape[1]), F32)

    y = _dot(ys_ref[...], wout_ref[0:SSD_WIDTH, :]) + _dot(yg_ref[...], wout_ref[SSD_WIDTH:, :])
    yn = (y * _rms_scale(y)) * npost1_ref[...]
    x1 = x_ref[...] + mod_ref[2:3, :] * yn
    h = ((x1 * _rms_scale(x1)) * npre_ref[...]) * (1.0 + mod_ref[4:5, :]) + mod_ref[3:4, :]
    hb = h.astype(BF16)

    done = set()

    def up(col):
        p = col // _UP_DOT
        if p not in done:
            done.add(p)
            cols = slice(p * _UP_DOT, (p + 1) * _UP_DOT)
            upad[HALO:HALO + tm, cols] = _dot(hb, wup_ref[:, cols])

    def conv(c0):
        cols = slice(c0, c0 + _UP_CHUNK)
        up(c0)
        up(c0 + _UP_CHUNK - 1)
        acc = cb_ref[:, cols]
        for k in range(FFN_CONV):
            off = HALO - (FFN_CONV - 1) + k
            acc = acc + cw_ref[k:k + 1, cols] * upad[off:off + tm, cols]
        upad[0:HALO, cols] = upad[tm:tm + HALO, cols]
        return acc

    for c in range(0, FFN_HIDDEN, _UP_CHUNK):
        gate[:, c:c + _UP_CHUNK] = (_silu(conv(c)) * conv(FFN_HIDDEN + c)).astype(BF16)
    f = _dot(gate[...], wdown_ref[...])
    fn = (f * _rms_scale(f)) * npost2_ref[...]
    o_ref[...] = x1 + mod_ref[5:6, :] * fn


def _ffn_call(ys, yg, x, mod, npost1, npre, wout, wup, cw, cb, npost2, wdown):
    b, s, d = x.shape
    tm = _FFN_TM
    row = lambda w: pl.BlockSpec((None, tm, w), lambda i, j: (i, j, 0))
    consts = (npost1, npre, wout, wup, cw, cb, npost2, wdown)
    in_hbm = pl.BlockSpec(memory_space=pl.ANY)
    return pl.pallas_call(
        _ffn_kernel,
        grid=(b, s // tm),
        in_specs=[row(SSD_WIDTH), row(GLA_V_WIDTH), row(d),
                  pl.BlockSpec((None, ADA_CHUNKS, d), lambda i, j: (i, 0, 0))]
        + [in_hbm if any(a is w for w in (wout, wup, wdown)) else _resident(a.shape) for a in consts],
        out_specs=row(d),
        out_shape=jax.ShapeDtypeStruct((b, s, d), F32),
        scratch_shapes=[pltpu.VMEM((HALO + tm, 2 * FFN_HIDDEN), F32),
                        pltpu.VMEM((tm, FFN_HIDDEN), BF16),
                        pltpu.VMEM(wout.shape, BF16), pltpu.VMEM(wup.shape, BF16), pltpu.VMEM(wdown.shape, BF16),
                        pltpu.VMEM((2, _CAST_ROWS, wout.shape[1]), F32),
                        pltpu.VMEM((2, _CAST_ROWS, wup.shape[1]), F32),
                        pltpu.VMEM((2, _CAST_ROWS, wdown.shape[1]), F32),
                        pltpu.SemaphoreType.DMA((3, 2))],
        compiler_params=_params("arbitrary", "arbitrary"),
        name="channel_mixer",
    )(ys, yg, x, mod, *consts)


def _lane_pad(v, width):
    return jnp.pad(v, ((0, 0), (0, width - v.shape[1])))


def _constants():
    q, c = SSD_CHUNK, GLA_CHUNK
    tri_q = np.tril(np.ones((q, q), np.float32))
    tri_c = np.tril(np.ones((c, c), np.float32))
    expand = np.zeros((SMALL_W, SSD_WIDTH), np.float32)
    for h in range(SSD_HEADS):
        expand[h, h * SSD_HEAD_DIM:(h + 1) * SSD_HEAD_DIM] = 1.0
    return (jnp.asarray(np.tile(tri_q, (1, 3)), BF16),
            jnp.asarray(np.tile(tri_c, (1, 3)), BF16),
            jnp.asarray(np.tile(expand, (3, 1)), BF16))


def kernel(x, c, w_ada, b_ada, norm_mix_pre, norm_mix_post, norm_ffn_pre, norm_ffn_post, w_in, ssd_conv_w, ssd_conv_b, ssd_dt_bias, ssd_a_log, ssd_d, ssd_norm, gla_gate_w, gla_gate_b, gla_norm, w_out, ffn_up, ffn_conv_w, ffn_conv_b, ffn_down):
    bsz, seqlen, d = x.shape
    depth = w_ada.shape[0]
    tri_q3, tri_c3, expand3 = _constants()
    c_pad = jnp.pad(c, ((0, SUBLANES - bsz % SUBLANES), (0, 0))) if bsz % SUBLANES else c

    sizes = [SSD_WIDTH, SSD_CONV_DIM, SSD_HEADS, GLA_K_WIDTH, GLA_K_WIDTH, GLA_V_WIDTH,
             GLA_GATE_RANK, GLA_V_WIDTH]
    o = np.concatenate([[0], np.cumsum(sizes)])

    for i in range(depth):
        mod = _ada_call(c_pad, w_ada[i], b_ada[i][None, :])[:bsz].reshape(bsz, ADA_CHUNKS, d)

        wbig, wsmall = _reorder_call(jnp.swapaxes(w_in[i], 0, 1), o)
        ssd_consts = (ssd_conv_w[i], ssd_conv_b[i][None, :],
                      _lane_pad(ssd_dt_bias[i][None, :], SMALL_W), _lane_pad(ssd_a_log[i][None, :], SMALL_W),
                      jnp.repeat(ssd_d[i], SSD_HEAD_DIM)[None, :], ssd_norm[i][None, :], tri_q3, expand3)
        gw = jnp.zeros((SMALL_W, GLA_K_WIDTH), F32).at[SSD_HEADS:SSD_HEADS + GLA_GATE_RANK].set(
            gla_gate_w[i]).astype(BF16)
        gla_consts = (gw, gla_gate_b[i][None, :], gla_norm[i][None, :], tri_c3)
        y_ssd, y_gla = _mixer_call(x, mod, norm_mix_pre[i][None, :], wbig, wsmall, ssd_consts, gla_consts)

        x = _ffn_call(y_ssd, y_gla, x, mod, norm_mix_post[i][None, :], norm_ffn_pre[i][None, :],
                      w_out[i], ffn_up[i], ffn_conv_w[i], ffn_conv_b[i][None, :],
                      norm_ffn_post[i][None, :], ffn_down[i])
    return x
```

```python
import functools

import numpy as np
import jax
import jax.numpy as jnp
from jax import lax
from jax.experimental import pallas as pl
from jax.experimental.pallas import tpu as pltpu

F32 = jnp.float32
BF16 = jnp.bfloat16

D_MODEL = 1024
MIX_WIDTH = 2 * D_MODEL
SSD_WIDTH = MIX_WIDTH // 2
GLA_V_WIDTH = MIX_WIDTH - SSD_WIDTH
SSD_HEAD_DIM = 64
SSD_HEADS = SSD_WIDTH // SSD_HEAD_DIM
SSD_GROUPS = 2
SSD_STATE = 128
SSD_CONV = 4
SSD_CHUNK = 128
SSD_CONV_DIM = SSD_WIDTH + 2 * SSD_GROUPS * SSD_STATE
GLA_HEADS = 4
GLA_K_WIDTH = GLA_V_WIDTH // 2
GLA_HEAD_K = GLA_K_WIDTH // GLA_HEADS
GLA_HEAD_V = GLA_V_WIDTH // GLA_HEADS
GLA_GATE_RANK = 16
GLA_GATE_NORM = 16.0
GLA_CHUNK = 64
FFN_HIDDEN = int(round(8 * D_MODEL / 3 / 128)) * 128
FFN_CONV = 3
ADA_CHUNKS = 6
NORM_EPS = 1e-6
_LOG2E = float(np.log2(np.e))

LANES = 128
SUBLANES = 8
VMEM_LIMIT_BYTES = 58 * 1024 * 1024

SMALL_W = LANES
GROUP_W = SSD_WIDTH // SSD_GROUPS
HEADS_PER_GROUP = SSD_HEADS // SSD_GROUPS
HALO = SUBLANES


def _silu(x):
    return x * jax.nn.sigmoid(x)


def _softplus(x):
    return jnp.maximum(x, 0.0) + jnp.log(1.0 + jnp.exp(-jnp.abs(x)))


def _rms_scale(x):
    return lax.rsqrt(jnp.mean(x * x, axis=-1, keepdims=True) + NORM_EPS)


def _split3(x):
    hi = x.astype(BF16)
    r1 = x - hi.astype(F32)
    mid = r1.astype(BF16)
    lo = (r1 - mid.astype(F32)).astype(BF16)
    return hi, mid, lo


def _dot(a, b):
    return jnp.dot(a, b, preferred_element_type=F32)


def _dot_nt(a, b):
    return lax.dot_general(a, b, (((1,), (1,)), ((), ())), preferred_element_type=F32)


def _dot_tn(a, b):
    return lax.dot_general(a, b, (((0,), (0,)), ((), ())), preferred_element_type=F32)


def _resident(shape):
    nd = len(shape)
    return pl.BlockSpec(shape, lambda *_: (0,) * nd, pipeline_mode=pl.Buffered(1))


def _params(*sem):
    return pltpu.CompilerParams(dimension_semantics=sem, vmem_limit_bytes=VMEM_LIMIT_BYTES)


def _ada_kernel(c_ref, w_ref, b_ref, o_ref):
    ca = _silu(c_ref[...]).astype(BF16)
    o_ref[...] = _dot(ca, w_ref[...].astype(BF16)) + b_ref[...]


def _ada_call(c_pad, w, b):
    rows, d = c_pad.shape
    n = w.shape[1]
    tn = 1024
    return pl.pallas_call(
        _ada_kernel,
        grid=(n // tn,),
        in_specs=[
            pl.BlockSpec((rows, d), lambda j: (0, 0)),
            pl.BlockSpec((d, tn), lambda j: (0, j)),
            pl.BlockSpec((1, tn), lambda j: (0, j)),
        ],
        out_specs=pl.BlockSpec((rows, tn), lambda j: (0, j)),
        out_shape=jax.ShapeDtypeStruct((rows, n), F32),
        compiler_params=_params("arbitrary"),
        name="ada_mod",
    )(c_pad, w, b)


def _reorder_kernel(wt_ref, big_ref, small_ref, *, offs):
    dst = pl.program_id(0) * _PROJ_CHUNK
    skip_dt = offs[3] - offs[2]
    skip_glr = offs[7] - offs[6]
    src = dst + jnp.where(dst >= offs[2], skip_dt, 0) + jnp.where(dst >= offs[6] - skip_dt, skip_glr, 0)
    rows = wt_ref[pl.ds(pl.multiple_of(src, 2 * SUBLANES), _PROJ_CHUNK), :]
    big_ref[...] = rows.T.astype(BF16)

    @pl.when(pl.program_id(0) == 0)
    def _():
        d = wt_ref.shape[1]
        pad = jnp.zeros((SMALL_W - skip_dt - skip_glr, d), F32)
        small = jnp.concatenate([wt_ref[offs[2]:offs[3], :], wt_ref[offs[6]:offs[7], :], pad], axis=0)
        small_ref[...] = small.T.astype(BF16)


def _reorder_call(wt, offs):
    n_in, d = wt.shape
    return pl.pallas_call(
        functools.partial(_reorder_kernel, offs=tuple(int(v) for v in offs)),
        grid=(_PROJ_W // _PROJ_CHUNK,),
        in_specs=[_resident((n_in, d))],
        out_specs=[pl.BlockSpec((d, _PROJ_CHUNK), lambda j: (0, j)), pl.BlockSpec((d, SMALL_W), lambda j: (0, 0))],
        out_shape=[jax.ShapeDtypeStruct((d, _PROJ_W), BF16), jax.ShapeDtypeStruct((d, SMALL_W), BF16)],
        compiler_params=_params("arbitrary"),
        name="reorder_w_in",
    )(wt)


_Z0 = 0
_XBC0 = _Z0 + SSD_WIDTH
_Q0 = _XBC0 + SSD_CONV_DIM
_K0 = _Q0 + GLA_K_WIDTH
_V0 = _K0 + GLA_K_WIDTH
_GO0 = _V0 + GLA_V_WIDTH
_PROJ_W = _GO0 + GLA_V_WIDTH
_PROJ_CHUNK = 512
_MIX_TILE = SSD_CHUNK


def _project(hb, w_ref, c0, width):
    parts = [_dot(hb, w_ref[:, c:c + min(_PROJ_CHUNK, c0 + width - c)])
             for c in range(c0, c0 + width, _PROJ_CHUNK)]
    return parts[0] if len(parts) == 1 else jnp.concatenate(parts, axis=1)


def _ssd_stages(env, rows, cw_ref, cb_ref, dtb_ref, alog_ref, dexp_ref, nw_ref, tri_ref, exp_ref,
                xpad, state, y_ref):
    q = SSD_CHUNK
    n = SSD_STATE

    xpad[HALO:HALO + q, :] = env["xbc"]
    ext = xpad[...]
    acc = cw_ref[0:1, :] * ext
    for k in range(1, SSD_CONV):
        acc = pltpu.roll(acc, 1, 0) + cw_ref[k:k + 1, :] * ext
    acc = acc[HALO:HALO + q, :] + cb_ref[...]
    xpad[0:HALO, :] = xpad[q:q + HALO, :]
    xc = _silu(acc)
    xs = xc[:, :SSD_WIDTH]
    bm = xc[:, SSD_WIDTH:SSD_WIDTH + SSD_GROUPS * n].astype(BF16)
    cm = xc[:, SSD_WIDTH + SSD_GROUPS * n:].astype(BF16)
    yield

    lane = lax.broadcasted_iota(jnp.int32, (q, SMALL_W), 1)
    head_lane = lane < SSD_HEADS
    dt = _softplus(env["small"] + dtb_ref[...])
    a = -jnp.exp(alog_ref[...]) * _LOG2E
    dt = jnp.where(head_lane, dt, 0.0)
    da = jnp.where(head_lane, dt * a, 0.0)
    cs = _dot(tri_ref[...], jnp.concatenate(_split3(da), axis=0))
    cs_t = cs.T
    cs_e = _dot(jnp.concatenate(_split3(cs), axis=1), exp_ref[...])
    dt_e = _dot(jnp.concatenate(_split3(dt), axis=1), exp_ref[...])
    cs_last = cs_e[q - 1:q, :]

    xdt = xs * dt_e
    xdt_b = xdt.astype(BF16)
    xdec_b = (xdt * jnp.exp2(cs_last - cs_e)).astype(BF16)
    decay_in = jnp.exp2(cs_e)
    decay_chunk = jnp.exp2(cs_last)
    yield

    row = lax.broadcasted_iota(jnp.int32, (q, q), 0)
    col = lax.broadcasted_iota(jnp.int32, (q, q), 1)
    causal = row >= col
    first_half = col < SSD_HEAD_DIM

    y_parts = []
    for g in range(SSD_GROUPS):
        cmg = cm[:, g * n:(g + 1) * n]
        bmg = bm[:, g * n:(g + 1) * n]
        gs = slice(g * GROUP_W, (g + 1) * GROUP_W)
        scores = _dot_nt(cmg, bmg)
        y_off = _dot(cmg, state[g].astype(BF16)) * decay_in[:, gs]
        state[g] = decay_chunk[:, gs] * state[g] + _dot_tn(bmg, xdec_b[:, gs])
        diag = []
        for pair in range(HEADS_PER_GROUP // 2):
            h0 = g * HEADS_PER_GROUP + 2 * pair
            ms = []
            for h in (h0, h0 + 1):
                seg = cs[:, h:h + 1] - cs_t[h:h + 1, :]
                decay = jnp.exp2(jnp.where(causal, seg, -jnp.inf))
                ms.append((scores * decay).astype(BF16))
            slab = xdt_b[:, h0 * SSD_HEAD_DIM:(h0 + 2) * SSD_HEAD_DIM]
            zero = jnp.zeros_like(slab)
            rhs = jnp.concatenate([jnp.where(first_half, slab, zero),
                                   jnp.where(first_half, zero, slab)], axis=0)
            diag.append(_dot(jnp.concatenate(ms, axis=1), rhs))
        y_parts.append(jnp.concatenate(diag, axis=1) + y_off)
        yield
    y = jnp.concatenate(y_parts, axis=1) + xs * dexp_ref[...]
    y = y * _silu(env["z"])
    outs = []
    for g in range(SSD_GROUPS):
        yg = y[:, g * GROUP_W:(g + 1) * GROUP_W]
        outs.append(yg * _rms_scale(yg))
    y_ref[rows, :] = (jnp.concatenate(outs, axis=1) * nw_ref[...]).astype(BF16)
    yield


def _gla_stages(env, r, rows, gw_ref, gb_ref, nw_ref, tri_ref, state, y_ref):
    c = GLA_CHUNK
    dk, dv = GLA_HEAD_K, GLA_HEAD_V
    row = lax.broadcasted_iota(jnp.int32, (c, c), 0)
    col = lax.broadcasted_iota(jnp.int32, (c, c), 1)
    causal = row >= col

    qf = env["q"][r, :] * (dk ** -0.5)
    kf = env["k"][r, :]
    logit = _dot(env["small"][r, :].astype(BF16), gw_ref[...]) + gb_ref[...]
    lg = -_softplus(-logit) * (_LOG2E / GLA_GATE_NORM)
    bcum = _dot(tri_ref[...], jnp.concatenate(_split3(lg), axis=0))
    blast = bcum[c - 1:c, :]
    qt = (qf * jnp.exp2(bcum)).astype(BF16)
    kt = (kf * jnp.exp2(-bcum)).astype(BF16)
    kd = (kf * jnp.exp2(blast - bcum)).astype(BF16)
    eblast = jnp.exp2(blast)
    yield
    for h in range(GLA_HEADS):
        ks = slice(h * dk, (h + 1) * dk)
        vs = slice(h * dv, (h + 1) * dv)
        attn = jnp.where(causal, _dot_nt(qt[:, ks], kt[:, ks]), 0.0)
        vh = env["v"][r, vs]
        o = _dot(attn.astype(BF16), vh) + _dot_nt(qt[:, ks], state[h].astype(BF16))
        state[h] = eblast[:, ks] * state[h] + _dot_tn(vh, kd[:, ks])
        o = o * _rms_scale(o) * nw_ref[...]
        o = o * _silu(env["go"][r, vs])
        y_ref[rows, vs] = o.astype(BF16)
        if h % 2 == 1:
            yield


def _mixer_kernel(x_ref, mod_ref, npre_ref, wbig_ref, wsmall_ref,
                  cw_ref, cb_ref, dtb_ref, alog_ref, dexp_ref, snw_ref, triq_ref, exp_ref,
                  gw_ref, gb_ref, gnw_ref, tric_ref,
                  ys_ref, yg_ref, xpad, sstate, gstate):
    @pl.when(pl.program_id(1) == 0)
    def _():
        xpad[0:HALO, :] = jnp.zeros((HALO, SSD_CONV_DIM), F32)
        sstate[...] = jnp.zeros(sstate.shape, F32)
        gstate[...] = jnp.zeros(gstate.shape, F32)

    for r0 in range(0, x_ref.shape[0], SSD_CHUNK):
        rows = slice(r0, r0 + SSD_CHUNK)
        x = x_ref[rows, :]
        xn = (x * _rms_scale(x)) * npre_ref[...]
        hb = (xn * (1.0 + mod_ref[1:2, :]) + mod_ref[0:1, :]).astype(BF16)

        env = {"small": _dot(hb, wsmall_ref[...]), "xbc": _project(hb, wbig_ref, _XBC0, SSD_CONV_DIM)}
        ssd = _ssd_stages(env, rows, cw_ref, cb_ref, dtb_ref, alog_ref, dexp_ref, snw_ref, triq_ref, exp_ref,
                          xpad, sstate, ys_ref)
        gla = [_gla_stages(env, slice(ci * GLA_CHUNK, (ci + 1) * GLA_CHUNK),
                           slice(r0 + ci * GLA_CHUNK, r0 + (ci + 1) * GLA_CHUNK), gw_ref, gb_ref, gnw_ref,
                           tric_ref, gstate, yg_ref) for ci in range(SSD_CHUNK // GLA_CHUNK)]
        env["q"] = _project(hb, wbig_ref, _Q0, GLA_K_WIDTH)
        env["k"] = _project(hb, wbig_ref, _K0, GLA_K_WIDTH)
        next(ssd)
        env["v"] = _project(hb, wbig_ref, _V0, GLA_V_WIDTH).astype(BF16)
        next(gla[0])
        next(ssd)
        env["go"] = _project(hb, wbig_ref, _GO0, GLA_V_WIDTH)
        next(gla[0])
        next(ssd)
        env["z"] = _project(hb, wbig_ref, _Z0, SSD_WIDTH)
        next(gla[0])
        next(gla[1])
        next(ssd)
        next(gla[1])
        next(ssd)
        next(gla[1])


def _mixer_call(x, mod, npre, wbig, wsmall, ssd_consts, gla_consts):
    b, s, d = x.shape
    t = _MIX_TILE
    row = lambda w: pl.BlockSpec((None, t, w), lambda i, j: (i, j, 0))
    consts = (npre, wbig, wsmall) + tuple(ssd_consts) + tuple(gla_consts)
    return pl.pallas_call(
        _mixer_kernel,
        grid=(b, s // t),
        in_specs=[row(d), pl.BlockSpec((None, ADA_CHUNKS, d), lambda i, j: (i, 0, 0))]
        + [_resident(a.shape) for a in consts],
        out_specs=[row(SSD_WIDTH), row(GLA_V_WIDTH)],
        out_shape=[jax.ShapeDtypeStruct((b, s, SSD_WIDTH), BF16),
                   jax.ShapeDtypeStruct((b, s, GLA_V_WIDTH), BF16)],
        scratch_shapes=[
            pltpu.VMEM((HALO + SSD_CHUNK, SSD_CONV_DIM), F32),
            pltpu.VMEM((SSD_GROUPS, SSD_STATE, GROUP_W), F32),
            pltpu.VMEM((GLA_HEADS, GLA_HEAD_V, GLA_HEAD_K), F32),
        ],
        compiler_params=_params("arbitrary", "arbitrary"),
        name="token_mixer",
    )(x, mod, *consts)


_UP_CHUNK = 384
_UP_DOT = 768
_FFN_TM = 256
_CAST_ROWS = 128


def _cast_weight(w_hbm, dst, stage, sem):
    rows = stage.shape[1]
    n = w_hbm.shape[0] // rows

    def copy(c):
        return pltpu.make_async_copy(w_hbm.at[pl.ds(c * rows, rows), :], stage.at[c % 2], sem.at[c % 2])

    copy(0).start()
    for c in range(n):
        if c + 1 < n:
            copy(c + 1).start()
        copy(c).wait()
        dst[c * rows:(c + 1) * rows, :] = stage[c % 2].astype(BF16)


def _ffn_kernel(ys_ref, yg_ref, x_ref, mod_ref, npost1_ref, npre_ref, wout_hbm, wup_hbm,
                cw_ref, cb_ref, npost2_ref, wdown_hbm, o_ref, upad, gate,
                wout_ref, wup_ref, wdown_ref, st_out, st_up, st_down, sem):
    tm = x_ref.shape[0]

    @pl.when((pl.program_id(0) == 0) & (pl.program_id(1) == 0))
    def _():
        _cast_weight(wout_hbm, wout_ref, st_out, sem.at[0])
        _cast_weight(wup_hbm, wup_ref, st_up, sem.at[1])
        _cast_weight(wdown_hbm, wdown_ref, st_down, sem.at[2])

    @pl.when(pl.program_id(1) == 0)
    def _():
        upad[0:HALO, :] = jnp.zeros((HALO, upad.shape[1]), F32)

    y = _dot(ys_ref[...], wout_ref[0:SSD_WIDTH, :]) + _dot(yg_ref[...], wout_ref[SSD_WIDTH:, :])
    yn = (y * _rms_scale(y)) * npost1_ref[...]
    x1 = x_ref[...] + mod_ref[2:3, :] * yn
    h = ((x1 * _rms_scale(x1)) * npre_ref[...]) * (1.0 + mod_ref[4:5, :]) + mod_ref[3:4, :]
    hb = h.astype(BF16)

    done = set()

    def up(col):
        p = col // _UP_DOT
        if p not in done:
            done.add(p)
            cols = slice(p * _UP_DOT, (p + 1) * _UP_DOT)
            upad[HALO:HALO + tm, cols] = _dot(hb, wup_ref[:, cols])

    def conv(c0):
        cols = slice(c0, c0 + _UP_CHUNK)
        up(c0)
        up(c0 + _UP_CHUNK - 1)
        ext = upad[:, cols]
        acc = cw_ref[0:1, cols] * ext
        for k in range(1, FFN_CONV):
            acc = pltpu.roll(acc, 1, 0) + cw_ref[k:k + 1, cols] * ext
        upad[0:HALO, cols] = upad[tm:tm + HALO, cols]
        return acc[HALO:HALO + tm, :] + cb_ref[:, cols]

    for c in range(0, FFN_HIDDEN, _UP_CHUNK):
        gate[:, c:c + _UP_CHUNK] = (_silu(conv(c)) * conv(FFN_HIDDEN + c)).astype(BF16)
    f = _dot(gate[...], wdown_ref[...])
    fn = (f * _rms_scale(f)) * npost2_ref[...]
    o_ref[...] = x1 + mod_ref[5:6, :] * fn


def _ffn_call(ys, yg, x, mod, npost1, npre, wout, wup, cw, cb, npost2, wdown):
    b, s, d = x.shape
    tm = _FFN_TM
    row = lambda w: pl.BlockSpec((None, tm, w), lambda i, j: (i, j, 0))
    consts = (npost1, npre, wout, wup, cw, cb, npost2, wdown)
    in_hbm = pl.BlockSpec(memory_space=pl.ANY)
    return pl.pallas_call(
        _ffn_kernel,
        grid=(b, s // tm),
        in_specs=[row(SSD_WIDTH), row(GLA_V_WIDTH), row(d),
                  pl.BlockSpec((None, ADA_CHUNKS, d), lambda i, j: (i, 0, 0))]
        + [in_hbm if any(a is w for w in (wout, wup, wdown)) else _resident(a.shape) for a in consts],
        out_specs=row(d),
        out_shape=jax.ShapeDtypeStruct((b, s, d), F32),
        scratch_shapes=[pltpu.VMEM((HALO + tm, 2 * FFN_HIDDEN), F32),
                        pltpu.VMEM((tm, FFN_HIDDEN), BF16),
                        pltpu.VMEM(wout.shape, BF16), pltpu.VMEM(wup.shape, BF16), pltpu.VMEM(wdown.shape, BF16),
                        pltpu.VMEM((2, _CAST_ROWS, wout.shape[1]), F32),
                        pltpu.VMEM((2, _CAST_ROWS, wup.shape[1]), F32),
                        pltpu.VMEM((2, _CAST_ROWS, wdown.shape[1]), F32),
                        pltpu.SemaphoreType.DMA((3, 2))],
        compiler_params=_params("arbitrary", "arbitrary"),
        name="channel_mixer",
    )(ys, yg, x, mod, *consts)


def _lane_pad(v, width):
    return jnp.pad(v, ((0, 0), (0, width - v.shape[1])))


def _constants():
    q, c = SSD_CHUNK, GLA_CHUNK
    tri_q = np.tril(np.ones((q, q), np.float32))
    tri_c = np.tril(np.ones((c, c), np.float32))
    expand = np.zeros((SMALL_W, SSD_WIDTH), np.float32)
    for h in range(SSD_HEADS):
        expand[h, h * SSD_HEAD_DIM:(h + 1) * SSD_HEAD_DIM] = 1.0
    return (jnp.asarray(np.tile(tri_q, (1, 3)), BF16),
            jnp.asarray(np.tile(tri_c, (1, 3)), BF16),
            jnp.asarray(np.tile(expand, (3, 1)), BF16))


def kernel(x, c, w_ada, b_ada, norm_mix_pre, norm_mix_post, norm_ffn_pre, norm_ffn_post, w_in, ssd_conv_w, ssd_conv_b, ssd_dt_bias, ssd_a_log, ssd_d, ssd_norm, gla_gate_w, gla_gate_b, gla_norm, w_out, ffn_up, ffn_conv_w, ffn_conv_b, ffn_down):
    bsz, seqlen, d = x.shape
    depth = w_ada.shape[0]
    tri_q3, tri_c3, expand3 = _constants()
    c_pad = jnp.pad(c, ((0, SUBLANES - bsz % SUBLANES), (0, 0))) if bsz % SUBLANES else c

    sizes = [SSD_WIDTH, SSD_CONV_DIM, SSD_HEADS, GLA_K_WIDTH, GLA_K_WIDTH, GLA_V_WIDTH,
             GLA_GATE_RANK, GLA_V_WIDTH]
    o = np.concatenate([[0], np.cumsum(sizes)])

    for i in range(depth):
        mod = _ada_call(c_pad, w_ada[i], b_ada[i][None, :])[:bsz].reshape(bsz, ADA_CHUNKS, d)

        wbig, wsmall = _reorder_call(jnp.swapaxes(w_in[i], 0, 1), o)
        ssd_consts = (ssd_conv_w[i], ssd_conv_b[i][None, :],
                      _lane_pad(ssd_dt_bias[i][None, :], SMALL_W), _lane_pad(ssd_a_log[i][None, :], SMALL_W),
                      jnp.repeat(ssd_d[i], SSD_HEAD_DIM)[None, :], ssd_norm[i][None, :], tri_q3, expand3)
        gw = jnp.zeros((SMALL_W, GLA_K_WIDTH), F32).at[SSD_HEADS:SSD_HEADS + GLA_GATE_RANK].set(
            gla_gate_w[i]).astype(BF16)
        gla_consts = (gw, gla_gate_b[i][None, :], gla_norm[i][None, :], tri_c3)
        y_ssd, y_gla = _mixer_call(x, mod, norm_mix_pre[i][None, :], wbig, wsmall, ssd_consts, gla_consts)

        x = _ffn_call(y_ssd, y_gla, x, mod, norm_mix_post[i][None, :], norm_ffn_pre[i][None, :],
                      w_out[i], ffn_up[i], ffn_conv_w[i], ffn_conv_b[i][None, :],
                      norm_ffn_post[i][None, :], ffn_down[i])
    return x
```

```python
import functools

import numpy as np
import jax
import jax.numpy as jnp
from jax import lax
from jax.experimental import pallas as pl
from jax.experimental.pallas import tpu as pltpu

F32 = jnp.float32
BF16 = jnp.bfloat16

D_MODEL = 1024
MIX_WIDTH = 2 * D_MODEL
SSD_WIDTH = MIX_WIDTH // 2
GLA_V_WIDTH = MIX_WIDTH - SSD_WIDTH
SSD_HEAD_DIM = 64
SSD_HEADS = SSD_WIDTH // SSD_HEAD_DIM
SSD_GROUPS = 2
SSD_STATE = 128
SSD_CONV = 4
SSD_CHUNK = 128
SSD_CONV_DIM = SSD_WIDTH + 2 * SSD_GROUPS * SSD_STATE
GLA_HEADS = 4
GLA_K_WIDTH = GLA_V_WIDTH // 2
GLA_HEAD_K = GLA_K_WIDTH // GLA_HEADS
GLA_HEAD_V = GLA_V_WIDTH // GLA_HEADS
GLA_GATE_RANK = 16
GLA_GATE_NORM = 16.0
GLA_CHUNK = 64
FFN_HIDDEN = int(round(8 * D_MODEL / 3 / 128)) * 128
FFN_CONV = 3
ADA_CHUNKS = 6
NORM_EPS = 1e-6
_LOG2E = float(np.log2(np.e))

LANES = 128
SUBLANES = 8
VMEM_LIMIT_BYTES = 58 * 1024 * 1024

SMALL_W = LANES
GROUP_W = SSD_WIDTH // SSD_GROUPS
HEADS_PER_GROUP = SSD_HEADS // SSD_GROUPS
HALO = SUBLANES


def _silu(x):
    return x * jax.nn.sigmoid(x)


def _softplus(x):
    return jnp.maximum(x, 0.0) + jnp.log(1.0 + jnp.exp(-jnp.abs(x)))


def _rms_scale(x):
    return lax.rsqrt(jnp.mean(x * x, axis=-1, keepdims=True) + NORM_EPS)


def _split3(x):
    hi = x.astype(BF16)
    r1 = x - hi.astype(F32)
    mid = r1.astype(BF16)
    lo = (r1 - mid.astype(F32)).astype(BF16)
    return hi, mid, lo


def _dot(a, b):
    return jnp.dot(a, b, preferred_element_type=F32)


def _dot_nt(a, b):
    return lax.dot_general(a, b, (((1,), (1,)), ((), ())), preferred_element_type=F32)


def _dot_tn(a, b):
    return lax.dot_general(a, b, (((0,), (0,)), ((), ())), preferred_element_type=F32)


def _resident(shape):
    nd = len(shape)
    return pl.BlockSpec(shape, lambda *_: (0,) * nd, pipeline_mode=pl.Buffered(1))


def _params(*sem):
    return pltpu.CompilerParams(dimension_semantics=sem, vmem_limit_bytes=VMEM_LIMIT_BYTES)


def _ada_kernel(c_ref, w_ref, b_ref, o_ref):
    ca = _silu(c_ref[...]).astype(BF16)
    o_ref[...] = _dot(ca, w_ref[...].astype(BF16)) + b_ref[...]


def _ada_call(c_pad, w, b):
    rows, d = c_pad.shape
    n = w.shape[1]
    tn = 1024
    return pl.pallas_call(
        _ada_kernel,
        grid=(n // tn,),
        in_specs=[
            pl.BlockSpec((rows, d), lambda j: (0, 0)),
            pl.BlockSpec((d, tn), lambda j: (0, j)),
            pl.BlockSpec((1, tn), lambda j: (0, j)),
        ],
        out_specs=pl.BlockSpec((rows, tn), lambda j: (0, j)),
        out_shape=jax.ShapeDtypeStruct((rows, n), F32),
        compiler_params=_params("arbitrary"),
        name="ada_mod",
    )(c_pad, w, b)


def _reorder_kernel(wt_ref, big_ref, small_ref, *, offs):
    dst = pl.program_id(0) * _PROJ_CHUNK
    skip_dt = offs[3] - offs[2]
    skip_glr = offs[7] - offs[6]
    src = dst + jnp.where(dst >= offs[2], skip_dt, 0) + jnp.where(dst >= offs[6] - skip_dt, skip_glr, 0)
    rows = wt_ref[pl.ds(pl.multiple_of(src, 2 * SUBLANES), _PROJ_CHUNK), :]
    big_ref[...] = rows.T.astype(BF16)

    @pl.when(pl.program_id(0) == 0)
    def _():
        d = wt_ref.shape[1]
        pad = jnp.zeros((SMALL_W - skip_dt - skip_glr, d), F32)
        small = jnp.concatenate([wt_ref[offs[2]:offs[3], :], wt_ref[offs[6]:offs[7], :], pad], axis=0)
        small_ref[...] = small.T.astype(BF16)


def _reorder_call(wt, offs):
    n_in, d = wt.shape
    return pl.pallas_call(
        functools.partial(_reorder_kernel, offs=tuple(int(v) for v in offs)),
        grid=(_PROJ_W // _PROJ_CHUNK,),
        in_specs=[_resident((n_in, d))],
        out_specs=[pl.BlockSpec((d, _PROJ_CHUNK), lambda j: (0, j)), pl.BlockSpec((d, SMALL_W), lambda j: (0, 0))],
        out_shape=[jax.ShapeDtypeStruct((d, _PROJ_W), BF16), jax.ShapeDtypeStruct((d, SMALL_W), BF16)],
        compiler_params=_params("arbitrary"),
        name="reorder_w_in",
    )(wt)


_Z0 = 0
_XBC0 = _Z0 + SSD_WIDTH
_Q0 = _XBC0 + SSD_CONV_DIM
_K0 = _Q0 + GLA_K_WIDTH
_V0 = _K0 + GLA_K_WIDTH
_GO0 = _V0 + GLA_V_WIDTH
_PROJ_W = _GO0 + GLA_V_WIDTH
_PROJ_CHUNK = 512


def _project(hb, w_ref, c0, width):
    parts = [_dot(hb, w_ref[:, c:c + min(_PROJ_CHUNK, c0 + width - c)])
             for c in range(c0, c0 + width, _PROJ_CHUNK)]
    return parts[0] if len(parts) == 1 else jnp.concatenate(parts, axis=1)


def _ssd_stages(env, rows, cw_ref, cb_ref, dtb_ref, alog_ref, dexp_ref, nw_ref, tri_ref, exp_ref,
                xpad, state, y_ref):
    q = SSD_CHUNK
    n = SSD_STATE

    xpad[HALO:HALO + q, :] = env["xbc"]
    ext = xpad[...]
    acc = cw_ref[0:1, :] * ext
    for k in range(1, SSD_CONV):
        acc = pltpu.roll(acc, 1, 0) + cw_ref[k:k + 1, :] * ext
    acc = acc[HALO:HALO + q, :] + cb_ref[...]
    xpad[0:HALO, :] = xpad[q:q + HALO, :]
    xc = _silu(acc)
    xs = xc[:, :SSD_WIDTH]
    bm = xc[:, SSD_WIDTH:SSD_WIDTH + SSD_GROUPS * n].astype(BF16)
    cm = xc[:, SSD_WIDTH + SSD_GROUPS * n:].astype(BF16)
    yield

    lane = lax.broadcasted_iota(jnp.int32, (q, SMALL_W), 1)
    head_lane = lane < SSD_HEADS
    dt = _softplus(env["small"] + dtb_ref[...])
    a = -jnp.exp(alog_ref[...]) * _LOG2E
    dt = jnp.where(head_lane, dt, 0.0)
    da = jnp.where(head_lane, dt * a, 0.0)
    cs = _dot(tri_ref[...], jnp.concatenate(_split3(da), axis=0))
    cs_t = cs.T
    cs_e = _dot(jnp.concatenate(_split3(cs), axis=1), exp_ref[...])
    dt_e = _dot(jnp.concatenate(_split3(dt), axis=1), exp_ref[...])
    cs_last = cs_e[q - 1:q, :]

    xdt = xs * dt_e
    xdt_b = xdt.astype(BF16)
    xdec_b = (xdt * jnp.exp2(cs_last - cs_e)).astype(BF16)
    decay_in = jnp.exp2(cs_e)
    decay_chunk = jnp.exp2(cs_last)
    yield

    row = lax.broadcasted_iota(jnp.int32, (q, q), 0)
    col = lax.broadcasted_iota(jnp.int32, (q, q), 1)
    causal = row >= col
    first_half = col < SSD_HEAD_DIM

    y_parts = []
    for g in range(SSD_GROUPS):
        cmg = cm[:, g * n:(g + 1) * n]
        bmg = bm[:, g * n:(g + 1) * n]
        gs = slice(g * GROUP_W, (g + 1) * GROUP_W)
        scores = _dot_nt(cmg, bmg)
        y_off = _dot(cmg, state[g].astype(BF16)) * decay_in[:, gs]
        state[g] = decay_chunk[:, gs] * state[g] + _dot_tn(bmg, xdec_b[:, gs])
        diag = []
        for pair in range(HEADS_PER_GROUP // 2):
            h0 = g * HEADS_PER_GROUP + 2 * pair
            ms = []
            for h in (h0, h0 + 1):
                seg = cs[:, h:h + 1] - cs_t[h:h + 1, :]
                decay = jnp.exp2(jnp.where(causal, seg, -jnp.inf))
                ms.append((scores * decay).astype(BF16))
            slab = xdt_b[:, h0 * SSD_HEAD_DIM:(h0 + 2) * SSD_HEAD_DIM]
            zero = jnp.zeros_like(slab)
            rhs = jnp.concatenate([jnp.where(first_half, slab, zero),
                                   jnp.where(first_half, zero, slab)], axis=0)
            diag.append(_dot(jnp.concatenate(ms, axis=1), rhs))
        y_parts.append(jnp.concatenate(diag, axis=1) + y_off)
        yield
    y = jnp.concatenate(y_parts, axis=1) + xs * dexp_ref[...]
    y = y * _silu(env["z"])
    outs = []
    for g in range(SSD_GROUPS):
        yg = y[:, g * GROUP_W:(g + 1) * GROUP_W]
        outs.append(yg * _rms_scale(yg))
    y_ref[rows, :] = (jnp.concatenate(outs, axis=1) * nw_ref[...]).astype(BF16)
    yield


def _gla_stages(env, r, rows, gw_ref, gb_ref, nw_ref, tri_ref, state, y_ref):
    c = GLA_CHUNK
    dk, dv = GLA_HEAD_K, GLA_HEAD_V
    row = lax.broadcasted_iota(jnp.int32, (c, c), 0)
    col = lax.broadcasted_iota(jnp.int32, (c, c), 1)
    causal = row >= col

    qf = env["q"][r, :] * (dk ** -0.5)
    kf = env["k"][r, :]
    logit = _dot(env["small"][r, :].astype(BF16), gw_ref[...]) + gb_ref[...]
    lg = -_softplus(-logit) * (_LOG2E / GLA_GATE_NORM)
    bcum = _dot(tri_ref[...], jnp.concatenate(_split3(lg), axis=0))
    blast = bcum[c - 1:c, :]
    qt = (qf * jnp.exp2(bcum)).astype(BF16)
    kt = (kf * jnp.exp2(-bcum)).astype(BF16)
    kd = (kf * jnp.exp2(blast - bcum)).astype(BF16)
    eblast = jnp.exp2(blast)
    yield
    for h in range(GLA_HEADS):
        ks = slice(h * dk, (h + 1) * dk)
        vs = slice(h * dv, (h + 1) * dv)
        attn = jnp.where(causal, _dot_nt(qt[:, ks], kt[:, ks]), 0.0)
        vh = env["v"][r, vs]
        o = _dot(attn.astype(BF16), vh) + _dot_nt(qt[:, ks], state[h].astype(BF16))
        state[h] = eblast[:, ks] * state[h] + _dot_tn(vh, kd[:, ks])
        o = o * _rms_scale(o) * nw_ref[...]
        o = o * _silu(env["go"][r, vs])
        y_ref[rows, vs] = o.astype(BF16)
        if h % 2 == 1:
            yield


def _prenorm(x_ref, mod_ref, npre_ref, hb_s):
    x = x_ref[...]
    xn = (x * _rms_scale(x)) * npre_ref[...]
    hb_s[...] = (xn * (1.0 + mod_ref[1:2, :]) + mod_ref[0:1, :]).astype(BF16)


def _mixer_kernel(x0_ref, mod0_ref, xn_ref, modn_ref, npre_ref, wbig_ref, wsmall_ref,
                  cw_ref, cb_ref, dtb_ref, alog_ref, dexp_ref, snw_ref, triq_ref, exp_ref,
                  gw_ref, gb_ref, gnw_ref, tric_ref,
                  ys_ref, yg_ref, xpad, sstate, gstate, hb_s, *, tiles_per_seq):
    i = pl.program_id(0)

    @pl.when(i == 0)
    def _():
        _prenorm(x0_ref, mod0_ref, npre_ref, hb_s)

    @pl.when(lax.rem(i, tiles_per_seq) == 0)
    def _():
        xpad[0:HALO, :] = jnp.zeros((HALO, SSD_CONV_DIM), F32)
        sstate[...] = jnp.zeros(sstate.shape, F32)
        gstate[...] = jnp.zeros(gstate.shape, F32)

    rows = slice(0, SSD_CHUNK)
    hb = hb_s[...]
    env = {"small": _dot(hb, wsmall_ref[...]), "xbc": _project(hb, wbig_ref, _XBC0, SSD_CONV_DIM)}
    ssd = _ssd_stages(env, rows, cw_ref, cb_ref, dtb_ref, alog_ref, dexp_ref, snw_ref, triq_ref, exp_ref,
                      xpad, sstate, ys_ref)
    gla = [_gla_stages(env, slice(ci * GLA_CHUNK, (ci + 1) * GLA_CHUNK),
                       slice(ci * GLA_CHUNK, (ci + 1) * GLA_CHUNK), gw_ref, gb_ref, gnw_ref,
                       tric_ref, gstate, yg_ref) for ci in range(SSD_CHUNK // GLA_CHUNK)]
    env["q"] = _project(hb, wbig_ref, _Q0, GLA_K_WIDTH)
    env["k"] = _project(hb, wbig_ref, _K0, GLA_K_WIDTH)
    next(ssd)
    env["v"] = _project(hb, wbig_ref, _V0, GLA_V_WIDTH).astype(BF16)
    next(gla[0])
    next(ssd)
    env["go"] = _project(hb, wbig_ref, _GO0, GLA_V_WIDTH)
    next(gla[0])
    next(ssd)
    env["z"] = _project(hb, wbig_ref, _Z0, SSD_WIDTH)
    _prenorm(xn_ref, modn_ref, npre_ref, hb_s)
    next(gla[0])
    next(gla[1])
    next(ssd)
    next(gla[1])
    next(ssd)
    next(gla[1])


def _mixer_call(x, mod, npre, wbig, wsmall, ssd_consts, gla_consts):
    b, s, d = x.shape
    t = SSD_CHUNK
    tiles_per_seq = s // t
    n = b * tiles_per_seq
    xf = x.reshape(b * s, d)
    nxt = lambda i: jnp.minimum(i + 1, n - 1)
    consts = (npre, wbig, wsmall) + tuple(ssd_consts) + tuple(gla_consts)
    ys, yg = pl.pallas_call(
        functools.partial(_mixer_kernel, tiles_per_seq=tiles_per_seq),
        grid=(n,),
        in_specs=[pl.BlockSpec((t, d), lambda i: (0, 0)),
                  pl.BlockSpec((None, ADA_CHUNKS, d), lambda i: (0, 0, 0)),
                  pl.BlockSpec((t, d), lambda i: (nxt(i), 0)),
                  pl.BlockSpec((None, ADA_CHUNKS, d), lambda i: (nxt(i) // tiles_per_seq, 0, 0))]
        + [_resident(a.shape) for a in consts],
        out_specs=[pl.BlockSpec((t, SSD_WIDTH), lambda i: (i, 0)),
                   pl.BlockSpec((t, GLA_V_WIDTH), lambda i: (i, 0))],
        out_shape=[jax.ShapeDtypeStruct((b * s, SSD_WIDTH), BF16),
                   jax.ShapeDtypeStruct((b * s, GLA_V_WIDTH), BF16)],
        scratch_shapes=[
            pltpu.VMEM((HALO + SSD_CHUNK, SSD_CONV_DIM), F32),
            pltpu.VMEM((SSD_GROUPS, SSD_STATE, GROUP_W), F32),
            pltpu.VMEM((GLA_HEADS, GLA_HEAD_V, GLA_HEAD_K), F32),
            pltpu.VMEM((t, d), BF16),
        ],
        compiler_params=_params("arbitrary"),
        name="token_mixer",
    )(xf, mod, xf, mod, *consts)
    return ys.reshape(b, s, SSD_WIDTH), yg.reshape(b, s, GLA_V_WIDTH)


_UP_CHUNK = 384
_UP_DOT = 768
_FFN_TM = 256
_CAST_ROWS = 128


def _cast_weight(w_hbm, dst, stage, sem):
    rows = stage.shape[1]
    n = w_hbm.shape[0] // rows

    def copy(c):
        return pltpu.make_async_copy(w_hbm.at[pl.ds(c * rows, rows), :], stage.at[c % 2], sem.at[c % 2])

    copy(0).start()
    for c in range(n):
        if c + 1 < n:
            copy(c + 1).start()
        copy(c).wait()
        dst[c * rows:(c + 1) * rows, :] = stage[c % 2].astype(BF16)


def _ffn_kernel(ys_ref, yg_ref, x_ref, mod_ref, npost1_ref, npre_ref, wout_hbm, wup_hbm,
                cw_ref, cb_ref, npost2_ref, wdown_hbm, o_ref, upad, gate,
                wout_ref, wup_ref, wdown_ref, st_out, st_up, st_down, sem):
    tm = x_ref.shape[0]

    @pl.when((pl.program_id(0) == 0) & (pl.program_id(1) == 0))
    def _():
        _cast_weight(wout_hbm, wout_ref, st_out, sem.at[0])
        _cast_weight(wup_hbm, wup_ref, st_up, sem.at[1])
        _cast_weight(wdown_hbm, wdown_ref, st_down, sem.at[2])

    @pl.when(pl.program_id(1) == 0)
    def _():
        upad[0:HALO, :] = jnp.zeros((HALO, upad.shape[1]), F32)

    y = _dot(ys_ref[...], wout_ref[0:SSD_WIDTH, :]) + _dot(yg_ref[...], wout_ref[SSD_WIDTH:, :])
    yn = (y * _rms_scale(y)) * npost1_ref[...]
    x1 = x_ref[...] + mod_ref[2:3, :] * yn
    h = ((x1 * _rms_scale(x1)) * npre_ref[...]) * (1.0 + mod_ref[4:5, :]) + mod_ref[3:4, :]
    hb = h.astype(BF16)

    done = set()

    def up(col):
        p = col // _UP_DOT
        if p not in done:
            done.add(p)
            cols = slice(p * _UP_DOT, (p + 1) * _UP_DOT)
            upad[HALO:HALO + tm, cols] = _dot(hb, wup_ref[:, cols])

    def conv(c0):
        cols = slice(c0, c0 + _UP_CHUNK)
        up(c0)
        up(c0 + _UP_CHUNK - 1)
        ext = upad[:, cols]
        acc = cw_ref[0:1, cols] * ext
        for k in range(1, FFN_CONV):
            acc = pltpu.roll(acc, 1, 0) + cw_ref[k:k + 1, cols] * ext
        upad[0:HALO, cols] = upad[tm:tm + HALO, cols]
        return acc[HALO:HALO + tm, :] + cb_ref[:, cols]

    for c in range(0, FFN_HIDDEN, _UP_CHUNK):
        gate[:, c:c + _UP_CHUNK] = (_silu(conv(c)) * conv(FFN_HIDDEN + c)).astype(BF16)
    f = _dot(gate[...], wdown_ref[...])
    fn = (f * _rms_scale(f)) * npost2_ref[...]
    o_ref[...] = x1 + mod_ref[5:6, :] * fn


def _ffn_call(ys, yg, x, mod, npost1, npre, wout, wup, cw, cb, npost2, wdown):
    b, s, d = x.shape
    tm = _FFN_TM
    row = lambda w: pl.BlockSpec((None, tm, w), lambda i, j: (i, j, 0))
    consts = (npost1, npre, wout, wup, cw, cb, npost2, wdown)
    in_hbm = pl.BlockSpec(memory_space=pl.ANY)
    return pl.pallas_call(
        _ffn_kernel,
        grid=(b, s // tm),
        in_specs=[row(SSD_WIDTH), row(GLA_V_WIDTH), row(d),
                  pl.BlockSpec((None, ADA_CHUNKS, d), lambda i, j: (i, 0, 0))]
        + [in_hbm if any(a is w for w in (wout, wup, wdown)) else _resident(a.shape) for a in consts],
        out_specs=row(d),
        out_shape=jax.ShapeDtypeStruct((b, s, d), F32),
        scratch_shapes=[pltpu.VMEM((HALO + tm, 2 * FFN_HIDDEN), F32),
                        pltpu.VMEM((tm, FFN_HIDDEN), BF16),
                        pltpu.VMEM(wout.shape, BF16), pltpu.VMEM(wup.shape, BF16), pltpu.VMEM(wdown.shape, BF16),
                        pltpu.VMEM((2, _CAST_ROWS, wout.shape[1]), F32),
                        pltpu.VMEM((2, _CAST_ROWS, wup.shape[1]), F32),
                        pltpu.VMEM((2, _CAST_ROWS, wdown.shape[1]), F32),
                        pltpu.SemaphoreType.DMA((3, 2))],
        compiler_params=_params("arbitrary", "arbitrary"),
        name="channel_mixer",
    )(ys, yg, x, mod, *consts)


def _lane_pad(v, width):
    return jnp.pad(v, ((0, 0), (0, width - v.shape[1])))


def _constants():
    q, c = SSD_CHUNK, GLA_CHUNK
    tri_q = np.tril(np.ones((q, q), np.float32))
    tri_c = np.tril(np.ones((c, c), np.float32))
    expand = np.zeros((SMALL_W, SSD_WIDTH), np.float32)
    for h in range(SSD_HEADS):
        expand[h, h * SSD_HEAD_DIM:(h + 1) * SSD_HEAD_DIM] = 1.0
    return (jnp.asarray(np.tile(tri_q, (1, 3)), BF16),
            jnp.asarray(np.tile(tri_c, (1, 3)), BF16),
            jnp.asarray(np.tile(expand, (3, 1)), BF16))


def kernel(x, c, w_ada, b_ada, norm_mix_pre, norm_mix_post, norm_ffn_pre, norm_ffn_post, w_in, ssd_conv_w, ssd_conv_b, ssd_dt_bias, ssd_a_log, ssd_d, ssd_norm, gla_gate_w, gla_gate_b, gla_norm, w_out, ffn_up, ffn_conv_w, ffn_conv_b, ffn_down):
    bsz, seqlen, d = x.shape
    depth = w_ada.shape[0]
    tri_q3, tri_c3, expand3 = _constants()
    c_pad = jnp.pad(c, ((0, SUBLANES - bsz % SUBLANES), (0, 0))) if bsz % SUBLANES else c

    sizes = [SSD_WIDTH, SSD_CONV_DIM, SSD_HEADS, GLA_K_WIDTH, GLA_K_WIDTH, GLA_V_WIDTH,
             GLA_GATE_RANK, GLA_V_WIDTH]
    o = np.concatenate([[0], np.cumsum(sizes)])

    for i in range(depth):
        mod = _ada_call(c_pad, w_ada[i], b_ada[i][None, :])[:bsz].reshape(bsz, ADA_CHUNKS, d)

        wbig, wsmall = _reorder_call(jnp.swapaxes(w_in[i], 0, 1), o)
        ssd_consts = (ssd_conv_w[i], ssd_conv_b[i][None, :],
                      _lane_pad(ssd_dt_bias[i][None, :], SMALL_W), _lane_pad(ssd_a_log[i][None, :], SMALL_W),
                      jnp.repeat(ssd_d[i], SSD_HEAD_DIM)[None, :], ssd_norm[i][None, :], tri_q3, expand3)
        gw = jnp.zeros((SMALL_W, GLA_K_WIDTH), F32).at[SSD_HEADS:SSD_HEADS + GLA_GATE_RANK].set(
            gla_gate_w[i]).astype(BF16)
        gla_consts = (gw, gla_gate_b[i][None, :], gla_norm[i][None, :], tri_c3)
        y_ssd, y_gla = _mixer_call(x, mod, norm_mix_pre[i][None, :], wbig, wsmall, ssd_consts, gla_consts)

        x = _ffn_call(y_ssd, y_gla, x, mod, norm_mix_post[i][None, :], norm_ffn_pre[i][None, :],
                      w_out[i], ffn_up[i], ffn_conv_w[i], ffn_conv_b[i][None, :],
                      norm_ffn_post[i][None, :], ffn_down[i])
    return x
```

```python
import functools

import numpy as np
import jax
import jax.numpy as jnp
from jax import lax
from jax.experimental import pallas as pl
from jax.experimental.pallas import tpu as pltpu

F32 = jnp.float32
BF16 = jnp.bfloat16

D_MODEL = 1024
MIX_WIDTH = 2 * D_MODEL
SSD_WIDTH = MIX_WIDTH // 2
GLA_V_WIDTH = MIX_WIDTH - SSD_WIDTH
SSD_HEAD_DIM = 64
SSD_HEADS = SSD_WIDTH // SSD_HEAD_DIM
SSD_GROUPS = 2
SSD_STATE = 128
SSD_CONV = 4
SSD_CHUNK = 128
SSD_CONV_DIM = SSD_WIDTH + 2 * SSD_GROUPS * SSD_STATE
GLA_HEADS = 4
GLA_K_WIDTH = GLA_V_WIDTH // 2
GLA_HEAD_K = GLA_K_WIDTH // GLA_HEADS
GLA_HEAD_V = GLA_V_WIDTH // GLA_HEADS
GLA_GATE_RANK = 16
GLA_GATE_NORM = 16.0
GLA_CHUNK = 64
FFN_HIDDEN = int(round(8 * D_MODEL / 3 / 128)) * 128
FFN_CONV = 3
ADA_CHUNKS = 6
NORM_EPS = 1e-6
_LOG2E = float(np.log2(np.e))

LANES = 128
SUBLANES = 8
VMEM_LIMIT_BYTES = 58 * 1024 * 1024

SMALL_W = LANES
GROUP_W = SSD_WIDTH // SSD_GROUPS
HEADS_PER_GROUP = SSD_HEADS // SSD_GROUPS
HALO = SUBLANES


def _silu(x):
    return x * jax.nn.sigmoid(x)


def _softplus(x):
    return jnp.maximum(x, 0.0) + jnp.log(1.0 + jnp.exp(-jnp.abs(x)))


def _rms_scale(x):
    return lax.rsqrt(jnp.mean(x * x, axis=-1, keepdims=True) + NORM_EPS)


def _split3(x):
    hi = x.astype(BF16)
    r1 = x - hi.astype(F32)
    mid = r1.astype(BF16)
    lo = (r1 - mid.astype(F32)).astype(BF16)
    return hi, mid, lo


def _dot(a, b):
    return jnp.dot(a, b, preferred_element_type=F32)


def _dot_nt(a, b):
    return lax.dot_general(a, b, (((1,), (1,)), ((), ())), preferred_element_type=F32)


def _dot_tn(a, b):
    return lax.dot_general(a, b, (((0,), (0,)), ((), ())), preferred_element_type=F32)


def _resident(shape):
    nd = len(shape)
    return pl.BlockSpec(shape, lambda *_: (0,) * nd, pipeline_mode=pl.Buffered(1))


def _params(*sem):
    return pltpu.CompilerParams(dimension_semantics=sem, vmem_limit_bytes=VMEM_LIMIT_BYTES)


def _ada_kernel(c_ref, w_ref, b_ref, o_ref):
    ca = _silu(c_ref[...]).astype(BF16)
    o_ref[...] = _dot(ca, w_ref[...].astype(BF16)) + b_ref[...]


def _ada_call(c_pad, w, b):
    rows, d = c_pad.shape
    n = w.shape[1]
    tn = 1024
    return pl.pallas_call(
        _ada_kernel,
        grid=(n // tn,),
        in_specs=[
            pl.BlockSpec((rows, d), lambda j: (0, 0)),
            pl.BlockSpec((d, tn), lambda j: (0, j)),
            pl.BlockSpec((1, tn), lambda j: (0, j)),
        ],
        out_specs=pl.BlockSpec((rows, tn), lambda j: (0, j)),
        out_shape=jax.ShapeDtypeStruct((rows, n), F32),
        compiler_params=_params("arbitrary"),
        name="ada_mod",
    )(c_pad, w, b)


def _reorder_kernel(wt_ref, big_ref, small_ref, *, offs):
    dst = pl.program_id(0) * _PROJ_CHUNK
    skip_dt = offs[3] - offs[2]
    skip_glr = offs[7] - offs[6]
    src = dst + jnp.where(dst >= offs[2], skip_dt, 0) + jnp.where(dst >= offs[6] - skip_dt, skip_glr, 0)
    rows = wt_ref[pl.ds(pl.multiple_of(src, 2 * SUBLANES), _PROJ_CHUNK), :]
    big_ref[...] = rows.T.astype(BF16)

    @pl.when(pl.program_id(0) == 0)
    def _():
        d = wt_ref.shape[1]
        pad = jnp.zeros((SMALL_W - skip_dt - skip_glr, d), F32)
        small = jnp.concatenate([wt_ref[offs[2]:offs[3], :], wt_ref[offs[6]:offs[7], :], pad], axis=0)
        small_ref[...] = small.T.astype(BF16)


def _reorder_call(wt, offs):
    n_in, d = wt.shape
    return pl.pallas_call(
        functools.partial(_reorder_kernel, offs=tuple(int(v) for v in offs)),
        grid=(_PROJ_W // _PROJ_CHUNK,),
        in_specs=[_resident((n_in, d))],
        out_specs=[pl.BlockSpec((d, _PROJ_CHUNK), lambda j: (0, j)), pl.BlockSpec((d, SMALL_W), lambda j: (0, 0))],
        out_shape=[jax.ShapeDtypeStruct((d, _PROJ_W), BF16), jax.ShapeDtypeStruct((d, SMALL_W), BF16)],
        compiler_params=_params("arbitrary"),
        name="reorder_w_in",
    )(wt)


_Z0 = 0
_XBC0 = _Z0 + SSD_WIDTH
_Q0 = _XBC0 + SSD_CONV_DIM
_K0 = _Q0 + GLA_K_WIDTH
_V0 = _K0 + GLA_K_WIDTH
_GO0 = _V0 + GLA_V_WIDTH
_PROJ_W = _GO0 + GLA_V_WIDTH
_PROJ_CHUNK = 512


def _project(hb, w_ref, c0, width):
    parts = [_dot(hb, w_ref[:, c:c + min(_PROJ_CHUNK, c0 + width - c)])
             for c in range(c0, c0 + width, _PROJ_CHUNK)]
    return parts[0] if len(parts) == 1 else jnp.concatenate(parts, axis=1)


def _ssd_stages(env, rows, cw_ref, cb_ref, dtb_ref, alog_ref, dexp_ref, nw_ref, tri_ref, exp_ref,
                xpad, state, y_ref):
    q = SSD_CHUNK
    n = SSD_STATE

    xpad[HALO:HALO + q, :] = env["xbc"]
    ext = xpad[...]
    acc = cw_ref[0:1, :] * ext
    for k in range(1, SSD_CONV):
        acc = pltpu.roll(acc, 1, 0) + cw_ref[k:k + 1, :] * ext
    acc = acc[HALO:HALO + q, :] + cb_ref[...]
    xpad[0:HALO, :] = xpad[q:q + HALO, :]
    xc = _silu(acc)
    xs = xc[:, :SSD_WIDTH]
    bm = xc[:, SSD_WIDTH:SSD_WIDTH + SSD_GROUPS * n].astype(BF16)
    cm = xc[:, SSD_WIDTH + SSD_GROUPS * n:].astype(BF16)
    yield

    lane = lax.broadcasted_iota(jnp.int32, (q, SMALL_W), 1)
    head_lane = lane < SSD_HEADS
    dt = _softplus(env["small"] + dtb_ref[...])
    a = -jnp.exp(alog_ref[...]) * _LOG2E
    dt = jnp.where(head_lane, dt, 0.0)
    da = jnp.where(head_lane, dt * a, 0.0)
    cs = _dot(tri_ref[...], jnp.concatenate(_split3(da), axis=0))
    cs_t = cs.T
    cs_e = _dot(jnp.concatenate(_split3(cs), axis=1), exp_ref[...])
    dt_e = _dot(jnp.concatenate(_split3(dt), axis=1), exp_ref[...])
    cs_last = cs_e[q - 1:q, :]

    xdt = xs * dt_e
    xdt_b = xdt.astype(BF16)
    xdec_b = (xdt * jnp.exp2(cs_last - cs_e)).astype(BF16)
    decay_in = jnp.exp2(cs_e)
    decay_chunk = jnp.exp2(cs_last)
    yield

    row = lax.broadcasted_iota(jnp.int32, (q, q), 0)
    col = lax.broadcasted_iota(jnp.int32, (q, q), 1)
    causal = row >= col
    first_half = col < SSD_HEAD_DIM

    y_parts = []
    for g in range(SSD_GROUPS):
        cmg = cm[:, g * n:(g + 1) * n]
        bmg = bm[:, g * n:(g + 1) * n]
        gs = slice(g * GROUP_W, (g + 1) * GROUP_W)
        scores = _dot_nt(cmg, bmg)
        y_off = _dot(cmg, state[g].astype(BF16)) * decay_in[:, gs]
        state[g] = decay_chunk[:, gs] * state[g] + _dot_tn(bmg, xdec_b[:, gs])
        diag = []
        for pair in range(HEADS_PER_GROUP // 2):
            h0 = g * HEADS_PER_GROUP + 2 * pair
            ms = []
            for h in (h0, h0 + 1):
                seg = cs[:, h:h + 1] - cs_t[h:h + 1, :]
                decay = jnp.exp2(jnp.where(causal, seg, -jnp.inf))
                ms.append((scores * decay).astype(BF16))
            slab = xdt_b[:, h0 * SSD_HEAD_DIM:(h0 + 2) * SSD_HEAD_DIM]
            zero = jnp.zeros_like(slab)
            rhs = jnp.concatenate([jnp.where(first_half, slab, zero),
                                   jnp.where(first_half, zero, slab)], axis=0)
            diag.append(_dot(jnp.concatenate(ms, axis=1), rhs))
        y_parts.append(jnp.concatenate(diag, axis=1) + y_off)
        yield
    y = jnp.concatenate(y_parts, axis=1) + xs * dexp_ref[...]
    y = y * _silu(env["z"])
    outs = []
    for g in range(SSD_GROUPS):
        yg = y[:, g * GROUP_W:(g + 1) * GROUP_W]
        outs.append(yg * _rms_scale(yg))
    y_ref[rows, :] = (jnp.concatenate(outs, axis=1) * nw_ref[...]).astype(BF16)
    yield


def _gla_stages(env, r, rows, gw_ref, gb_ref, nw_ref, tri_ref, state, y_ref):
    c = GLA_CHUNK
    dk, dv = GLA_HEAD_K, GLA_HEAD_V
    row = lax.broadcasted_iota(jnp.int32, (c, c), 0)
    col = lax.broadcasted_iota(jnp.int32, (c, c), 1)
    causal = row >= col

    qf = env["q"][r, :] * (dk ** -0.5)
    kf = env["k"][r, :]
    logit = _dot(env["small"][r, :].astype(BF16), gw_ref[...]) + gb_ref[...]
    lg = -_softplus(-logit) * (_LOG2E / GLA_GATE_NORM)
    bcum = _dot(tri_ref[...], jnp.concatenate(_split3(lg), axis=0))
    blast = bcum[c - 1:c, :]
    qt = (qf * jnp.exp2(bcum)).astype(BF16)
    kt = (kf * jnp.exp2(-bcum)).astype(BF16)
    kd = (kf * jnp.exp2(blast - bcum)).astype(BF16)
    eblast = jnp.exp2(blast)
    yield
    for h in range(GLA_HEADS):
        ks = slice(h * dk, (h + 1) * dk)
        vs = slice(h * dv, (h + 1) * dv)
        attn = jnp.where(causal, _dot_nt(qt[:, ks], kt[:, ks]), 0.0)
        vh = env["v"][r, vs]
        o = _dot(attn.astype(BF16), vh) + _dot_nt(qt[:, ks], state[h].astype(BF16))
        state[h] = eblast[:, ks] * state[h] + _dot_tn(vh, kd[:, ks])
        o = o * _rms_scale(o) * nw_ref[...]
        o = o * _silu(env["go"][r, vs])
        y_ref[rows, vs] = o.astype(BF16)
        if h % 2 == 1:
            yield


def _prenorm(x_ref, mod_ref, npre_ref, hb_s):
    x = x_ref[...]
    xn = (x * _rms_scale(x)) * npre_ref[...]
    hb_s[...] = (xn * (1.0 + mod_ref[1:2, :]) + mod_ref[0:1, :]).astype(BF16)


def _mixer_kernel(x0_ref, mod0_ref, xn_ref, modn_ref, npre_ref, wbig_ref, wsmall_ref,
                  cw_ref, cb_ref, dtb_ref, alog_ref, dexp_ref, snw_ref, triq_ref, exp_ref,
                  gw_ref, gb_ref, gnw_ref, tric_ref,
                  ys_ref, yg_ref, xpad, sstate, gstate, hb_s, *, tiles_per_seq):
    i = pl.program_id(0)

    @pl.when(i == 0)
    def _():
        _prenorm(x0_ref, mod0_ref, npre_ref, hb_s)

    @pl.when(lax.rem(i, tiles_per_seq) == 0)
    def _():
        xpad[0:HALO, :] = jnp.zeros((HALO, SSD_CONV_DIM), F32)
        sstate[...] = jnp.zeros(sstate.shape, F32)
        gstate[...] = jnp.zeros(gstate.shape, F32)

    rows = slice(0, SSD_CHUNK)
    hb = hb_s[...]
    env = {"small": _dot(hb, wsmall_ref[...]), "xbc": _project(hb, wbig_ref, _XBC0, SSD_CONV_DIM)}
    ssd = _ssd_stages(env, rows, cw_ref, cb_ref, dtb_ref, alog_ref, dexp_ref, snw_ref, triq_ref, exp_ref,
                      xpad, sstate, ys_ref)
    gla = [_gla_stages(env, slice(ci * GLA_CHUNK, (ci + 1) * GLA_CHUNK),
                       slice(ci * GLA_CHUNK, (ci + 1) * GLA_CHUNK), gw_ref, gb_ref, gnw_ref,
                       tric_ref, gstate, yg_ref) for ci in range(SSD_CHUNK // GLA_CHUNK)]
    env["q"] = _project(hb, wbig_ref, _Q0, GLA_K_WIDTH)
    env["k"] = _project(hb, wbig_ref, _K0, GLA_K_WIDTH)
    next(ssd)
    env["v"] = _project(hb, wbig_ref, _V0, GLA_V_WIDTH).astype(BF16)
    next(gla[0])
    next(ssd)
    env["go"] = _project(hb, wbig_ref, _GO0, GLA_V_WIDTH)
    next(gla[0])
    next(ssd)
    env["z"] = _project(hb, wbig_ref, _Z0, SSD_WIDTH)
    _prenorm(xn_ref, modn_ref, npre_ref, hb_s)
    next(gla[0])
    next(gla[1])
    next(ssd)
    next(gla[1])
    next(ssd)
    next(gla[1])


def _mixer_call(x, mod, npre, wbig, wsmall, ssd_consts, gla_consts):
    b, s, d = x.shape
    t = SSD_CHUNK
    tiles_per_seq = s // t
    n = b * tiles_per_seq
    xf = x.reshape(b * s, d)
    nxt = lambda i: jnp.minimum(i + 1, n - 1)
    consts = (npre, wbig, wsmall) + tuple(ssd_consts) + tuple(gla_consts)
    ys, yg = pl.pallas_call(
        functools.partial(_mixer_kernel, tiles_per_seq=tiles_per_seq),
        grid=(n,),
        in_specs=[pl.BlockSpec((t, d), lambda i: (0, 0)),
                  pl.BlockSpec((None, ADA_CHUNKS, d), lambda i: (0, 0, 0)),
                  pl.BlockSpec((t, d), lambda i: (nxt(i), 0)),
                  pl.BlockSpec((None, ADA_CHUNKS, d), lambda i: (nxt(i) // tiles_per_seq, 0, 0))]
        + [_resident(a.shape) for a in consts],
        out_specs=[pl.BlockSpec((t, SSD_WIDTH), lambda i: (i, 0)),
                   pl.BlockSpec((t, GLA_V_WIDTH), lambda i: (i, 0))],
        out_shape=[jax.ShapeDtypeStruct((b * s, SSD_WIDTH), BF16),
                   jax.ShapeDtypeStruct((b * s, GLA_V_WIDTH), BF16)],
        scratch_shapes=[
            pltpu.VMEM((HALO + SSD_CHUNK, SSD_CONV_DIM), F32),
            pltpu.VMEM((SSD_GROUPS, SSD_STATE, GROUP_W), F32),
            pltpu.VMEM((GLA_HEADS, GLA_HEAD_V, GLA_HEAD_K), F32),
            pltpu.VMEM((t, d), BF16),
        ],
        compiler_params=_params("arbitrary"),
        name="token_mixer",
    )(xf, mod, xf, mod, *consts)
    return ys.reshape(b, s, SSD_WIDTH), yg.reshape(b, s, GLA_V_WIDTH)


_UP_CHUNK = 384
_UP_DOT = 768
_FFN_TM = 256
_CAST_ROWS = 128


def _cast_weight(w_hbm, dst, stage, sem):
    rows = stage.shape[1]
    n = w_hbm.shape[0] // rows

    def copy(c):
        return pltpu.make_async_copy(w_hbm.at[pl.ds(c * rows, rows), :], stage.at[c % 2], sem.at[c % 2])

    copy(0).start()
    for c in range(n):
        if c + 1 < n:
            copy(c + 1).start()
        copy(c).wait()
        dst[c * rows:(c + 1) * rows, :] = stage[c % 2].astype(BF16)


def _stage_ffn_input(ys_ref, yg_ref, x_ref, mod_ref, npost1_ref, npre_ref, wout_ref, x1_s, hb_s):
    y = _dot(ys_ref[...], wout_ref[0:SSD_WIDTH, :]) + _dot(yg_ref[...], wout_ref[SSD_WIDTH:, :])
    yn = (y * _rms_scale(y)) * npost1_ref[...]
    x1 = x_ref[...] + mod_ref[2:3, :] * yn
    x1_s[...] = x1
    h = ((x1 * _rms_scale(x1)) * npre_ref[...]) * (1.0 + mod_ref[4:5, :]) + mod_ref[3:4, :]
    hb_s[...] = h.astype(BF16)


def _ffn_kernel(ys0_ref, yg0_ref, x0_ref, mod0_ref, ysn_ref, ygn_ref, xn_ref, modn_ref, mod_ref,
                npost1_ref, npre_ref, wout_hbm, wup_hbm, cw_ref, cb_ref, npost2_ref, wdown_hbm,
                o_ref, upad, gate, x1_s, hb_s,
                wout_ref, wup_ref, wdown_ref, st_out, st_up, st_down, sem, *, tiles_per_seq):
    tm = o_ref.shape[0]
    i = pl.program_id(0)

    @pl.when(i == 0)
    def _():
        _cast_weight(wout_hbm, wout_ref, st_out, sem.at[0])
        _cast_weight(wup_hbm, wup_ref, st_up, sem.at[1])
        _cast_weight(wdown_hbm, wdown_ref, st_down, sem.at[2])
        _stage_ffn_input(ys0_ref, yg0_ref, x0_ref, mod0_ref, npost1_ref, npre_ref, wout_ref, x1_s, hb_s)

    @pl.when(lax.rem(i, tiles_per_seq) == 0)
    def _():
        upad[0:HALO, :] = jnp.zeros((HALO, upad.shape[1]), F32)

    x1 = x1_s[...]
    hb = hb_s[...]

    done = set()

    def up(col):
        p = col // _UP_DOT
        if p not in done:
            done.add(p)
            cols = slice(p * _UP_DOT, (p + 1) * _UP_DOT)
            upad[HALO:HALO + tm, cols] = _dot(hb, wup_ref[:, cols])

    def conv(c0):
        cols = slice(c0, c0 + _UP_CHUNK)
        up(c0)
        up(c0 + _UP_CHUNK - 1)
        ext = upad[:, cols]
        acc = cw_ref[0:1, cols] * ext
        for k in range(1, FFN_CONV):
            acc = pltpu.roll(acc, 1, 0) + cw_ref[k:k + 1, cols] * ext
        upad[0:HALO, cols] = upad[tm:tm + HALO, cols]
        return acc[HALO:HALO + tm, :] + cb_ref[:, cols]

    for c in range(0, FFN_HIDDEN, _UP_CHUNK):
        gate[:, c:c + _UP_CHUNK] = (_silu(conv(c)) * conv(FFN_HIDDEN + c)).astype(BF16)
    _stage_ffn_input(ysn_ref, ygn_ref, xn_ref, modn_ref, npost1_ref, npre_ref, wout_ref, x1_s, hb_s)
    f = _dot(gate[...], wdown_ref[...])
    fn = (f * _rms_scale(f)) * npost2_ref[...]
    o_ref[...] = x1 + mod_ref[5:6, :] * fn


def _ffn_call(ys, yg, x, mod, npost1, npre, wout, wup, cw, cb, npost2, wdown):
    b, s, d = x.shape
    tm = _FFN_TM
    tiles_per_seq = s // tm
    n = b * tiles_per_seq
    flat = lambda a: a.reshape(b * s, a.shape[-1])
    nxt = lambda i: jnp.minimum(i + 1, n - 1)
    first = lambda w: pl.BlockSpec((tm, w), lambda i: (0, 0))
    ahead = lambda w: pl.BlockSpec((tm, w), lambda i: (nxt(i), 0))
    mod_of = lambda tile: pl.BlockSpec((None, ADA_CHUNKS, d), lambda i: (tile(i) // tiles_per_seq, 0, 0))
    consts = (npost1, npre, wout, wup, cw, cb, npost2, wdown)
    in_hbm = pl.BlockSpec(memory_space=pl.ANY)
    out = pl.pallas_call(
        functools.partial(_ffn_kernel, tiles_per_seq=tiles_per_seq),
        grid=(n,),
        in_specs=[first(SSD_WIDTH), first(GLA_V_WIDTH), first(d), mod_of(lambda i: 0 * i),
                  ahead(SSD_WIDTH), ahead(GLA_V_WIDTH), ahead(d), mod_of(nxt), mod_of(lambda i: i)]
        + [in_hbm if any(a is w for w in (wout, wup, wdown)) else _resident(a.shape) for a in consts],
        out_specs=pl.BlockSpec((tm, d), lambda i: (i, 0)),
        out_shape=jax.ShapeDtypeStruct((b * s, d), F32),
        scratch_shapes=[pltpu.VMEM((HALO + tm, 2 * FFN_HIDDEN), F32),
                        pltpu.VMEM((tm, FFN_HIDDEN), BF16),
                        pltpu.VMEM((tm, d), F32), pltpu.VMEM((tm, d), BF16),
                        pltpu.VMEM(wout.shape, BF16), pltpu.VMEM(wup.shape, BF16), pltpu.VMEM(wdown.shape, BF16),
                        pltpu.VMEM((2, _CAST_ROWS, wout.shape[1]), F32),
                        pltpu.VMEM((2, _CAST_ROWS, wup.shape[1]), F32),
                        pltpu.VMEM((2, _CAST_ROWS, wdown.shape[1]), F32),
                        pltpu.SemaphoreType.DMA((3, 2))],
        compiler_params=_params("arbitrary"),
        name="channel_mixer",
    )(flat(ys), flat(yg), flat(x), mod, flat(ys), flat(yg), flat(x), mod, mod, *consts)
    return out.reshape(b, s, d)


def _lane_pad(v, width):
    return jnp.pad(v, ((0, 0), (0, width - v.shape[1])))


def _constants():
    q, c = SSD_CHUNK, GLA_CHUNK
    tri_q = np.tril(np.ones((q, q), np.float32))
    tri_c = np.tril(np.ones((c, c), np.float32))
    expand = np.zeros((SMALL_W, SSD_WIDTH), np.float32)
    for h in range(SSD_HEADS):
        expand[h, h * SSD_HEAD_DIM:(h + 1) * SSD_HEAD_DIM] = 1.0
    return (jnp.asarray(np.tile(tri_q, (1, 3)), BF16),
            jnp.asarray(np.tile(tri_c, (1, 3)), BF16),
            jnp.asarray(np.tile(expand, (3, 1)), BF16))


def kernel(x, c, w_ada, b_ada, norm_mix_pre, norm_mix_post, norm_ffn_pre, norm_ffn_post, w_in, ssd_conv_w, ssd_conv_b, ssd_dt_bias, ssd_a_log, ssd_d, ssd_norm, gla_gate_w, gla_gate_b, gla_norm, w_out, ffn_up, ffn_conv_w, ffn_conv_b, ffn_down):
    bsz, seqlen, d = x.shape
    depth = w_ada.shape[0]
    tri_q3, tri_c3, expand3 = _constants()
    c_pad = jnp.pad(c, ((0, SUBLANES - bsz % SUBLANES), (0, 0))) if bsz % SUBLANES else c

    sizes = [SSD_WIDTH, SSD_CONV_DIM, SSD_HEADS, GLA_K_WIDTH, GLA_K_WIDTH, GLA_V_WIDTH,
             GLA_GATE_RANK, GLA_V_WIDTH]
    o = np.concatenate([[0], np.cumsum(sizes)])

    for i in range(depth):
        mod = _ada_call(c_pad, w_ada[i], b_ada[i][None, :])[:bsz].reshape(bsz, ADA_CHUNKS, d)

        wbig, wsmall = _reorder_call(jnp.swapaxes(w_in[i], 0, 1), o)
        ssd_consts = (ssd_conv_w[i], ssd_conv_b[i][None, :],
                      _lane_pad(ssd_dt_bias[i][None, :], SMALL_W), _lane_pad(ssd_a_log[i][None, :], SMALL_W),
                      jnp.repeat(ssd_d[i], SSD_HEAD_DIM)[None, :], ssd_norm[i][None, :], tri_q3, expand3)
        gw = jnp.zeros((SMALL_W, GLA_K_WIDTH), F32).at[SSD_HEADS:SSD_HEADS + GLA_GATE_RANK].set(
            gla_gate_w[i]).astype(BF16)
        gla_consts = (gw, gla_gate_b[i][None, :], gla_norm[i][None, :], tri_c3)
        y_ssd, y_gla = _mixer_call(x, mod, norm_mix_pre[i][None, :], wbig, wsmall, ssd_consts, gla_consts)

        x = _ffn_call(y_ssd, y_gla, x, mod, norm_mix_post[i][None, :], norm_ffn_pre[i][None, :],
                      w_out[i], ffn_up[i], ffn_conv_w[i], ffn_conv_b[i][None, :],
                      norm_ffn_post[i][None, :], ffn_down[i])
    return x
```

```python
import functools

import numpy as np
import jax
import jax.numpy as jnp
from jax import lax
from jax.experimental import pallas as pl
from jax.experimental.pallas import tpu as pltpu

F32 = jnp.float32
BF16 = jnp.bfloat16

D_MODEL = 1024
MIX_WIDTH = 2 * D_MODEL
SSD_WIDTH = MIX_WIDTH // 2
GLA_V_WIDTH = MIX_WIDTH - SSD_WIDTH
SSD_HEAD_DIM = 64
SSD_HEADS = SSD_WIDTH // SSD_HEAD_DIM
SSD_GROUPS = 2
SSD_STATE = 128
SSD_CONV = 4
SSD_CHUNK = 128
SSD_CONV_DIM = SSD_WIDTH + 2 * SSD_GROUPS * SSD_STATE
GLA_HEADS = 4
GLA_K_WIDTH = GLA_V_WIDTH // 2
GLA_HEAD_K = GLA_K_WIDTH // GLA_HEADS
GLA_HEAD_V = GLA_V_WIDTH // GLA_HEADS
GLA_GATE_RANK = 16
GLA_GATE_NORM = 16.0
GLA_CHUNK = 64
FFN_HIDDEN = int(round(8 * D_MODEL / 3 / 128)) * 128
FFN_CONV = 3
ADA_CHUNKS = 6
NORM_EPS = 1e-6
_LOG2E = float(np.log2(np.e))

LANES = 128
SUBLANES = 8
VMEM_LIMIT_BYTES = 58 * 1024 * 1024

SMALL_W = LANES
GROUP_W = SSD_WIDTH // SSD_GROUPS
HEADS_PER_GROUP = SSD_HEADS // SSD_GROUPS
HALO = SUBLANES


def _silu(x):
    return x * jax.nn.sigmoid(x)


def _softplus(x):
    return jnp.maximum(x, 0.0) + jnp.log(1.0 + jnp.exp(-jnp.abs(x)))


def _rms_scale(x):
    return lax.rsqrt(jnp.mean(x * x, axis=-1, keepdims=True) + NORM_EPS)


def _split3(x):
    hi = x.astype(BF16)
    r1 = x - hi.astype(F32)
    mid = r1.astype(BF16)
    lo = (r1 - mid.astype(F32)).astype(BF16)
    return hi, mid, lo


def _dot(a, b):
    return jnp.dot(a, b, preferred_element_type=F32)


def _dot_nt(a, b):
    return lax.dot_general(a, b, (((1,), (1,)), ((), ())), preferred_element_type=F32)


def _dot_tn(a, b):
    return lax.dot_general(a, b, (((0,), (0,)), ((), ())), preferred_element_type=F32)


def _resident(shape):
    nd = len(shape)
    return pl.BlockSpec(shape, lambda *_: (0,) * nd, pipeline_mode=pl.Buffered(1))


def _params(*sem):
    return pltpu.CompilerParams(dimension_semantics=sem, vmem_limit_bytes=VMEM_LIMIT_BYTES)


def _ada_kernel(c_ref, w_ref, b_ref, o_ref):
    ca = _silu(c_ref[...]).astype(BF16)
    o_ref[...] = _dot(ca, w_ref[...].astype(BF16)) + b_ref[...]


def _ada_call(c_pad, w, b):
    rows, d = c_pad.shape
    n = w.shape[1]
    tn = 1024
    return pl.pallas_call(
        _ada_kernel,
        grid=(n // tn,),
        in_specs=[
            pl.BlockSpec((rows, d), lambda j: (0, 0)),
            pl.BlockSpec((d, tn), lambda j: (0, j)),
            pl.BlockSpec((1, tn), lambda j: (0, j)),
        ],
        out_specs=pl.BlockSpec((rows, tn), lambda j: (0, j)),
        out_shape=jax.ShapeDtypeStruct((rows, n), F32),
        compiler_params=_params("arbitrary"),
        name="ada_mod",
    )(c_pad, w, b)


def _reorder_kernel(wt_ref, big_ref, small_ref, *, offs):
    dst = pl.program_id(0) * _PROJ_CHUNK
    skip_dt = offs[3] - offs[2]
    skip_glr = offs[7] - offs[6]
    src = dst + jnp.where(dst >= offs[2], skip_dt, 0) + jnp.where(dst >= offs[6] - skip_dt, skip_glr, 0)
    rows = wt_ref[pl.ds(pl.multiple_of(src, 2 * SUBLANES), _PROJ_CHUNK), :]
    big_ref[...] = rows.T.astype(BF16)

    @pl.when(pl.program_id(0) == 0)
    def _():
        d = wt_ref.shape[1]
        pad = jnp.zeros((SMALL_W - skip_dt - skip_glr, d), F32)
        small = jnp.concatenate([wt_ref[offs[2]:offs[3], :], wt_ref[offs[6]:offs[7], :], pad], axis=0)
        small_ref[...] = small.T.astype(BF16)


def _reorder_call(wt, offs):
    n_in, d = wt.shape
    return pl.pallas_call(
        functools.partial(_reorder_kernel, offs=tuple(int(v) for v in offs)),
        grid=(_PROJ_W // _PROJ_CHUNK,),
        in_specs=[_resident((n_in, d))],
        out_specs=[pl.BlockSpec((d, _PROJ_CHUNK), lambda j: (0, j)), pl.BlockSpec((d, SMALL_W), lambda j: (0, 0))],
        out_shape=[jax.ShapeDtypeStruct((d, _PROJ_W), BF16), jax.ShapeDtypeStruct((d, SMALL_W), BF16)],
        compiler_params=_params("arbitrary"),
        name="reorder_w_in",
    )(wt)


_Z0 = 0
_XBC0 = _Z0 + SSD_WIDTH
_Q0 = _XBC0 + SSD_CONV_DIM
_K0 = _Q0 + GLA_K_WIDTH
_V0 = _K0 + GLA_K_WIDTH
_GO0 = _V0 + GLA_V_WIDTH
_PROJ_W = _GO0 + GLA_V_WIDTH
_PROJ_CHUNK = 512


def _project(hb, w_ref, c0, width):
    parts = [_dot(hb, w_ref[:, c:c + min(_PROJ_CHUNK, c0 + width - c)])
             for c in range(c0, c0 + width, _PROJ_CHUNK)]
    return parts[0] if len(parts) == 1 else jnp.concatenate(parts, axis=1)


def _ssd_stages(env, cw_ref, cb_ref, dtb_ref, alog_ref, dexp_ref, tri_ref, exp_ref, xpad, state, y_s):
    q = SSD_CHUNK
    n = SSD_STATE

    xpad[HALO:HALO + q, :] = env["xbc"]
    ext = xpad[...]
    acc = cw_ref[0:1, :] * ext
    for k in range(1, SSD_CONV):
        acc = pltpu.roll(acc, 1, 0) + cw_ref[k:k + 1, :] * ext
    acc = acc[HALO:HALO + q, :] + cb_ref[...]
    xpad[0:HALO, :] = xpad[q:q + HALO, :]
    xc = _silu(acc)
    xs = xc[:, :SSD_WIDTH]
    bm = xc[:, SSD_WIDTH:SSD_WIDTH + SSD_GROUPS * n].astype(BF16)
    cm = xc[:, SSD_WIDTH + SSD_GROUPS * n:].astype(BF16)
    yield

    lane = lax.broadcasted_iota(jnp.int32, (q, SMALL_W), 1)
    head_lane = lane < SSD_HEADS
    dt = _softplus(env["small"] + dtb_ref[...])
    a = -jnp.exp(alog_ref[...]) * _LOG2E
    dt = jnp.where(head_lane, dt, 0.0)
    da = jnp.where(head_lane, dt * a, 0.0)
    cs = _dot(tri_ref[...], jnp.concatenate(_split3(da), axis=0))
    cs_t = cs.T
    cs_e = _dot(jnp.concatenate(_split3(cs), axis=1), exp_ref[...])
    dt_e = _dot(jnp.concatenate(_split3(dt), axis=1), exp_ref[...])
    cs_last = cs_e[q - 1:q, :]

    xdt = xs * dt_e
    xdt_b = xdt.astype(BF16)
    xdec_b = (xdt * jnp.exp2(cs_last - cs_e)).astype(BF16)
    decay_in = jnp.exp2(cs_e)
    decay_chunk = jnp.exp2(cs_last)
    yield

    row = lax.broadcasted_iota(jnp.int32, (q, q), 0)
    col = lax.broadcasted_iota(jnp.int32, (q, q), 1)
    causal = row >= col
    first_half = col < SSD_HEAD_DIM

    for g in range(SSD_GROUPS):
        cmg = cm[:, g * n:(g + 1) * n]
        bmg = bm[:, g * n:(g + 1) * n]
        gs = slice(g * GROUP_W, (g + 1) * GROUP_W)
        scores = _dot_nt(cmg, bmg)
        y_off = _dot(cmg, state[g].astype(BF16)) * decay_in[:, gs]
        state[g] = decay_chunk[:, gs] * state[g] + _dot_tn(bmg, xdec_b[:, gs])
        diag = []
        for pair in range(HEADS_PER_GROUP // 2):
            h0 = g * HEADS_PER_GROUP + 2 * pair
            ms = []
            for h in (h0, h0 + 1):
                seg = cs[:, h:h + 1] - cs_t[h:h + 1, :]
                decay = jnp.exp2(jnp.where(causal, seg, -jnp.inf))
                ms.append((scores * decay).astype(BF16))
            slab = xdt_b[:, h0 * SSD_HEAD_DIM:(h0 + 2) * SSD_HEAD_DIM]
            zero = jnp.zeros_like(slab)
            rhs = jnp.concatenate([jnp.where(first_half, slab, zero),
                                   jnp.where(first_half, zero, slab)], axis=0)
            diag.append(_dot(jnp.concatenate(ms, axis=1), rhs))
        y_s[:, gs] = jnp.concatenate(diag, axis=1) + y_off + xs[:, gs] * dexp_ref[:, gs]
        yield


def _ssd_finish(g, y_s, z_s, nw_ref, y_ref):
    gs = slice(g * GROUP_W, (g + 1) * GROUP_W)
    y = y_s[:, gs] * _silu(z_s[:, gs])
    y_ref[:, gs] = (y * _rms_scale(y) * nw_ref[:, gs]).astype(BF16)


def _gla_stages(env, r, fresh, gw_ref, gb_ref, nw_ref, tri_ref, state, y_ref):
    c = GLA_CHUNK
    dk, dv = GLA_HEAD_K, GLA_HEAD_V
    row = lax.broadcasted_iota(jnp.int32, (c, c), 0)
    col = lax.broadcasted_iota(jnp.int32, (c, c), 1)
    causal = row >= col

    qf = env["q"][r, :] * (dk ** -0.5)
    kf = env["k"][r, :]
    logit = _dot(env["small"][r, :].astype(BF16), gw_ref[...]) + gb_ref[...]
    lg = -_softplus(-logit) * (_LOG2E / GLA_GATE_NORM)
    bcum = _dot(tri_ref[...], jnp.concatenate(_split3(lg), axis=0))
    blast = bcum[c - 1:c, :]
    qt = (qf * jnp.exp2(bcum)).astype(BF16)
    kt = (kf * jnp.exp2(-bcum)).astype(BF16)
    kd = (kf * jnp.exp2(blast - bcum)).astype(BF16)
    eblast = jnp.exp2(blast)
    yield
    for h in range(GLA_HEADS):
        ks = slice(h * dk, (h + 1) * dk)
        vs = slice(h * dv, (h + 1) * dv)
        attn = jnp.where(causal, _dot_nt(qt[:, ks], kt[:, ks]), 0.0)
        vh = env["v"][r, vs]
        st = jnp.where(fresh, 0.0, state[h])
        o = _dot(attn.astype(BF16), vh) + _dot_nt(qt[:, ks], st.astype(BF16))
        state[h] = eblast[:, ks] * st + _dot_tn(vh, kd[:, ks])
        o = o * _rms_scale(o) * nw_ref[...]
        o = o * _silu(env["go"][r, vs])
        y_ref[:, vs] = o.astype(BF16)
        if h % 2 == 1:
            yield


def _gla_levels(env, rows, gw_ref, gb_ref, nw_ref, tri_ref, state, y_ref):
    c = GLA_CHUNK
    dk, dv = GLA_HEAD_K, GLA_HEAD_V
    row = lax.broadcasted_iota(jnp.int32, (c, c), 0)
    col = lax.broadcasted_iota(jnp.int32, (c, c), 1)
    causal = row >= col
    heads = range(GLA_HEADS)

    logit = _dot(env["small"].astype(BF16), gw_ref[...]) + gb_ref[...]
    lg = -_softplus(-logit) * (_LOG2E / GLA_GATE_NORM)
    yield
    bcum = _dot(tri_ref[...], jnp.concatenate(_split3(lg), axis=0))
    blast = bcum[c - 1:c, :]
    qf = env["q"] * (dk ** -0.5)
    kf = env["k"]
    qt = (qf * jnp.exp2(bcum)).astype(BF16)
    kt = (kf * jnp.exp2(-bcum)).astype(BF16)
    kd = (kf * jnp.exp2(blast - bcum)).astype(BF16)
    eblast = jnp.exp2(blast)
    yield
    ks = [slice(h * dk, (h + 1) * dk) for h in heads]
    vs = [slice(h * dv, (h + 1) * dv) for h in heads]
    attn = [jnp.where(causal, _dot_nt(qt[:, ks[h]], kt[:, ks[h]]), 0.0).astype(BF16) for h in heads]
    yield
    o = [_dot(attn[h], env["v"][:, vs[h]]) + _dot_nt(qt[:, ks[h]], state[h].astype(BF16)) for h in heads]
    for h in heads:
        state[h] = eblast[:, ks[h]] * state[h] + _dot_tn(env["v"][:, vs[h]], kd[:, ks[h]])
    for h in heads:
        oh = o[h] * _rms_scale(o[h]) * nw_ref[...]
        y_ref[rows, vs[h]] = (oh * _silu(env["go"][:, vs[h]])).astype(BF16)
    yield


def _prenorm(x_ref, mod_ref, npre_ref, hb_s):
    x = x_ref[...]
    xn = (x * _rms_scale(x)) * npre_ref[...]
    hb_s[...] = (xn * (1.0 + mod_ref[1:2, :]) + mod_ref[0:1, :]).astype(BF16)


def _mixer_kernel(x0_ref, mod0_ref, xn_ref, modn_ref, npre_ref, wbig_ref, wsmall_ref,
                  cw_ref, cb_ref, dtb_ref, alog_ref, dexp_ref, snw_ref, triq_ref, exp_ref,
                  gw_ref, gb_ref, gnw_ref, tric_ref,
                  ys_ref, yg_ref, xpad, sstate, gstate, hb_s,
                  y_s, z_s, yg0_s, q_s, k_s, v_s, go_s, small_s, *, tiles_per_seq):
    i = pl.program_id(0)
    fresh = lax.rem(i, tiles_per_seq) == 0

    @pl.when(i == 0)
    def _():
        _prenorm(x0_ref, mod0_ref, npre_ref, hb_s)
        gstate[...] = jnp.zeros(gstate.shape, F32)
        for ref in (y_s, z_s, yg0_s, q_s, k_s, v_s, go_s, small_s):
            ref[...] = jnp.zeros(ref.shape, ref.dtype)

    @pl.when(fresh)
    def _():
        xpad[0:HALO, :] = jnp.zeros((HALO, SSD_CONV_DIM), F32)
        sstate[...] = jnp.zeros(sstate.shape, F32)

    hb = hb_s[...]
    first = slice(0, GLA_CHUNK)
    second = slice(GLA_CHUNK, 2 * GLA_CHUNK)
    xbc_cols = [slice(c, c + _PROJ_CHUNK) for c in range(_XBC0, _XBC0 + SSD_CONV_DIM, _PROJ_CHUNK)]
    prev = {"q": q_s[...], "k": k_s[...], "v": v_s[...], "go": go_s[...], "small": small_s[...]}
    gla1 = _gla_levels(prev, second, gw_ref, gb_ref, gnw_ref, tric_ref, gstate, yg_ref)
    yg_ref[first, :] = yg0_s[...]
    next(gla1)
    env = {"small": _dot(hb, wsmall_ref[...])}
    xbc = [_dot(hb, wbig_ref[:, xbc_cols[0]])]
    next(gla1)
    xbc.append(_dot(hb, wbig_ref[:, xbc_cols[1]]))
    _ssd_finish(0, y_s, z_s, snw_ref, ys_ref)
    next(gla1)
    xbc.append(_dot(hb, wbig_ref[:, xbc_cols[2]]))
    _ssd_finish(1, y_s, z_s, snw_ref, ys_ref)
    env["xbc"] = jnp.concatenate(xbc, axis=1)
    ssd = _ssd_stages(env, cw_ref, cb_ref, dtb_ref, alog_ref, dexp_ref, triq_ref, exp_ref, xpad, sstate, y_s)
    gla0 = _gla_stages(env, first, fresh, gw_ref, gb_ref, gnw_ref, tric_ref, gstate, yg0_s)
    env["q"] = _project(hb, wbig_ref, _Q0, GLA_K_WIDTH)
    next(gla1)
    env["k"] = _project(hb, wbig_ref, _K0, GLA_K_WIDTH)
    next(ssd)
    env["v"] = _project(hb, wbig_ref, _V0, GLA_V_WIDTH).astype(BF16)
    next(gla0)
    next(ssd)
    env["go"] = _project(hb, wbig_ref, _GO0, GLA_V_WIDTH)
    next(gla0)
    next(ssd)
    z_s[...] = _project(hb, wbig_ref, _Z0, SSD_WIDTH)
    _prenorm(xn_ref, modn_ref, npre_ref, hb_s)
    next(gla0)
    next(ssd)
    q_s[...] = env["q"][second, :]
    k_s[...] = env["k"][second, :]
    v_s[...] = env["v"][second, :]
    go_s[...] = env["go"][second, :]
    small_s[...] = env["small"][second, :]


def _mixer_call(x, mod, npre, wbig, wsmall, ssd_consts, gla_consts):
    b, s, d = x.shape
    t = SSD_CHUNK
    assert t == 2 * GLA_CHUNK
    tiles_per_seq = s // t
    n = b * tiles_per_seq
    xf = x.reshape(b * s, d)
    nxt = lambda i: jnp.minimum(i + 1, n - 1)
    done = lambda i: (jnp.maximum(i - 1, 0), 0)
    consts = (npre, wbig, wsmall) + tuple(ssd_consts) + tuple(gla_consts)
    ys, yg = pl.pallas_call(
        functools.partial(_mixer_kernel, tiles_per_seq=tiles_per_seq),
        grid=(n + 1,),
        in_specs=[pl.BlockSpec((t, d), lambda i: (0, 0)),
                  pl.BlockSpec((None, ADA_CHUNKS, d), lambda i: (0, 0, 0)),
                  pl.BlockSpec((t, d), lambda i: (nxt(i), 0)),
                  pl.BlockSpec((None, ADA_CHUNKS, d), lambda i: (nxt(i) // tiles_per_seq, 0, 0))]
        + [_resident(a.shape) for a in consts],
        out_specs=[pl.BlockSpec((t, SSD_WIDTH), done),
                   pl.BlockSpec((t, GLA_V_WIDTH), done)],
        out_shape=[jax.ShapeDtypeStruct((b * s, SSD_WIDTH), BF16),
                   jax.ShapeDtypeStruct((b * s, GLA_V_WIDTH), BF16)],
        scratch_shapes=[
            pltpu.VMEM((HALO + SSD_CHUNK, SSD_CONV_DIM), F32),
            pltpu.VMEM((SSD_GROUPS, SSD_STATE, GROUP_W), F32),
            pltpu.VMEM((GLA_HEADS, GLA_HEAD_V, GLA_HEAD_K), F32),
            pltpu.VMEM((t, d), BF16),
            pltpu.VMEM((t, SSD_WIDTH), F32),
            pltpu.VMEM((t, SSD_WIDTH), F32),
            pltpu.VMEM((GLA_CHUNK, GLA_V_WIDTH), BF16),
            pltpu.VMEM((GLA_CHUNK, GLA_K_WIDTH), F32),
            pltpu.VMEM((GLA_CHUNK, GLA_K_WIDTH), F32),
            pltpu.VMEM((GLA_CHUNK, GLA_V_WIDTH), BF16),
            pltpu.VMEM((GLA_CHUNK, GLA_V_WIDTH), F32),
            pltpu.VMEM((GLA_CHUNK, SMALL_W), F32),
        ],
        compiler_params=_params("arbitrary"),
        name="token_mixer",
    )(xf, mod, xf, mod, *consts)
    return ys.reshape(b, s, SSD_WIDTH), yg.reshape(b, s, GLA_V_WIDTH)


_UP_CHUNK = 384
_UP_DOT = 768
_FFN_TM = 256
_CAST_ROWS = 128


def _cast_weight(w_hbm, dst, stage, sem):
    rows = stage.shape[1]
    n = w_hbm.shape[0] // rows

    def copy(c):
        return pltpu.make_async_copy(w_hbm.at[pl.ds(c * rows, rows), :], stage.at[c % 2], sem.at[c % 2])

    copy(0).start()
    for c in range(n):
        if c + 1 < n:
            copy(c + 1).start()
        copy(c).wait()
        dst[c * rows:(c + 1) * rows, :] = stage[c % 2].astype(BF16)


def _stage_ffn_input(ys_ref, yg_ref, x_ref, mod_ref, npost1_ref, npre_ref, wout_ref, x1_s, hb_s):
    y = _dot(ys_ref[...], wout_ref[0:SSD_WIDTH, :]) + _dot(yg_ref[...], wout_ref[SSD_WIDTH:, :])
    yn = (y * _rms_scale(y)) * npost1_ref[...]
    x1 = x_ref[...] + mod_ref[2:3, :] * yn
    x1_s[...] = x1
    h = ((x1 * _rms_scale(x1)) * npre_ref[...]) * (1.0 + mod_ref[4:5, :]) + mod_ref[3:4, :]
    hb_s[...] = h.astype(BF16)


def _ffn_kernel(ys0_ref, yg0_ref, x0_ref, mod0_ref, ysn_ref, ygn_ref, xn_ref, modn_ref, mod_ref,
                npost1_ref, npre_ref, wout_hbm, wup_hbm, cw_ref, cb_ref, npost2_ref, wdown_hbm,
                o_ref, upad, gate, x1_s, hb_s,
                wout_ref, wup_ref, wdown_ref, st_out, st_up, st_down, sem, *, tiles_per_seq):
    tm = o_ref.shape[0]
    i = pl.program_id(0)

    @pl.when(i == 0)
    def _():
        _cast_weight(wout_hbm, wout_ref, st_out, sem.at[0])
        _cast_weight(wup_hbm, wup_ref, st_up, sem.at[1])
        _cast_weight(wdown_hbm, wdown_ref, st_down, sem.at[2])
        _stage_ffn_input(ys0_ref, yg0_ref, x0_ref, mod0_ref, npost1_ref, npre_ref, wout_ref, x1_s, hb_s)

    @pl.when(lax.rem(i, tiles_per_seq) == 0)
    def _():
        upad[0:HALO, :] = jnp.zeros((HALO, upad.shape[1]), F32)

    x1 = x1_s[...]
    hb = hb_s[...]

    done = set()

    def up(col):
        p = col // _UP_DOT
        if p not in done:
            done.add(p)
            cols = slice(p * _UP_DOT, (p + 1) * _UP_DOT)
            upad[HALO:HALO + tm, cols] = _dot(hb, wup_ref[:, cols])

    def conv(c0):
        cols = slice(c0, c0 + _UP_CHUNK)
        up(c0)
        up(c0 + _UP_CHUNK - 1)
        ext = upad[:, cols]
        acc = cw_ref[0:1, cols] * ext
        for k in range(1, FFN_CONV):
            acc = pltpu.roll(acc, 1, 0) + cw_ref[k:k + 1, cols] * ext
        upad[0:HALO, cols] = upad[tm:tm + HALO, cols]
        return acc[HALO:HALO + tm, :] + cb_ref[:, cols]

    for c in range(0, FFN_HIDDEN, _UP_CHUNK):
        gate[:, c:c + _UP_CHUNK] = (_silu(conv(c)) * conv(FFN_HIDDEN + c)).astype(BF16)
    _stage_ffn_input(ysn_ref, ygn_ref, xn_ref, modn_ref, npost1_ref, npre_ref, wout_ref, x1_s, hb_s)
    f = _dot(gate[...], wdown_ref[...])
    fn = (f * _rms_scale(f)) * npost2_ref[...]
    o_ref[...] = x1 + mod_ref[5:6, :] * fn


def _ffn_call(ys, yg, x, mod, npost1, npre, wout, wup, cw, cb, npost2, wdown):
    b, s, d = x.shape
    tm = _FFN_TM
    tiles_per_seq = s // tm
    n = b * tiles_per_seq
    flat = lambda a: a.reshape(b * s, a.shape[-1])
    nxt = lambda i: jnp.minimum(i + 1, n - 1)
    first = lambda w: pl.BlockSpec((tm, w), lambda i: (0, 0))
    ahead = lambda w: pl.BlockSpec((tm, w), lambda i: (nxt(i), 0))
    mod_of = lambda tile: pl.BlockSpec((None, ADA_CHUNKS, d), lambda i: (tile(i) // tiles_per_seq, 0, 0))
    consts = (npost1, npre, wout, wup, cw, cb, npost2, wdown)
    in_hbm = pl.BlockSpec(memory_space=pl.ANY)
    out = pl.pallas_call(
        functools.partial(_ffn_kernel, tiles_per_seq=tiles_per_seq),
        grid=(n,),
        in_specs=[first(SSD_WIDTH), first(GLA_V_WIDTH), first(d), mod_of(lambda i: 0 * i),
                  ahead(SSD_WIDTH), ahead(GLA_V_WIDTH), ahead(d), mod_of(nxt), mod_of(lambda i: i)]
        + [in_hbm if any(a is w for w in (wout, wup, wdown)) else _resident(a.shape) for a in consts],
        out_specs=pl.BlockSpec((tm, d), lambda i: (i, 0)),
        out_shape=jax.ShapeDtypeStruct((b * s, d), F32),
        scratch_shapes=[pltpu.VMEM((HALO + tm, 2 * FFN_HIDDEN), F32),
                        pltpu.VMEM((tm, FFN_HIDDEN), BF16),
                        pltpu.VMEM((tm, d), F32), pltpu.VMEM((tm, d), BF16),
                        pltpu.VMEM(wout.shape, BF16), pltpu.VMEM(wup.shape, BF16), pltpu.VMEM(wdown.shape, BF16),
                        pltpu.VMEM((2, _CAST_ROWS, wout.shape[1]), F32),
                        pltpu.VMEM((2, _CAST_ROWS, wup.shape[1]), F32),
                        pltpu.VMEM((2, _CAST_ROWS, wdown.shape[1]), F32),
                        pltpu.SemaphoreType.DMA((3, 2))],
        compiler_params=_params("arbitrary"),
        name="channel_mixer",
    )(flat(ys), flat(yg), flat(x), mod, flat(ys), flat(yg), flat(x), mod, mod, *consts)
    return out.reshape(b, s, d)


def _lane_pad(v, width):
    return jnp.pad(v, ((0, 0), (0, width - v.shape[1])))


def _constants():
    q, c = SSD_CHUNK, GLA_CHUNK
    tri_q = np.tril(np.ones((q, q), np.float32))
    tri_c = np.tril(np.ones((c, c), np.float32))
    expand = np.zeros((SMALL_W, SSD_WIDTH), np.float32)
    for h in range(SSD_HEADS):
        expand[h, h * SSD_HEAD_DIM:(h + 1) * SSD_HEAD_DIM] = 1.0
    return (jnp.asarray(np.tile(tri_q, (1, 3)), BF16),
            jnp.asarray(np.tile(tri_c, (1, 3)), BF16),
            jnp.asarray(np.tile(expand, (3, 1)), BF16))


def kernel(x, c, w_ada, b_ada, norm_mix_pre, norm_mix_post, norm_ffn_pre, norm_ffn_post, w_in, ssd_conv_w, ssd_conv_b, ssd_dt_bias, ssd_a_log, ssd_d, ssd_norm, gla_gate_w, gla_gate_b, gla_norm, w_out, ffn_up, ffn_conv_w, ffn_conv_b, ffn_down):
    bsz, seqlen, d = x.shape
    depth = w_ada.shape[0]
    tri_q3, tri_c3, expand3 = _constants()
    c_pad = jnp.pad(c, ((0, SUBLANES - bsz % SUBLANES), (0, 0))) if bsz % SUBLANES else c

    sizes = [SSD_WIDTH, SSD_CONV_DIM, SSD_HEADS, GLA_K_WIDTH, GLA_K_WIDTH, GLA_V_WIDTH,
             GLA_GATE_RANK, GLA_V_WIDTH]
    o = np.concatenate([[0], np.cumsum(sizes)])

    for i in range(depth):
        mod = _ada_call(c_pad, w_ada[i], b_ada[i][None, :])[:bsz].reshape(bsz, ADA_CHUNKS, d)

        wbig, wsmall = _reorder_call(jnp.swapaxes(w_in[i], 0, 1), o)
        ssd_consts = (ssd_conv_w[i], ssd_conv_b[i][None, :],
                      _lane_pad(ssd_dt_bias[i][None, :], SMALL_W), _lane_pad(ssd_a_log[i][None, :], SMALL_W),
                      jnp.repeat(ssd_d[i], SSD_HEAD_DIM)[None, :], ssd_norm[i][None, :], tri_q3, expand3)
        gw = jnp.zeros((SMALL_W, GLA_K_WIDTH), F32).at[SSD_HEADS:SSD_HEADS + GLA_GATE_RANK].set(
            gla_gate_w[i]).astype(BF16)
        gla_consts = (gw, gla_gate_b[i][None, :], gla_norm[i][None, :], tri_c3)
        y_ssd, y_gla = _mixer_call(x, mod, norm_mix_pre[i][None, :], wbig, wsmall, ssd_consts, gla_consts)

        x = _ffn_call(y_ssd, y_gla, x, mod, norm_mix_post[i][None, :], norm_ffn_pre[i][None, :],
                      w_out[i], ffn_up[i], ffn_conv_w[i], ffn_conv_b[i][None, :],
                      norm_ffn_post[i][None, :], ffn_down[i])
    return x
```

```python
import functools

import numpy as np
import jax
import jax.numpy as jnp
from jax import lax
from jax.experimental import pallas as pl
from jax.experimental.pallas import tpu as pltpu

F32 = jnp.float32
BF16 = jnp.bfloat16

D_MODEL = 1024
MIX_WIDTH = 2 * D_MODEL
SSD_WIDTH = MIX_WIDTH // 2
GLA_V_WIDTH = MIX_WIDTH - SSD_WIDTH
SSD_HEAD_DIM = 64
SSD_HEADS = SSD_WIDTH // SSD_HEAD_DIM
SSD_GROUPS = 2
SSD_STATE = 128
SSD_CONV = 4
SSD_CHUNK = 128
SSD_CONV_DIM = SSD_WIDTH + 2 * SSD_GROUPS * SSD_STATE
GLA_HEADS = 4
GLA_K_WIDTH = GLA_V_WIDTH // 2
GLA_HEAD_K = GLA_K_WIDTH // GLA_HEADS
GLA_HEAD_V = GLA_V_WIDTH // GLA_HEADS
GLA_GATE_RANK = 16
GLA_GATE_NORM = 16.0
GLA_CHUNK = 64
FFN_HIDDEN = int(round(8 * D_MODEL / 3 / 128)) * 128
FFN_CONV = 3
ADA_CHUNKS = 6
NORM_EPS = 1e-6
_LOG2E = float(np.log2(np.e))

LANES = 128
SUBLANES = 8
VMEM_LIMIT_BYTES = 58 * 1024 * 1024

SMALL_W = LANES
GROUP_W = SSD_WIDTH // SSD_GROUPS
HEADS_PER_GROUP = SSD_HEADS // SSD_GROUPS
HALO = SUBLANES


def _silu(x):
    return x * jax.nn.sigmoid(x)


def _softplus(x):
    return jnp.maximum(x, 0.0) + jnp.log(1.0 + jnp.exp(-jnp.abs(x)))


def _rms_scale(x):
    return lax.rsqrt(jnp.mean(x * x, axis=-1, keepdims=True) + NORM_EPS)


def _split3(x):
    hi = x.astype(BF16)
    r1 = x - hi.astype(F32)
    mid = r1.astype(BF16)
    lo = (r1 - mid.astype(F32)).astype(BF16)
    return hi, mid, lo


def _dot(a, b):
    return jnp.dot(a, b, preferred_element_type=F32)


def _dot_nt(a, b):
    return lax.dot_general(a, b, (((1,), (1,)), ((), ())), preferred_element_type=F32)


def _dot_tn(a, b):
    return lax.dot_general(a, b, (((0,), (0,)), ((), ())), preferred_element_type=F32)


def _resident(shape):
    nd = len(shape)
    return pl.BlockSpec(shape, lambda *_: (0,) * nd, pipeline_mode=pl.Buffered(1))


def _params(*sem):
    return pltpu.CompilerParams(dimension_semantics=sem, vmem_limit_bytes=VMEM_LIMIT_BYTES)


def _ada_kernel(c_ref, w_ref, b_ref, o_ref):
    ca = _silu(c_ref[...]).astype(BF16)
    o_ref[...] = _dot(ca, w_ref[...].astype(BF16)) + b_ref[...]


def _ada_call(c_pad, w, b):
    rows, d = c_pad.shape
    n = w.shape[1]
    tn = 1024
    return pl.pallas_call(
        _ada_kernel,
        grid=(n // tn,),
        in_specs=[
            pl.BlockSpec((rows, d), lambda j: (0, 0)),
            pl.BlockSpec((d, tn), lambda j: (0, j)),
            pl.BlockSpec((1, tn), lambda j: (0, j)),
        ],
        out_specs=pl.BlockSpec((rows, tn), lambda j: (0, j)),
        out_shape=jax.ShapeDtypeStruct((rows, n), F32),
        compiler_params=_params("arbitrary"),
        name="ada_mod",
    )(c_pad, w, b)


def _reorder_kernel(wt_ref, big_ref, small_ref, *, offs):
    dst = pl.program_id(0) * _PROJ_CHUNK
    skip_dt = offs[3] - offs[2]
    skip_glr = offs[7] - offs[6]
    src = dst + jnp.where(dst >= offs[2], skip_dt, 0) + jnp.where(dst >= offs[6] - skip_dt, skip_glr, 0)
    rows = wt_ref[pl.ds(pl.multiple_of(src, 2 * SUBLANES), _PROJ_CHUNK), :]
    big_ref[...] = rows.T.astype(BF16)

    @pl.when(pl.program_id(0) == 0)
    def _():
        d = wt_ref.shape[1]
        pad = jnp.zeros((SMALL_W - skip_dt - skip_glr, d), F32)
        small = jnp.concatenate([wt_ref[offs[2]:offs[3], :], wt_ref[offs[6]:offs[7], :], pad], axis=0)
        small_ref[...] = small.T.astype(BF16)


def _reorder_call(wt, offs):
    n_in, d = wt.shape
    return pl.pallas_call(
        functools.partial(_reorder_kernel, offs=tuple(int(v) for v in offs)),
        grid=(_PROJ_W // _PROJ_CHUNK,),
        in_specs=[_resident((n_in, d))],
        out_specs=[pl.BlockSpec((d, _PROJ_CHUNK), lambda j: (0, j)), pl.BlockSpec((d, SMALL_W), lambda j: (0, 0))],
        out_shape=[jax.ShapeDtypeStruct((d, _PROJ_W), BF16), jax.ShapeDtypeStruct((d, SMALL_W), BF16)],
        compiler_params=_params("arbitrary"),
        name="reorder_w_in",
    )(wt)


_Z0 = 0
_XBC0 = _Z0 + SSD_WIDTH
_Q0 = _XBC0 + SSD_CONV_DIM
_K0 = _Q0 + GLA_K_WIDTH
_V0 = _K0 + GLA_K_WIDTH
_GO0 = _V0 + GLA_V_WIDTH
_PROJ_W = _GO0 + GLA_V_WIDTH
_PROJ_CHUNK = 512


def _project(hb, w_ref, c0, width):
    parts = [_dot(hb, w_ref[:, c:c + min(_PROJ_CHUNK, c0 + width - c)])
             for c in range(c0, c0 + width, _PROJ_CHUNK)]
    return parts[0] if len(parts) == 1 else jnp.concatenate(parts, axis=1)


def _ssd_stages(env, live, cw_ref, cb_ref, dtb_ref, alog_ref, dexp_ref, nw_ref, tri_ref, exp_ref,
                xpad, state, y_ref):
    q = SSD_CHUNK
    n = SSD_STATE

    xpad[HALO:HALO + q, :] = env["xbc"]
    ext = xpad[...]
    acc = cw_ref[0:1, :] * ext
    for k in range(1, SSD_CONV):
        acc = pltpu.roll(acc, 1, 0) + cw_ref[k:k + 1, :] * ext
    acc = acc[HALO:HALO + q, :] + cb_ref[...]
    xpad[0:HALO, :] = xpad[q:q + HALO, :]
    xc = _silu(acc)
    xs = xc[:, :SSD_WIDTH]
    bm = xc[:, SSD_WIDTH:SSD_WIDTH + SSD_GROUPS * n].astype(BF16)
    cm = xc[:, SSD_WIDTH + SSD_GROUPS * n:].astype(BF16)
    yield

    lane = lax.broadcasted_iota(jnp.int32, (q, SMALL_W), 1)
    head_lane = lane < SSD_HEADS
    dt = _softplus(env["small"] + dtb_ref[...])
    a = -jnp.exp(alog_ref[...]) * _LOG2E
    dt = jnp.where(head_lane, dt, 0.0)
    da = jnp.where(head_lane, dt * a, 0.0)
    cs = _dot(tri_ref[...], jnp.concatenate(_split3(da), axis=0))
    cs_t = cs.T
    cs_e = _dot(jnp.concatenate(_split3(cs), axis=1), exp_ref[...])
    dt_e = _dot(jnp.concatenate(_split3(dt), axis=1), exp_ref[...])
    cs_last = cs_e[q - 1:q, :]

    xdt = xs * dt_e
    xdt_b = xdt.astype(BF16)
    xdec_b = (xdt * jnp.exp2(cs_last - cs_e)).astype(BF16)
    decay_in = jnp.exp2(cs_e)
    decay_chunk = jnp.exp2(cs_last)
    yield

    row = lax.broadcasted_iota(jnp.int32, (q, q), 0)
    col = lax.broadcasted_iota(jnp.int32, (q, q), 1)
    causal = row >= col
    first_half = col < SSD_HEAD_DIM

    y_parts = []
    for g in range(SSD_GROUPS):
        cmg = cm[:, g * n:(g + 1) * n]
        bmg = bm[:, g * n:(g + 1) * n]
        gs = slice(g * GROUP_W, (g + 1) * GROUP_W)
        scores = _dot_nt(cmg, bmg)
        y_off = _dot(cmg, state[g].astype(BF16)) * decay_in[:, gs]
        state[g] = decay_chunk[:, gs] * state[g] + _dot_tn(bmg, xdec_b[:, gs])
        diag = []
        for pair in range(HEADS_PER_GROUP // 2):
            h0 = g * HEADS_PER_GROUP + 2 * pair
            ms = []
            for h in (h0, h0 + 1):
                seg = cs[:, h:h + 1] - cs_t[h:h + 1, :]
                decay = jnp.exp2(jnp.where(causal, seg, -jnp.inf))
                ms.append((scores * decay).astype(BF16))
            slab = xdt_b[:, h0 * SSD_HEAD_DIM:(h0 + 2) * SSD_HEAD_DIM]
            zero = jnp.zeros_like(slab)
            rhs = jnp.concatenate([jnp.where(first_half, slab, zero),
                                   jnp.where(first_half, zero, slab)], axis=0)
            diag.append(_dot(jnp.concatenate(ms, axis=1), rhs))
        y_parts.append(jnp.concatenate(diag, axis=1) + y_off)
        yield
    y = jnp.concatenate(y_parts, axis=1) + xs * dexp_ref[...]
    y = y * _silu(env["z"])
    outs = []
    for g in range(SSD_GROUPS):
        yg = y[:, g * GROUP_W:(g + 1) * GROUP_W]
        outs.append(yg * _rms_scale(yg))
    out = (jnp.concatenate(outs, axis=1) * nw_ref[...]).astype(BF16)
    y_ref[...] = jnp.where(live, out, y_ref[...])
    yield


def _gla_stages(env, r, fresh, gw_ref, gb_ref, nw_ref, tri_ref, state, y_ref):
    c = GLA_CHUNK
    dk, dv = GLA_HEAD_K, GLA_HEAD_V
    row = lax.broadcasted_iota(jnp.int32, (c, c), 0)
    col = lax.broadcasted_iota(jnp.int32, (c, c), 1)
    causal = row >= col

    qf = env["q"][r, :] * (dk ** -0.5)
    kf = env["k"][r, :]
    logit = _dot(env["small"][r, :].astype(BF16), gw_ref[...]) + gb_ref[...]
    lg = -_softplus(-logit) * (_LOG2E / GLA_GATE_NORM)
    bcum = _dot(tri_ref[...], jnp.concatenate(_split3(lg), axis=0))
    blast = bcum[c - 1:c, :]
    qt = (qf * jnp.exp2(bcum)).astype(BF16)
    kt = (kf * jnp.exp2(-bcum)).astype(BF16)
    kd = (kf * jnp.exp2(blast - bcum)).astype(BF16)
    eblast = jnp.exp2(blast)
    yield
    for h in range(GLA_HEADS):
        ks = slice(h * dk, (h + 1) * dk)
        vs = slice(h * dv, (h + 1) * dv)
        attn = jnp.where(causal, _dot_nt(qt[:, ks], kt[:, ks]), 0.0)
        vh = env["v"][r, vs]
        st = jnp.where(fresh, 0.0, state[h])
        o = _dot(attn.astype(BF16), vh) + _dot_nt(qt[:, ks], st.astype(BF16))
        state[h] = eblast[:, ks] * st + _dot_tn(vh, kd[:, ks])
        o = o * _rms_scale(o) * nw_ref[...]
        o = o * _silu(env["go"][r, vs])
        y_ref[:, vs] = o.astype(BF16)
        if h % 2 == 1:
            yield


def _gla_levels(env, rows, gw_ref, gb_ref, nw_ref, tri_ref, state, y_ref):
    c = GLA_CHUNK
    dk, dv = GLA_HEAD_K, GLA_HEAD_V
    row = lax.broadcasted_iota(jnp.int32, (c, c), 0)
    col = lax.broadcasted_iota(jnp.int32, (c, c), 1)
    causal = row >= col
    heads = range(GLA_HEADS)

    logit = _dot(env["small"].astype(BF16), gw_ref[...]) + gb_ref[...]
    lg = -_softplus(-logit) * (_LOG2E / GLA_GATE_NORM)
    yield
    bcum = _dot(tri_ref[...], jnp.concatenate(_split3(lg), axis=0))
    blast = bcum[c - 1:c, :]
    qf = env["q"] * (dk ** -0.5)
    kf = env["k"]
    qt = (qf * jnp.exp2(bcum)).astype(BF16)
    kt = (kf * jnp.exp2(-bcum)).astype(BF16)
    kd = (kf * jnp.exp2(blast - bcum)).astype(BF16)
    eblast = jnp.exp2(blast)
    yield
    ks = [slice(h * dk, (h + 1) * dk) for h in heads]
    vs = [slice(h * dv, (h + 1) * dv) for h in heads]
    attn = [jnp.where(causal, _dot_nt(qt[:, ks[h]], kt[:, ks[h]]), 0.0).astype(BF16) for h in heads]
    yield
    o = [_dot(attn[h], env["v"][:, vs[h]]) + _dot_nt(qt[:, ks[h]], state[h].astype(BF16)) for h in heads]
    for h in heads:
        state[h] = eblast[:, ks[h]] * state[h] + _dot_tn(env["v"][:, vs[h]], kd[:, ks[h]])
    for h in heads:
        oh = o[h] * _rms_scale(o[h]) * nw_ref[...]
        y_ref[rows, vs[h]] = (oh * _silu(env["go"][:, vs[h]])).astype(BF16)
    yield


def _prenorm(x_ref, mod_ref, npre_ref, hb_s):
    x = x_ref[...]
    xn = (x * _rms_scale(x)) * npre_ref[...]
    hb_s[...] = (xn * (1.0 + mod_ref[1:2, :]) + mod_ref[0:1, :]).astype(BF16)


def _mixer_kernel(x0_ref, mod0_ref, xn_ref, modn_ref, npre_ref, wbig_ref, wsmall_ref,
                  cw_ref, cb_ref, dtb_ref, alog_ref, dexp_ref, snw_ref, triq_ref, exp_ref,
                  gw_ref, gb_ref, gnw_ref, tric_ref,
                  ys_ref, yg_ref, xpad, sstate, gstate, hb_s,
                  yg0_s, q_s, k_s, v_s, go_s, small_s, *, tiles_per_seq):
    i = pl.program_id(0)
    fresh = lax.rem(i, tiles_per_seq) == 0
    live = i < pl.num_programs(0) - 1

    @pl.when(i == 0)
    def _():
        _prenorm(x0_ref, mod0_ref, npre_ref, hb_s)
        gstate[...] = jnp.zeros(gstate.shape, F32)
        for ref in (yg0_s, q_s, k_s, v_s, go_s, small_s):
            ref[...] = jnp.zeros(ref.shape, ref.dtype)

    @pl.when(fresh)
    def _():
        xpad[0:HALO, :] = jnp.zeros((HALO, SSD_CONV_DIM), F32)
        sstate[...] = jnp.zeros(sstate.shape, F32)

    hb = hb_s[...]
    first = slice(0, GLA_CHUNK)
    second = slice(GLA_CHUNK, 2 * GLA_CHUNK)
    xbc_cols = [slice(c, c + _PROJ_CHUNK) for c in range(_XBC0, _XBC0 + SSD_CONV_DIM, _PROJ_CHUNK)]
    prev = {"q": q_s[...], "k": k_s[...], "v": v_s[...], "go": go_s[...], "small": small_s[...]}
    gla1 = _gla_levels(prev, second, gw_ref, gb_ref, gnw_ref, tric_ref, gstate, yg_ref)
    yg_ref[first, :] = yg0_s[...]
    next(gla1)
    env = {"small": _dot(hb, wsmall_ref[...])}
    xbc = [_dot(hb, wbig_ref[:, xbc_cols[0]])]
    next(gla1)
    xbc.append(_dot(hb, wbig_ref[:, xbc_cols[1]]))
    next(gla1)
    xbc.append(_dot(hb, wbig_ref[:, xbc_cols[2]]))
    env["xbc"] = jnp.concatenate(xbc, axis=1)
    ssd = _ssd_stages(env, live, cw_ref, cb_ref, dtb_ref, alog_ref, dexp_ref, snw_ref, triq_ref, exp_ref,
                      xpad, sstate, ys_ref)
    gla0 = _gla_stages(env, first, fresh, gw_ref, gb_ref, gnw_ref, tric_ref, gstate, yg0_s)
    env["q"] = _project(hb, wbig_ref, _Q0, GLA_K_WIDTH)
    next(gla1)
    env["k"] = _project(hb, wbig_ref, _K0, GLA_K_WIDTH)
    next(ssd)
    env["v"] = _project(hb, wbig_ref, _V0, GLA_V_WIDTH).astype(BF16)
    next(gla0)
    next(ssd)
    env["go"] = _project(hb, wbig_ref, _GO0, GLA_V_WIDTH)
    next(gla0)
    next(ssd)
    env["z"] = _project(hb, wbig_ref, _Z0, SSD_WIDTH)
    _prenorm(xn_ref, modn_ref, npre_ref, hb_s)
    next(gla0)
    next(ssd)
    next(ssd)
    q_s[...] = env["q"][second, :]
    k_s[...] = env["k"][second, :]
    v_s[...] = env["v"][second, :]
    go_s[...] = env["go"][second, :]
    small_s[...] = env["small"][second, :]


def _mixer_call(x, mod, npre, wbig, wsmall, ssd_consts, gla_consts):
    b, s, d = x.shape
    t = SSD_CHUNK
    assert t == 2 * GLA_CHUNK
    tiles_per_seq = s // t
    n = b * tiles_per_seq
    xf = x.reshape(b * s, d)
    nxt = lambda i: jnp.minimum(i + 1, n - 1)
    cur = lambda i: (jnp.minimum(i, n - 1), 0)
    done = lambda i: (jnp.maximum(i - 1, 0), 0)
    consts = (npre, wbig, wsmall) + tuple(ssd_consts) + tuple(gla_consts)
    ys, yg = pl.pallas_call(
        functools.partial(_mixer_kernel, tiles_per_seq=tiles_per_seq),
        grid=(n + 1,),
        in_specs=[pl.BlockSpec((t, d), lambda i: (0, 0)),
                  pl.BlockSpec((None, ADA_CHUNKS, d), lambda i: (0, 0, 0)),
                  pl.BlockSpec((t, d), lambda i: (nxt(i), 0)),
                  pl.BlockSpec((None, ADA_CHUNKS, d), lambda i: (nxt(i) // tiles_per_seq, 0, 0))]
        + [_resident(a.shape) for a in consts],
        out_specs=[pl.BlockSpec((t, SSD_WIDTH), cur),
                   pl.BlockSpec((t, GLA_V_WIDTH), done)],
        out_shape=[jax.ShapeDtypeStruct((b * s, SSD_WIDTH), BF16),
                   jax.ShapeDtypeStruct((b * s, GLA_V_WIDTH), BF16)],
        scratch_shapes=[
            pltpu.VMEM((HALO + SSD_CHUNK, SSD_CONV_DIM), F32),
            pltpu.VMEM((SSD_GROUPS, SSD_STATE, GROUP_W), F32),
            pltpu.VMEM((GLA_HEADS, GLA_HEAD_V, GLA_HEAD_K), F32),
            pltpu.VMEM((t, d), BF16),
            pltpu.VMEM((GLA_CHUNK, GLA_V_WIDTH), BF16),
            pltpu.VMEM((GLA_CHUNK, GLA_K_WIDTH), F32),
            pltpu.VMEM((GLA_CHUNK, GLA_K_WIDTH), F32),
            pltpu.VMEM((GLA_CHUNK, GLA_V_WIDTH), BF16),
            pltpu.VMEM((GLA_CHUNK, GLA_V_WIDTH), F32),
            pltpu.VMEM((GLA_CHUNK, SMALL_W), F32),
        ],
        compiler_params=_params("arbitrary"),
        name="token_mixer",
    )(xf, mod, xf, mod, *consts)
    return ys.reshape(b, s, SSD_WIDTH), yg.reshape(b, s, GLA_V_WIDTH)


_UP_CHUNK = 384
_UP_DOT = 768
_FFN_TM = 256
_CAST_ROWS = 128


def _cast_weight(w_hbm, dst, stage, sem):
    rows = stage.shape[1]
    n = w_hbm.shape[0] // rows

    def copy(c):
        return pltpu.make_async_copy(w_hbm.at[pl.ds(c * rows, rows), :], stage.at[c % 2], sem.at[c % 2])

    copy(0).start()
    for c in range(n):
        if c + 1 < n:
            copy(c + 1).start()
        copy(c).wait()
        dst[c * rows:(c + 1) * rows, :] = stage[c % 2].astype(BF16)


def _stage_ffn_input(ys_ref, yg_ref, x_ref, mod_ref, npost1_ref, npre_ref, wout_ref, x1_s, hb_s):
    y = _dot(ys_ref[...], wout_ref[0:SSD_WIDTH, :]) + _dot(yg_ref[...], wout_ref[SSD_WIDTH:, :])
    yn = (y * _rms_scale(y)) * npost1_ref[...]
    x1 = x_ref[...] + mod_ref[2:3, :] * yn
    x1_s[...] = x1
    h = ((x1 * _rms_scale(x1)) * npre_ref[...]) * (1.0 + mod_ref[4:5, :]) + mod_ref[3:4, :]
    hb_s[...] = h.astype(BF16)


def _ffn_kernel(ys0_ref, yg0_ref, x0_ref, mod0_ref, ysn_ref, ygn_ref, xn_ref, modn_ref, mod_ref,
                npost1_ref, npre_ref, wout_hbm, wup_hbm, cw_ref, cb_ref, npost2_ref, wdown_hbm,
                o_ref, upad, gate, x1_s, hb_s,
                wout_ref, wup_ref, wdown_ref, st_out, st_up, st_down, sem, *, tiles_per_seq):
    tm = o_ref.shape[0]
    i = pl.program_id(0)

    @pl.when(i == 0)
    def _():
        _cast_weight(wout_hbm, wout_ref, st_out, sem.at[0])
        _cast_weight(wup_hbm, wup_ref, st_up, sem.at[1])
        _cast_weight(wdown_hbm, wdown_ref, st_down, sem.at[2])
        _stage_ffn_input(ys0_ref, yg0_ref, x0_ref, mod0_ref, npost1_ref, npre_ref, wout_ref, x1_s, hb_s)

    @pl.when(lax.rem(i, tiles_per_seq) == 0)
    def _():
        upad[0:HALO, :] = jnp.zeros((HALO, upad.shape[1]), F32)

    x1 = x1_s[...]
    hb = hb_s[...]

    done = set()

    def up(col):
        p = col // _UP_DOT
        if p not in done:
            done.add(p)
            cols = slice(p * _UP_DOT, (p + 1) * _UP_DOT)
            upad[HALO:HALO + tm, cols] = _dot(hb, wup_ref[:, cols])

    def conv(c0):
        cols = slice(c0, c0 + _UP_CHUNK)
        up(c0)
        up(c0 + _UP_CHUNK - 1)
        ext = upad[:, cols]
        acc = cw_ref[0:1, cols] * ext
        for k in range(1, FFN_CONV):
            acc = pltpu.roll(acc, 1, 0) + cw_ref[k:k + 1, cols] * ext
        upad[0:HALO, cols] = upad[tm:tm + HALO, cols]
        return acc[HALO:HALO + tm, :] + cb_ref[:, cols]

    for c in range(0, FFN_HIDDEN, _UP_CHUNK):
        gate[:, c:c + _UP_CHUNK] = (_silu(conv(c)) * conv(FFN_HIDDEN + c)).astype(BF16)
    _stage_ffn_input(ysn_ref, ygn_ref, xn_ref, modn_ref, npost1_ref, npre_ref, wout_ref, x1_s, hb_s)
    f = _dot(gate[...], wdown_ref[...])
    fn = (f * _rms_scale(f)) * npost2_ref[...]
    o_ref[...] = x1 + mod_ref[5:6, :] * fn


def _ffn_call(ys, yg, x, mod, npost1, npre, wout, wup, cw, cb, npost2, wdown):
    b, s, d = x.shape
    tm = _FFN_TM
    tiles_per_seq = s // tm
    n = b * tiles_per_seq
    flat = lambda a: a.reshape(b * s, a.shape[-1])
    nxt = lambda i: jnp.minimum(i + 1, n - 1)
    first = lambda w: pl.BlockSpec((tm, w), lambda i: (0, 0))
    ahead = lambda w: pl.BlockSpec((tm, w), lambda i: (nxt(i), 0))
    mod_of = lambda tile: pl.BlockSpec((None, ADA_CHUNKS, d), lambda i: (tile(i) // tiles_per_seq, 0, 0))
    consts = (npost1, npre, wout, wup, cw, cb, npost2, wdown)
    in_hbm = pl.BlockSpec(memory_space=pl.ANY)
    out = pl.pallas_call(
        functools.partial(_ffn_kernel, tiles_per_seq=tiles_per_seq),
        grid=(n,),
        in_specs=[first(SSD_WIDTH), first(GLA_V_WIDTH), first(d), mod_of(lambda i: 0 * i),
                  ahead(SSD_WIDTH), ahead(GLA_V_WIDTH), ahead(d), mod_of(nxt), mod_of(lambda i: i)]
        + [in_hbm if any(a is w for w in (wout, wup, wdown)) else _resident(a.shape) for a in consts],
        out_specs=pl.BlockSpec((tm, d), lambda i: (i, 0)),
        out_shape=jax.ShapeDtypeStruct((b * s, d), F32),
        scratch_shapes=[pltpu.VMEM((HALO + tm, 2 * FFN_HIDDEN), F32),
                        pltpu.VMEM((tm, FFN_HIDDEN), BF16),
                        pltpu.VMEM((tm, d), F32), pltpu.VMEM((tm, d), BF16),
                        pltpu.VMEM(wout.shape, BF16), pltpu.VMEM(wup.shape, BF16), pltpu.VMEM(wdown.shape, BF16),
                        pltpu.VMEM((2, _CAST_ROWS, wout.shape[1]), F32),
                        pltpu.VMEM((2, _CAST_ROWS, wup.shape[1]), F32),
                        pltpu.VMEM((2, _CAST_ROWS, wdown.shape[1]), F32),
                        pltpu.SemaphoreType.DMA((3, 2))],
        compiler_params=_params("arbitrary"),
        name="channel_mixer",
    )(flat(ys), flat(yg), flat(x), mod, flat(ys), flat(yg), flat(x), mod, mod, *consts)
    return out.reshape(b, s, d)


def _lane_pad(v, width):
    return jnp.pad(v, ((0, 0), (0, width - v.shape[1])))


def _constants():
    q, c = SSD_CHUNK, GLA_CHUNK
    tri_q = np.tril(np.ones((q, q), np.float32))
    tri_c = np.tril(np.ones((c, c), np.float32))
    expand = np.zeros((SMALL_W, SSD_WIDTH), np.float32)
    for h in range(SSD_HEADS):
        expand[h, h * SSD_HEAD_DIM:(h + 1) * SSD_HEAD_DIM] = 1.0
    return (jnp.asarray(np.tile(tri_q, (1, 3)), BF16),
            jnp.asarray(np.tile(tri_c, (1, 3)), BF16),
            jnp.asarray(np.tile(expand, (3, 1)), BF16))


def kernel(x, c, w_ada, b_ada, norm_mix_pre, norm_mix_post, norm_ffn_pre, norm_ffn_post, w_in, ssd_conv_w, ssd_conv_b, ssd_dt_bias, ssd_a_log, ssd_d, ssd_norm, gla_gate_w, gla_gate_b, gla_norm, w_out, ffn_up, ffn_conv_w, ffn_conv_b, ffn_down):
    bsz, seqlen, d = x.shape
    depth = w_ada.shape[0]
    tri_q3, tri_c3, expand3 = _constants()
    c_pad = jnp.pad(c, ((0, SUBLANES - bsz % SUBLANES), (0, 0))) if bsz % SUBLANES else c

    sizes = [SSD_WIDTH, SSD_CONV_DIM, SSD_HEADS, GLA_K_WIDTH, GLA_K_WIDTH, GLA_V_WIDTH,
             GLA_GATE_RANK, GLA_V_WIDTH]
    o = np.concatenate([[0], np.cumsum(sizes)])

    for i in range(depth):
        mod = _ada_call(c_pad, w_ada[i], b_ada[i][None, :])[:bsz].reshape(bsz, ADA_CHUNKS, d)

        wbig, wsmall = _reorder_call(jnp.swapaxes(w_in[i], 0, 1), o)
        ssd_consts = (ssd_conv_w[i], ssd_conv_b[i][None, :],
                      _lane_pad(ssd_dt_bias[i][None, :], SMALL_W), _lane_pad(ssd_a_log[i][None, :], SMALL_W),
                      jnp.repeat(ssd_d[i], SSD_HEAD_DIM)[None, :], ssd_norm[i][None, :], tri_q3, expand3)
        gw = jnp.zeros((SMALL_W, GLA_K_WIDTH), F32).at[SSD_HEADS:SSD_HEADS + GLA_GATE_RANK].set(
            gla_gate_w[i]).astype(BF16)
        gla_consts = (gw, gla_gate_b[i][None, :], gla_norm[i][None, :], tri_c3)
        y_ssd, y_gla = _mixer_call(x, mod, norm_mix_pre[i][None, :], wbig, wsmall, ssd_consts, gla_consts)

        x = _ffn_call(y_ssd, y_gla, x, mod, norm_mix_post[i][None, :], norm_ffn_pre[i][None, :],
                      w_out[i], ffn_up[i], ffn_conv_w[i], ffn_conv_b[i][None, :],
                      norm_ffn_post[i][None, :], ffn_down[i])
    return x
```

```python
import functools

import numpy as np
import jax
import jax.numpy as jnp
from jax import lax
from jax.experimental import pallas as pl
from jax.experimental.pallas import tpu as pltpu

F32 = jnp.float32
BF16 = jnp.bfloat16

D_MODEL = 1024
MIX_WIDTH = 2 * D_MODEL
SSD_WIDTH = MIX_WIDTH // 2
GLA_V_WIDTH = MIX_WIDTH - SSD_WIDTH
SSD_HEAD_DIM = 64
SSD_HEADS = SSD_WIDTH // SSD_HEAD_DIM
SSD_GROUPS = 2
SSD_STATE = 128
SSD_CONV = 4
SSD_CHUNK = 128
SSD_CONV_DIM = SSD_WIDTH + 2 * SSD_GROUPS * SSD_STATE
GLA_HEADS = 4
GLA_K_WIDTH = GLA_V_WIDTH // 2
GLA_HEAD_K = GLA_K_WIDTH // GLA_HEADS
GLA_HEAD_V = GLA_V_WIDTH // GLA_HEADS
GLA_GATE_RANK = 16
GLA_GATE_NORM = 16.0
GLA_CHUNK = 64
FFN_HIDDEN = int(round(8 * D_MODEL / 3 / 128)) * 128
FFN_CONV = 3
ADA_CHUNKS = 6
NORM_EPS = 1e-6
_LOG2E = float(np.log2(np.e))

LANES = 128
SUBLANES = 8
VMEM_LIMIT_BYTES = 58 * 1024 * 1024

SMALL_W = LANES
GROUP_W = SSD_WIDTH // SSD_GROUPS
HEADS_PER_GROUP = SSD_HEADS // SSD_GROUPS
HALO = SUBLANES


def _silu(x):
    return x * jax.nn.sigmoid(x)


def _softplus(x):
    return jnp.maximum(x, 0.0) + jnp.log(1.0 + jnp.exp(-jnp.abs(x)))


def _rms_scale(x):
    return lax.rsqrt(jnp.mean(x * x, axis=-1, keepdims=True) + NORM_EPS)


def _split3(x):
    hi = x.astype(BF16)
    r1 = x - hi.astype(F32)
    mid = r1.astype(BF16)
    lo = (r1 - mid.astype(F32)).astype(BF16)
    return hi, mid, lo


def _dot(a, b):
    return jnp.dot(a, b, preferred_element_type=F32)


def _dot_nt(a, b):
    return lax.dot_general(a, b, (((1,), (1,)), ((), ())), preferred_element_type=F32)


def _dot_tn(a, b):
    return lax.dot_general(a, b, (((0,), (0,)), ((), ())), preferred_element_type=F32)


def _resident(shape):
    nd = len(shape)
    return pl.BlockSpec(shape, lambda *_: (0,) * nd, pipeline_mode=pl.Buffered(1))


def _params(*sem):
    return pltpu.CompilerParams(dimension_semantics=sem, vmem_limit_bytes=VMEM_LIMIT_BYTES)


def _ada_kernel(c_ref, w_ref, b_ref, o_ref):
    ca = _silu(c_ref[...]).astype(BF16)
    o_ref[...] = _dot(ca, w_ref[...].astype(BF16)) + b_ref[...]


def _ada_call(c_pad, w, b):
    rows, d = c_pad.shape
    n = w.shape[1]
    tn = 1024
    return pl.pallas_call(
        _ada_kernel,
        grid=(n // tn,),
        in_specs=[
            pl.BlockSpec((rows, d), lambda j: (0, 0)),
            pl.BlockSpec((d, tn), lambda j: (0, j)),
            pl.BlockSpec((1, tn), lambda j: (0, j)),
        ],
        out_specs=pl.BlockSpec((rows, tn), lambda j: (0, j)),
        out_shape=jax.ShapeDtypeStruct((rows, n), F32),
        compiler_params=_params("arbitrary"),
        name="ada_mod",
    )(c_pad, w, b)


def _reorder_kernel(wt_ref, big_ref, small_ref, *, offs):
    dst = pl.program_id(0) * _PROJ_CHUNK
    skip_dt = offs[3] - offs[2]
    skip_glr = offs[7] - offs[6]
    src = dst + jnp.where(dst >= offs[2], skip_dt, 0) + jnp.where(dst >= offs[6] - skip_dt, skip_glr, 0)
    src = jnp.minimum(src, wt_ref.shape[0] - _PROJ_CHUNK)
    rows = wt_ref[pl.ds(pl.multiple_of(src, 2 * SUBLANES), _PROJ_CHUNK), :]
    big_ref[...] = rows.T.astype(BF16)

    @pl.when(pl.program_id(0) == 0)
    def _():
        d = wt_ref.shape[1]
        pad = jnp.zeros((SMALL_W - skip_dt - skip_glr, d), F32)
        small = jnp.concatenate([wt_ref[offs[2]:offs[3], :], wt_ref[offs[6]:offs[7], :], pad], axis=0)
        small_ref[...] = small.T.astype(BF16)


def _reorder_call(wt, offs):
    n_in, d = wt.shape
    return pl.pallas_call(
        functools.partial(_reorder_kernel, offs=tuple(int(v) for v in offs)),
        grid=(_PROJ_W // _PROJ_CHUNK + 1,),
        in_specs=[_resident((n_in, d))],
        out_specs=[pl.BlockSpec((d, _PROJ_CHUNK), lambda j: (0, j)), pl.BlockSpec((d, SMALL_W), lambda j: (0, 0))],
        out_shape=[jax.ShapeDtypeStruct((d, _PROJ_W + LANES), BF16), jax.ShapeDtypeStruct((d, SMALL_W), BF16)],
        compiler_params=_params("arbitrary"),
        name="reorder_w_in",
    )(wt)


_Z0 = 0
_XBC0 = _Z0 + SSD_WIDTH
_Q0 = _XBC0 + SSD_CONV_DIM
_K0 = _Q0 + GLA_K_WIDTH
_V0 = _K0 + GLA_K_WIDTH
_GO0 = _V0 + GLA_V_WIDTH
_PROJ_W = _GO0 + GLA_V_WIDTH
_PROJ_CHUNK = 512


def _project(hb, w_ref, c0, width):
    parts = [_dot(hb, w_ref[:, c:c + min(_PROJ_CHUNK, c0 + width - c)])
             for c in range(c0, c0 + width, _PROJ_CHUNK)]
    return parts[0] if len(parts) == 1 else jnp.concatenate(parts, axis=1)


def _ssd_stages(env, live, cw_ref, cb_ref, dtb_ref, alog_ref, dexp_ref, nw_ref, tri_ref, exp_ref,
                xpad, state, y_ref):
    q = SSD_CHUNK
    n = SSD_STATE

    xpad[HALO:HALO + q, :] = env["xbc"]
    ext = xpad[...]
    acc = cw_ref[0:1, :] * ext
    for k in range(1, SSD_CONV):
        acc = pltpu.roll(acc, 1, 0) + cw_ref[k:k + 1, :] * ext
    acc = acc[HALO:HALO + q, :] + cb_ref[...]
    xpad[0:HALO, :] = xpad[q:q + HALO, :]
    xc = _silu(acc)
    xs = xc[:, :SSD_WIDTH]
    bm = xc[:, SSD_WIDTH:SSD_WIDTH + SSD_GROUPS * n].astype(BF16)
    cm = xc[:, SSD_WIDTH + SSD_GROUPS * n:].astype(BF16)
    yield

    lane = lax.broadcasted_iota(jnp.int32, (q, SMALL_W), 1)
    head_lane = lane < SSD_HEADS
    dt = _softplus(env["small"] + dtb_ref[...])
    a = -jnp.exp(alog_ref[...]) * _LOG2E
    dt = jnp.where(head_lane, dt, 0.0)
    da = jnp.where(head_lane, dt * a, 0.0)
    cs = _dot(tri_ref[...], jnp.concatenate(_split3(da), axis=0))
    cs_t = cs.T
    cs_e = _dot(jnp.concatenate(_split3(cs), axis=1), exp_ref[...])
    dt_e = _dot(jnp.concatenate(_split3(dt), axis=1), exp_ref[...])
    cs_last = cs_e[q - 1:q, :]

    xdt = xs * dt_e
    xdt_b = xdt.astype(BF16)
    xdec_b = (xdt * jnp.exp2(cs_last - cs_e)).astype(BF16)
    decay_in = jnp.exp2(cs_e)
    decay_chunk = jnp.exp2(cs_last)
    yield

    row = lax.broadcasted_iota(jnp.int32, (q, q), 0)
    col = lax.broadcasted_iota(jnp.int32, (q, q), 1)
    causal = row >= col
    first_half = col < SSD_HEAD_DIM

    y_parts = []
    for g in range(SSD_GROUPS):
        cmg = cm[:, g * n:(g + 1) * n]
        bmg = bm[:, g * n:(g + 1) * n]
        gs = slice(g * GROUP_W, (g + 1) * GROUP_W)
        scores = _dot_nt(cmg, bmg)
        y_off = _dot(cmg, state[g].astype(BF16)) * decay_in[:, gs]
        state[g] = decay_chunk[:, gs] * state[g] + _dot_tn(bmg, xdec_b[:, gs])
        diag = []
        for pair in range(HEADS_PER_GROUP // 2):
            h0 = g * HEADS_PER_GROUP + 2 * pair
            ms = []
            for h in (h0, h0 + 1):
                seg = cs[:, h:h + 1] - cs_t[h:h + 1, :]
                decay = jnp.exp2(jnp.where(causal, seg, -jnp.inf))
                ms.append((scores * decay).astype(BF16))
            slab = xdt_b[:, h0 * SSD_HEAD_DIM:(h0 + 2) * SSD_HEAD_DIM]
            zero = jnp.zeros_like(slab)
            rhs = jnp.concatenate([jnp.where(first_half, slab, zero),
                                   jnp.where(first_half, zero, slab)], axis=0)
            diag.append(_dot(jnp.concatenate(ms, axis=1), rhs))
        y_parts.append(jnp.concatenate(diag, axis=1) + y_off)
        yield
    y = jnp.concatenate(y_parts, axis=1) + xs * dexp_ref[...]
    y = y * _silu(env["z"])
    outs = []
    for g in range(SSD_GROUPS):
        yg = y[:, g * GROUP_W:(g + 1) * GROUP_W]
        outs.append(yg * _rms_scale(yg))
    out = (jnp.concatenate(outs, axis=1) * nw_ref[...]).astype(BF16)
    y_ref[...] = jnp.where(live, out, y_ref[...])
    yield


def _gla_stages(env, r, fresh, gw_ref, gb_ref, nw_ref, tri_ref, state, y_ref):
    c = GLA_CHUNK
    dk, dv = GLA_HEAD_K, GLA_HEAD_V
    row = lax.broadcasted_iota(jnp.int32, (c, c), 0)
    col = lax.broadcasted_iota(jnp.int32, (c, c), 1)
    causal = row >= col

    qf = env["q"][r, :] * (dk ** -0.5)
    kf = env["k"][r, :]
    logit = _dot(env["small"][r, :].astype(BF16), gw_ref[...]) + gb_ref[...]
    lg = -_softplus(-logit) * (_LOG2E / GLA_GATE_NORM)
    bcum = _dot(tri_ref[...], jnp.concatenate(_split3(lg), axis=0))
    blast = bcum[c - 1:c, :]
    qt = (qf * jnp.exp2(bcum)).astype(BF16)
    kt = (kf * jnp.exp2(-bcum)).astype(BF16)
    kd = (kf * jnp.exp2(blast - bcum)).astype(BF16)
    eblast = jnp.exp2(blast)
    yield
    for h in range(GLA_HEADS):
        ks = slice(h * dk, (h + 1) * dk)
        vs = slice(h * dv, (h + 1) * dv)
        attn = jnp.where(causal, _dot_nt(qt[:, ks], kt[:, ks]), 0.0)
        vh = env["v"][r, vs]
        st = jnp.where(fresh, 0.0, state[h])
        o = _dot(attn.astype(BF16), vh) + _dot_nt(qt[:, ks], st.astype(BF16))
        state[h] = eblast[:, ks] * st + _dot_tn(vh, kd[:, ks])
        o = o * _rms_scale(o) * nw_ref[...]
        o = o * _silu(env["go"][r, vs])
        y_ref[:, vs] = o.astype(BF16)
        if h % 2 == 1:
            yield


def _gla_levels(env, rows, gw_ref, gb_ref, nw_ref, tri_ref, state, y_ref):
    c = GLA_CHUNK
    dk, dv = GLA_HEAD_K, GLA_HEAD_V
    row = lax.broadcasted_iota(jnp.int32, (c, c), 0)
    col = lax.broadcasted_iota(jnp.int32, (c, c), 1)
    causal = row >= col
    heads = range(GLA_HEADS)

    logit = _dot(env["small"].astype(BF16), gw_ref[...]) + gb_ref[...]
    lg = -_softplus(-logit) * (_LOG2E / GLA_GATE_NORM)
    yield
    bcum = _dot(tri_ref[...], jnp.concatenate(_split3(lg), axis=0))
    blast = bcum[c - 1:c, :]
    qf = env["q"] * (dk ** -0.5)
    kf = env["k"]
    qt = (qf * jnp.exp2(bcum)).astype(BF16)
    kt = (kf * jnp.exp2(-bcum)).astype(BF16)
    kd = (kf * jnp.exp2(blast - bcum)).astype(BF16)
    eblast = jnp.exp2(blast)
    yield
    ks = [slice(h * dk, (h + 1) * dk) for h in heads]
    vs = [slice(h * dv, (h + 1) * dv) for h in heads]
    attn = [jnp.where(causal, _dot_nt(qt[:, ks[h]], kt[:, ks[h]]), 0.0).astype(BF16) for h in heads]
    yield
    o = [_dot(attn[h], env["v"][:, vs[h]]) + _dot_nt(qt[:, ks[h]], state[h].astype(BF16)) for h in heads]
    for h in heads:
        state[h] = eblast[:, ks[h]] * state[h] + _dot_tn(env["v"][:, vs[h]], kd[:, ks[h]])
    for h in heads:
        oh = o[h] * _rms_scale(o[h]) * nw_ref[...]
        y_ref[rows, vs[h]] = (oh * _silu(env["go"][:, vs[h]])).astype(BF16)
    yield


def _prenorm(x_ref, mod_ref, npre_ref, hb_s):
    x = x_ref[...]
    xn = (x * _rms_scale(x)) * npre_ref[...]
    hb_s[...] = (xn * (1.0 + mod_ref[1:2, :]) + mod_ref[0:1, :]).astype(BF16)


def _mixer_kernel(x0_ref, mod0_ref, xn_ref, modn_ref, npre_ref, wbig_ref, wsmall_ref,
                  cw_ref, cb_ref, dtb_ref, alog_ref, dexp_ref, snw_ref, triq_ref, exp_ref,
                  gw_ref, gb_ref, gnw_ref, tric_ref,
                  ys_ref, yg_ref, xpad, sstate, gstate, hb_s,
                  yg0_s, q_s, k_s, v_s, go_s, small_s, *, tiles_per_seq):
    i = pl.program_id(0)
    fresh = lax.rem(i, tiles_per_seq) == 0
    live = i < pl.num_programs(0) - 1

    @pl.when(i == 0)
    def _():
        _prenorm(x0_ref, mod0_ref, npre_ref, hb_s)
        gstate[...] = jnp.zeros(gstate.shape, F32)
        for ref in (yg0_s, q_s, k_s, v_s, go_s, small_s):
            ref[...] = jnp.zeros(ref.shape, ref.dtype)

    @pl.when(fresh)
    def _():
        xpad[0:HALO, :] = jnp.zeros((HALO, SSD_CONV_DIM), F32)
        sstate[...] = jnp.zeros(sstate.shape, F32)

    hb = hb_s[...]
    first = slice(0, GLA_CHUNK)
    second = slice(GLA_CHUNK, 2 * GLA_CHUNK)
    xbc_cols = [slice(c, c + _PROJ_CHUNK) for c in range(_XBC0, _XBC0 + SSD_CONV_DIM, _PROJ_CHUNK)]
    prev = {"q": q_s[...], "k": k_s[...], "v": v_s[...], "go": go_s[...], "small": small_s[...]}
    gla1 = _gla_levels(prev, second, gw_ref, gb_ref, gnw_ref, tric_ref, gstate, yg_ref)
    yg_ref[first, :] = yg0_s[...]
    next(gla1)
    env = {"small": _dot(hb, wsmall_ref[...])}
    xbc = [_dot(hb, wbig_ref[:, xbc_cols[0]])]
    next(gla1)
    xbc.append(_dot(hb, wbig_ref[:, xbc_cols[1]]))
    next(gla1)
    xbc.append(_dot(hb, wbig_ref[:, xbc_cols[2]]))
    env["xbc"] = jnp.concatenate(xbc, axis=1)
    ssd = _ssd_stages(env, live, cw_ref, cb_ref, dtb_ref, alog_ref, dexp_ref, snw_ref, triq_ref, exp_ref,
                      xpad, sstate, ys_ref)
    gla0 = _gla_stages(env, first, fresh, gw_ref, gb_ref, gnw_ref, tric_ref, gstate, yg0_s)
    env["q"] = _project(hb, wbig_ref, _Q0, GLA_K_WIDTH)
    next(gla1)
    env["k"] = _project(hb, wbig_ref, _K0, GLA_K_WIDTH)
    next(ssd)
    env["v"] = _project(hb, wbig_ref, _V0, GLA_V_WIDTH).astype(BF16)
    next(gla0)
    next(ssd)
    env["go"] = _project(hb, wbig_ref, _GO0, GLA_V_WIDTH)
    next(gla0)
    next(ssd)
    env["z"] = _project(hb, wbig_ref, _Z0, SSD_WIDTH)
    _prenorm(xn_ref, modn_ref, npre_ref, hb_s)
    next(gla0)
    next(ssd)
    next(ssd)
    q_s[...] = env["q"][second, :]
    k_s[...] = env["k"][second, :]
    v_s[...] = env["v"][second, :]
    go_s[...] = env["go"][second, :]
    small_s[...] = env["small"][second, :]


def _mixer_call(x, mod, npre, wbig, wsmall, ssd_consts, gla_consts):
    b, s, d = x.shape
    t = SSD_CHUNK
    assert t == 2 * GLA_CHUNK
    tiles_per_seq = s // t
    n = b * tiles_per_seq
    xf = x.reshape(b * s, d)
    nxt = lambda i: jnp.minimum(i + 1, n - 1)
    cur = lambda i: (jnp.minimum(i, n - 1), 0)
    done = lambda i: (jnp.maximum(i - 1, 0), 0)
    consts = (npre, wbig, wsmall) + tuple(ssd_consts) + tuple(gla_consts)
    ys, yg = pl.pallas_call(
        functools.partial(_mixer_kernel, tiles_per_seq=tiles_per_seq),
        grid=(n + 1,),
        in_specs=[pl.BlockSpec((t, d), lambda i: (0, 0)),
                  pl.BlockSpec((None, ADA_CHUNKS, d), lambda i: (0, 0, 0)),
                  pl.BlockSpec((t, d), lambda i: (nxt(i), 0)),
                  pl.BlockSpec((None, ADA_CHUNKS, d), lambda i: (nxt(i) // tiles_per_seq, 0, 0))]
        + [_resident(a.shape) for a in consts],
        out_specs=[pl.BlockSpec((t, SSD_WIDTH), cur),
                   pl.BlockSpec((t, GLA_V_WIDTH), done)],
        out_shape=[jax.ShapeDtypeStruct((b * s, SSD_WIDTH), BF16),
                   jax.ShapeDtypeStruct((b * s, GLA_V_WIDTH), BF16)],
        scratch_shapes=[
            pltpu.VMEM((HALO + SSD_CHUNK, SSD_CONV_DIM), F32),
            pltpu.VMEM((SSD_GROUPS, SSD_STATE, GROUP_W), F32),
            pltpu.VMEM((GLA_HEADS, GLA_HEAD_V, GLA_HEAD_K), F32),
            pltpu.VMEM((t, d), BF16),
            pltpu.VMEM((GLA_CHUNK, GLA_V_WIDTH), BF16),
            pltpu.VMEM((GLA_CHUNK, GLA_K_WIDTH), F32),
            pltpu.VMEM((GLA_CHUNK, GLA_K_WIDTH), F32),
            pltpu.VMEM((GLA_CHUNK, GLA_V_WIDTH), BF16),
            pltpu.VMEM((GLA_CHUNK, GLA_V_WIDTH), F32),
            pltpu.VMEM((GLA_CHUNK, SMALL_W), F32),
        ],
        compiler_params=_params("arbitrary"),
        name="token_mixer",
    )(xf, mod, xf, mod, *consts)
    return ys.reshape(b, s, SSD_WIDTH), yg.reshape(b, s, GLA_V_WIDTH)


_UP_CHUNK = 384
_UP_DOT = 768
_FFN_TM = 256
_CAST_ROWS = 128


def _cast_weight(w_hbm, dst, stage, sem):
    rows = stage.shape[1]
    n = w_hbm.shape[0] // rows

    def copy(c):
        return pltpu.make_async_copy(w_hbm.at[pl.ds(c * rows, rows), :], stage.at[c % 2], sem.at[c % 2])

    copy(0).start()
    for c in range(n):
        if c + 1 < n:
            copy(c + 1).start()
        copy(c).wait()
        dst[c * rows:(c + 1) * rows, :] = stage[c % 2].astype(BF16)


def _stage_ffn_input(ys_ref, yg_ref, x_ref, mod_ref, npost1_ref, npre_ref, wout_ref, x1_s, hb_s):
    y = _dot(ys_ref[...], wout_ref[0:SSD_WIDTH, :]) + _dot(yg_ref[...], wout_ref[SSD_WIDTH:, :])
    yn = (y * _rms_scale(y)) * npost1_ref[...]
    x1 = x_ref[...] + mod_ref[2:3, :] * yn
    x1_s[...] = x1
    h = ((x1 * _rms_scale(x1)) * npre_ref[...]) * (1.0 + mod_ref[4:5, :]) + mod_ref[3:4, :]
    hb_s[...] = h.astype(BF16)


def _ffn_kernel(ys0_ref, yg0_ref, x0_ref, mod0_ref, ysn_ref, ygn_ref, xn_ref, modn_ref, mod_ref,
                npost1_ref, npre_ref, wout_hbm, wup_hbm, cw_ref, cb_ref, npost2_ref, wdown_hbm,
                o_ref, upad, gate, x1_s, hb_s,
                wout_ref, wup_ref, wdown_ref, st_out, st_up, st_down, sem, *, tiles_per_seq):
    tm = o_ref.shape[0]
    i = pl.program_id(0)

    @pl.when(i == 0)
    def _():
        _cast_weight(wout_hbm, wout_ref, st_out, sem.at[0])
        _cast_weight(wup_hbm, wup_ref, st_up, sem.at[1])
        _cast_weight(wdown_hbm, wdown_ref, st_down, sem.at[2])
        _stage_ffn_input(ys0_ref, yg0_ref, x0_ref, mod0_ref, npost1_ref, npre_ref, wout_ref, x1_s, hb_s)

    @pl.when(lax.rem(i, tiles_per_seq) == 0)
    def _():
        upad[0:HALO, :] = jnp.zeros((HALO, upad.shape[1]), F32)

    x1 = x1_s[...]
    hb = hb_s[...]

    done = set()

    def up(col):
        p = col // _UP_DOT
        if p not in done:
            done.add(p)
            cols = slice(p * _UP_DOT, (p + 1) * _UP_DOT)
            upad[HALO:HALO + tm, cols] = _dot(hb, wup_ref[:, cols])

    def conv(c0):
        cols = slice(c0, c0 + _UP_CHUNK)
        up(c0)
        up(c0 + _UP_CHUNK - 1)
        ext = upad[:, cols]
        acc = cw_ref[0:1, cols] * ext
        for k in range(1, FFN_CONV):
            acc = pltpu.roll(acc, 1, 0) + cw_ref[k:k + 1, cols] * ext
        upad[0:HALO, cols] = upad[tm:tm + HALO, cols]
        return acc[HALO:HALO + tm, :] + cb_ref[:, cols]

    for c in range(0, FFN_HIDDEN, _UP_CHUNK):
        gate[:, c:c + _UP_CHUNK] = (_silu(conv(c)) * conv(FFN_HIDDEN + c)).astype(BF16)
    _stage_ffn_input(ysn_ref, ygn_ref, xn_ref, modn_ref, npost1_ref, npre_ref, wout_ref, x1_s, hb_s)
    f = _dot(gate[...], wdown_ref[...])
    fn = (f * _rms_scale(f)) * npost2_ref[...]
    o_ref[...] = x1 + mod_ref[5:6, :] * fn


def _ffn_call(ys, yg, x, mod, npost1, npre, wout, wup, cw, cb, npost2, wdown):
    b, s, d = x.shape
    tm = _FFN_TM
    tiles_per_seq = s // tm
    n = b * tiles_per_seq
    flat = lambda a: a.reshape(b * s, a.shape[-1])
    nxt = lambda i: jnp.minimum(i + 1, n - 1)
    first = lambda w: pl.BlockSpec((tm, w), lambda i: (0, 0))
    ahead = lambda w: pl.BlockSpec((tm, w), lambda i: (nxt(i), 0))
    mod_of = lambda tile: pl.BlockSpec((None, ADA_CHUNKS, d), lambda i: (tile(i) // tiles_per_seq, 0, 0))
    consts = (npost1, npre, wout, wup, cw, cb, npost2, wdown)
    in_hbm = pl.BlockSpec(memory_space=pl.ANY)
    out = pl.pallas_call(
        functools.partial(_ffn_kernel, tiles_per_seq=tiles_per_seq),
        grid=(n,),
        in_specs=[first(SSD_WIDTH), first(GLA_V_WIDTH), first(d), mod_of(lambda i: 0 * i),
                  ahead(SSD_WIDTH), ahead(GLA_V_WIDTH), ahead(d), mod_of(nxt), mod_of(lambda i: i)]
        + [in_hbm if any(a is w for w in (wout, wup, wdown)) else _resident(a.shape) for a in consts],
        out_specs=pl.BlockSpec((tm, d), lambda i: (i, 0)),
        out_shape=jax.ShapeDtypeStruct((b * s, d), F32),
        scratch_shapes=[pltpu.VMEM((HALO + tm, 2 * FFN_HIDDEN), F32),
                        pltpu.VMEM((tm, FFN_HIDDEN), BF16),
                        pltpu.VMEM((tm, d), F32), pltpu.VMEM((tm, d), BF16),
                        pltpu.VMEM(wout.shape, BF16), pltpu.VMEM(wup.shape, BF16), pltpu.VMEM(wdown.shape, BF16),
                        pltpu.VMEM((2, _CAST_ROWS, wout.shape[1]), F32),
                        pltpu.VMEM((2, _CAST_ROWS, wup.shape[1]), F32),
                        pltpu.VMEM((2, _CAST_ROWS, wdown.shape[1]), F32),
                        pltpu.SemaphoreType.DMA((3, 2))],
        compiler_params=_params("arbitrary"),
        name="channel_mixer",
    )(flat(ys), flat(yg), flat(x), mod, flat(ys), flat(yg), flat(x), mod, mod, *consts)
    return out.reshape(b, s, d)


def _lane_pad(v, width):
    return jnp.pad(v, ((0, 0), (0, width - v.shape[1])))


def _constants():
    q, c = SSD_CHUNK, GLA_CHUNK
    tri_q = np.tril(np.ones((q, q), np.float32))
    tri_c = np.tril(np.ones((c, c), np.float32))
    expand = np.zeros((SMALL_W, SSD_WIDTH), np.float32)
    for h in range(SSD_HEADS):
        expand[h, h * SSD_HEAD_DIM:(h + 1) * SSD_HEAD_DIM] = 1.0
    return (jnp.asarray(np.tile(tri_q, (1, 3)), BF16),
            jnp.asarray(np.tile(tri_c, (1, 3)), BF16),
            jnp.asarray(np.tile(expand, (3, 1)), BF16))


def kernel(x, c, w_ada, b_ada, norm_mix_pre, norm_mix_post, norm_ffn_pre, norm_ffn_post, w_in, ssd_conv_w, ssd_conv_b, ssd_dt_bias, ssd_a_log, ssd_d, ssd_norm, gla_gate_w, gla_gate_b, gla_norm, w_out, ffn_up, ffn_conv_w, ffn_conv_b, ffn_down):
    bsz, seqlen, d = x.shape
    depth = w_ada.shape[0]
    tri_q3, tri_c3, expand3 = _constants()
    c_pad = jnp.pad(c, ((0, SUBLANES - bsz % SUBLANES), (0, 0))) if bsz % SUBLANES else c

    sizes = [SSD_WIDTH, SSD_CONV_DIM, SSD_HEADS, GLA_K_WIDTH, GLA_K_WIDTH, GLA_V_WIDTH,
             GLA_GATE_RANK, GLA_V_WIDTH]
    o = np.concatenate([[0], np.cumsum(sizes)])

    for i in range(depth):
        mod = _ada_call(c_pad, w_ada[i], b_ada[i][None, :])[:bsz].reshape(bsz, ADA_CHUNKS, d)

        wbig, wsmall = _reorder_call(jnp.swapaxes(w_in[i], 0, 1), o)
        ssd_consts = (ssd_conv_w[i], ssd_conv_b[i][None, :],
                      _lane_pad(ssd_dt_bias[i][None, :], SMALL_W), _lane_pad(ssd_a_log[i][None, :], SMALL_W),
                      jnp.repeat(ssd_d[i], SSD_HEAD_DIM)[None, :], ssd_norm[i][None, :], tri_q3, expand3)
        gw = jnp.zeros((SMALL_W, GLA_K_WIDTH), F32).at[SSD_HEADS:SSD_HEADS + GLA_GATE_RANK].set(
            gla_gate_w[i]).astype(BF16)
        gla_consts = (gw, gla_gate_b[i][None, :], gla_norm[i][None, :], tri_c3)
        y_ssd, y_gla = _mixer_call(x, mod, norm_mix_pre[i][None, :], wbig, wsmall, ssd_consts, gla_consts)

        x = _ffn_call(y_ssd, y_gla, x, mod, norm_mix_post[i][None, :], norm_ffn_pre[i][None, :],
                      w_out[i], ffn_up[i], ffn_conv_w[i], ffn_conv_b[i][None, :],
                      norm_ffn_post[i][None, :], ffn_down[i])
    return x
```

```python
import functools

import numpy as np
import jax
import jax.numpy as jnp
from jax import lax
from jax.experimental import pallas as pl
from jax.experimental.pallas import tpu as pltpu

F32 = jnp.float32
BF16 = jnp.bfloat16

D_MODEL = 1024
MIX_WIDTH = 2 * D_MODEL
SSD_WIDTH = MIX_WIDTH // 2
GLA_V_WIDTH = MIX_WIDTH - SSD_WIDTH
SSD_HEAD_DIM = 64
SSD_HEADS = SSD_WIDTH // SSD_HEAD_DIM
SSD_GROUPS = 2
SSD_STATE = 128
SSD_CONV = 4
SSD_CHUNK = 128
SSD_CONV_DIM = SSD_WIDTH + 2 * SSD_GROUPS * SSD_STATE
GLA_HEADS = 4
GLA_K_WIDTH = GLA_V_WIDTH // 2
GLA_HEAD_K = GLA_K_WIDTH // GLA_HEADS
GLA_HEAD_V = GLA_V_WIDTH // GLA_HEADS
GLA_GATE_RANK = 16
GLA_GATE_NORM = 16.0
GLA_CHUNK = 64
FFN_HIDDEN = int(round(8 * D_MODEL / 3 / 128)) * 128
FFN_CONV = 3
ADA_CHUNKS = 6
NORM_EPS = 1e-6
_LOG2E = float(np.log2(np.e))

LANES = 128
SUBLANES = 8
VMEM_LIMIT_BYTES = 58 * 1024 * 1024

SMALL_W = LANES
GROUP_W = SSD_WIDTH // SSD_GROUPS
HEADS_PER_GROUP = SSD_HEADS // SSD_GROUPS
HALO = SUBLANES


def _silu(x):
    return x * jax.nn.sigmoid(x)


def _softplus(x):
    return jnp.maximum(x, 0.0) + jnp.log(1.0 + jnp.exp(-jnp.abs(x)))


def _rms_scale(x):
    return lax.rsqrt(jnp.mean(x * x, axis=-1, keepdims=True) + NORM_EPS)


def _split3(x):
    hi = x.astype(BF16)
    r1 = x - hi.astype(F32)
    mid = r1.astype(BF16)
    lo = (r1 - mid.astype(F32)).astype(BF16)
    return hi, mid, lo


def _dot(a, b):
    return jnp.dot(a, b, preferred_element_type=F32)


def _dot_nt(a, b):
    return lax.dot_general(a, b, (((1,), (1,)), ((), ())), preferred_element_type=F32)


def _dot_tn(a, b):
    return lax.dot_general(a, b, (((0,), (0,)), ((), ())), preferred_element_type=F32)


def _resident(shape):
    nd = len(shape)
    return pl.BlockSpec(shape, lambda *_: (0,) * nd, pipeline_mode=pl.Buffered(1))


def _params(*sem):
    return pltpu.CompilerParams(dimension_semantics=sem, vmem_limit_bytes=VMEM_LIMIT_BYTES)


def _ada_kernel(c_ref, w_ref, b_ref, o_ref):
    ca = _silu(c_ref[...]).astype(BF16)
    o_ref[...] = _dot(ca, w_ref[...].astype(BF16)) + b_ref[...]


def _ada_call(c_pad, w, b):
    rows, d = c_pad.shape
    n = w.shape[1]
    tn = 1024
    return pl.pallas_call(
        _ada_kernel,
        grid=(n // tn,),
        in_specs=[
            pl.BlockSpec((rows, d), lambda j: (0, 0)),
            pl.BlockSpec((d, tn), lambda j: (0, j)),
            pl.BlockSpec((1, tn), lambda j: (0, j)),
        ],
        out_specs=pl.BlockSpec((rows, tn), lambda j: (0, j)),
        out_shape=jax.ShapeDtypeStruct((rows, n), F32),
        compiler_params=_params("arbitrary"),
        name="ada_mod",
    )(c_pad, w, b)


def _reorder_kernel(wt_ref, big_ref, small_ref, *, offs):
    dst = pl.program_id(0) * _PROJ_CHUNK
    skip_dt = offs[3] - offs[2]
    skip_glr = offs[7] - offs[6]
    src = dst + jnp.where(dst >= offs[2], skip_dt, 0) + jnp.where(dst >= offs[6] - skip_dt, skip_glr, 0)
    src = jnp.minimum(src, wt_ref.shape[0] - _PROJ_CHUNK)
    rows = wt_ref[pl.ds(pl.multiple_of(src, 2 * SUBLANES), _PROJ_CHUNK), :]
    big_ref[...] = rows.T.astype(BF16)

    @pl.when(pl.program_id(0) == 0)
    def _():
        d = wt_ref.shape[1]
        pad = jnp.zeros((SMALL_W - skip_dt - skip_glr, d), F32)
        small = jnp.concatenate([wt_ref[offs[2]:offs[3], :], wt_ref[offs[6]:offs[7], :], pad], axis=0)
        small_ref[...] = small.T.astype(BF16)


def _reorder_call(wt, offs):
    n_in, d = wt.shape
    return pl.pallas_call(
        functools.partial(_reorder_kernel, offs=tuple(int(v) for v in offs)),
        grid=(_PROJ_W // _PROJ_CHUNK + 1,),
        in_specs=[_resident((n_in, d))],
        out_specs=[pl.BlockSpec((d, _PROJ_CHUNK), lambda j: (0, j)), pl.BlockSpec((d, SMALL_W), lambda j: (0, 0))],
        out_shape=[jax.ShapeDtypeStruct((d, _PROJ_W + LANES), BF16), jax.ShapeDtypeStruct((d, SMALL_W), BF16)],
        compiler_params=_params("arbitrary"),
        name="reorder_w_in",
    )(wt)


_Z0 = 0
_XBC0 = _Z0 + SSD_WIDTH
_Q0 = _XBC0 + SSD_CONV_DIM
_K0 = _Q0 + GLA_K_WIDTH
_V0 = _K0 + GLA_K_WIDTH
_GO0 = _V0 + GLA_V_WIDTH
_PROJ_W = _GO0 + GLA_V_WIDTH
_PROJ_CHUNK = 512


def _project(hb, w_ref, c0, width):
    parts = [_dot(hb, w_ref[:, c:c + min(_PROJ_CHUNK, c0 + width - c)])
             for c in range(c0, c0 + width, _PROJ_CHUNK)]
    return parts[0] if len(parts) == 1 else jnp.concatenate(parts, axis=1)


def _ssd_stages(env, live, cw_ref, cb_ref, dtb_ref, alog_ref, dexp_ref, nw_ref, tri_ref, exp_ref,
                xpad, state, y_ref):
    q = SSD_CHUNK
    n = SSD_STATE

    xpad[HALO:HALO + q, :] = env["xbc"]
    ext = xpad[...]
    acc = cw_ref[0:1, :] * ext
    for k in range(1, SSD_CONV):
        acc = pltpu.roll(acc, 1, 0) + cw_ref[k:k + 1, :] * ext
    acc = acc[HALO:HALO + q, :] + cb_ref[...]
    xpad[0:HALO, :] = xpad[q:q + HALO, :]
    xc = _silu(acc)
    xs = xc[:, :SSD_WIDTH]
    bm = xc[:, SSD_WIDTH:SSD_WIDTH + SSD_GROUPS * n].astype(BF16)
    cm = xc[:, SSD_WIDTH + SSD_GROUPS * n:].astype(BF16)
    yield

    lane = lax.broadcasted_iota(jnp.int32, (q, SMALL_W), 1)
    head_lane = lane < SSD_HEADS
    dt = _softplus(env["small"] + dtb_ref[...])
    a = -jnp.exp(alog_ref[...]) * _LOG2E
    dt = jnp.where(head_lane, dt, 0.0)
    da = jnp.where(head_lane, dt * a, 0.0)
    cs = _dot(tri_ref[...], jnp.concatenate(_split3(da), axis=0))
    cs_t = cs.T
    cs_e = _dot(jnp.concatenate(_split3(cs), axis=1), exp_ref[...])
    dt_e = _dot(jnp.concatenate(_split3(dt), axis=1), exp_ref[...])
    cs_last = cs_e[q - 1:q, :]

    xdt = xs * dt_e
    xdt_b = xdt.astype(BF16)
    xdec_b = (xdt * jnp.exp2(cs_last - cs_e)).astype(BF16)
    decay_in = jnp.exp2(cs_e)
    decay_chunk = jnp.exp2(cs_last)
    yield

    row = lax.broadcasted_iota(jnp.int32, (q, q), 0)
    col = lax.broadcasted_iota(jnp.int32, (q, q), 1)
    causal = row >= col
    first_half = col < SSD_HEAD_DIM

    y_parts = []
    for g in range(SSD_GROUPS):
        cmg = cm[:, g * n:(g + 1) * n]
        bmg = bm[:, g * n:(g + 1) * n]
        gs = slice(g * GROUP_W, (g + 1) * GROUP_W)
        scores = _dot_nt(cmg, bmg)
        y_off = _dot(cmg, state[g].astype(BF16)) * decay_in[:, gs]
        state[g] = decay_chunk[:, gs] * state[g] + _dot_tn(bmg, xdec_b[:, gs])
        diag = []
        for pair in range(HEADS_PER_GROUP // 2):
            h0 = g * HEADS_PER_GROUP + 2 * pair
            ms = []
            for h in (h0, h0 + 1):
                seg = cs[:, h:h + 1] - cs_t[h:h + 1, :]
                decay = jnp.exp2(jnp.where(causal, seg, -jnp.inf))
                ms.append((scores * decay).astype(BF16))
            slab = xdt_b[:, h0 * SSD_HEAD_DIM:(h0 + 2) * SSD_HEAD_DIM]
            zero = jnp.zeros_like(slab)
            rhs = jnp.concatenate([jnp.where(first_half, slab, zero),
                                   jnp.where(first_half, zero, slab)], axis=0)
            diag.append(_dot(jnp.concatenate(ms, axis=1), rhs))
        y_parts.append(jnp.concatenate(diag, axis=1) + y_off)
        yield
    y = jnp.concatenate(y_parts, axis=1) + xs * dexp_ref[...]
    y = y * _silu(env["z"])
    outs = []
    for g in range(SSD_GROUPS):
        yg = y[:, g * GROUP_W:(g + 1) * GROUP_W]
        outs.append(yg * _rms_scale(yg))
    out = (jnp.concatenate(outs, axis=1) * nw_ref[...]).astype(BF16)
    y_ref[...] = jnp.where(live, out, y_ref[...])
    yield


def _gla_stages(env, r, fresh, gw_ref, gb_ref, nw_ref, tri_ref, state, y_ref):
    c = GLA_CHUNK
    dk, dv = GLA_HEAD_K, GLA_HEAD_V
    row = lax.broadcasted_iota(jnp.int32, (c, c), 0)
    col = lax.broadcasted_iota(jnp.int32, (c, c), 1)
    causal = row >= col

    qf = env["q"][r, :] * (dk ** -0.5)
    kf = env["k"][r, :]
    logit = _dot(env["small"][r, :].astype(BF16), gw_ref[...]) + gb_ref[...]
    lg = -_softplus(-logit) * (_LOG2E / GLA_GATE_NORM)
    bcum = _dot(tri_ref[...], jnp.concatenate(_split3(lg), axis=0))
    blast = bcum[c - 1:c, :]
    qt = (qf * jnp.exp2(bcum)).astype(BF16)
    kt = (kf * jnp.exp2(-bcum)).astype(BF16)
    kd = (kf * jnp.exp2(blast - bcum)).astype(BF16)
    eblast = jnp.exp2(blast)
    yield
    for h in range(GLA_HEADS):
        ks = slice(h * dk, (h + 1) * dk)
        vs = slice(h * dv, (h + 1) * dv)
        attn = jnp.where(causal, _dot_nt(qt[:, ks], kt[:, ks]), 0.0)
        vh = env["v"][r, vs]
        st = jnp.where(fresh, 0.0, state[h])
        o = _dot(attn.astype(BF16), vh) + _dot_nt(qt[:, ks], st.astype(BF16))
        state[h] = eblast[:, ks] * st + _dot_tn(vh, kd[:, ks])
        o = o * _rms_scale(o) * nw_ref[...]
        o = o * _silu(env["go"][r, vs])
        y_ref[:, vs] = o.astype(BF16)
        if h % 2 == 1:
            yield


def _gla_levels(env, rows, gw_ref, gb_ref, nw_ref, tri_ref, state, y_ref):
    c = GLA_CHUNK
    dk, dv = GLA_HEAD_K, GLA_HEAD_V
    row = lax.broadcasted_iota(jnp.int32, (c, c), 0)
    col = lax.broadcasted_iota(jnp.int32, (c, c), 1)
    causal = row >= col
    heads = range(GLA_HEADS)

    logit = _dot(env["small"].astype(BF16), gw_ref[...]) + gb_ref[...]
    lg = -_softplus(-logit) * (_LOG2E / GLA_GATE_NORM)
    yield
    bcum = _dot(tri_ref[...], jnp.concatenate(_split3(lg), axis=0))
    blast = bcum[c - 1:c, :]
    qf = env["q"] * (dk ** -0.5)
    kf = env["k"]
    qt = (qf * jnp.exp2(bcum)).astype(BF16)
    kt = (kf * jnp.exp2(-bcum)).astype(BF16)
    kd = (kf * jnp.exp2(blast - bcum)).astype(BF16)
    eblast = jnp.exp2(blast)
    yield
    ks = [slice(h * dk, (h + 1) * dk) for h in heads]
    vs = [slice(h * dv, (h + 1) * dv) for h in heads]
    attn = [jnp.where(causal, _dot_nt(qt[:, ks[h]], kt[:, ks[h]]), 0.0).astype(BF16) for h in heads]
    yield
    o = [_dot(attn[h], env["v"][:, vs[h]]) + _dot_nt(qt[:, ks[h]], state[h].astype(BF16)) for h in heads]
    for h in heads:
        state[h] = eblast[:, ks[h]] * state[h] + _dot_tn(env["v"][:, vs[h]], kd[:, ks[h]])
    for h in heads:
        oh = o[h] * _rms_scale(o[h]) * nw_ref[...]
        y_ref[rows, vs[h]] = (oh * _silu(env["go"][:, vs[h]])).astype(BF16)
    yield


def _prenorm(x_ref, mod_ref, npre_ref, hb_s):
    x = x_ref[...]
    xn = (x * _rms_scale(x)) * npre_ref[...]
    hb_s[...] = (xn * (1.0 + mod_ref[1:2, :]) + mod_ref[0:1, :]).astype(BF16)


def _mixer_kernel(x0_ref, mod0_ref, xn_ref, modn_ref, npre_ref, wbig_ref, wsmall_ref,
                  cw_ref, cb_ref, dtb_ref, alog_ref, dexp_ref, snw_ref, triq_ref, exp_ref,
                  gw_ref, gb_ref, gnw_ref, tric_ref,
                  ys_ref, yg_ref, xpad, sstate, gstate, hb_s,
                  yg0_s, q_s, k_s, v_s, go_s, small_s, *, tiles_per_seq):
    i = pl.program_id(0)
    fresh = lax.rem(i, tiles_per_seq) == 0
    live = i < pl.num_programs(0) - 1

    @pl.when(i == 0)
    def _():
        _prenorm(x0_ref, mod0_ref, npre_ref, hb_s)
        gstate[...] = jnp.zeros(gstate.shape, F32)
        for ref in (yg0_s, q_s, k_s, v_s, go_s, small_s):
            ref[...] = jnp.zeros(ref.shape, ref.dtype)

    @pl.when(fresh)
    def _():
        xpad[0:HALO, :] = jnp.zeros((HALO, SSD_CONV_DIM), F32)
        sstate[...] = jnp.zeros(sstate.shape, F32)

    hb = hb_s[...]
    first = slice(0, GLA_CHUNK)
    second = slice(GLA_CHUNK, 2 * GLA_CHUNK)
    xbc_cols = [slice(c, c + _PROJ_CHUNK) for c in range(_XBC0, _XBC0 + SSD_CONV_DIM, _PROJ_CHUNK)]
    prev = {"q": q_s[...], "k": k_s[...], "v": v_s[...], "go": go_s[...], "small": small_s[...]}
    gla1 = _gla_levels(prev, second, gw_ref, gb_ref, gnw_ref, tric_ref, gstate, yg_ref)
    yg_ref[first, :] = yg0_s[...]
    next(gla1)
    env = {"small": _dot(hb, wsmall_ref[...])}
    xbc = [_dot(hb, wbig_ref[:, xbc_cols[0]])]
    next(gla1)
    xbc.append(_dot(hb, wbig_ref[:, xbc_cols[1]]))
    next(gla1)
    xbc.append(_dot(hb, wbig_ref[:, xbc_cols[2]]))
    env["xbc"] = jnp.concatenate(xbc, axis=1)
    ssd = _ssd_stages(env, live, cw_ref, cb_ref, dtb_ref, alog_ref, dexp_ref, snw_ref, triq_ref, exp_ref,
                      xpad, sstate, ys_ref)
    gla0 = _gla_stages(env, first, fresh, gw_ref, gb_ref, gnw_ref, tric_ref, gstate, yg0_s)
    env["q"] = _project(hb, wbig_ref, _Q0, GLA_K_WIDTH)
    next(gla1)
    env["k"] = _project(hb, wbig_ref, _K0, GLA_K_WIDTH)
    next(ssd)
    env["v"] = _project(hb, wbig_ref, _V0, GLA_V_WIDTH).astype(BF16)
    next(gla0)
    next(ssd)
    env["go"] = _project(hb, wbig_ref, _GO0, GLA_V_WIDTH)
    next(gla0)
    next(ssd)
    env["z"] = _project(hb, wbig_ref, _Z0, SSD_WIDTH)
    _prenorm(xn_ref, modn_ref, npre_ref, hb_s)
    next(gla0)
    next(ssd)
    next(ssd)
    q_s[...] = env["q"][second, :]
    k_s[...] = env["k"][second, :]
    v_s[...] = env["v"][second, :]
    go_s[...] = env["go"][second, :]
    small_s[...] = env["small"][second, :]


def _mixer_call(x, mod, npre, wbig, wsmall, ssd_consts, gla_consts):
    b, s, d = x.shape
    t = SSD_CHUNK
    assert t == 2 * GLA_CHUNK
    tiles_per_seq = s // t
    n = b * tiles_per_seq
    xf = x.reshape(b * s, d)
    nxt = lambda i: jnp.minimum(i + 1, n - 1)
    cur = lambda i: (jnp.minimum(i, n - 1), 0)
    done = lambda i: (jnp.maximum(i - 1, 0), 0)
    consts = (npre, wbig, wsmall) + tuple(ssd_consts) + tuple(gla_consts)
    ys, yg = pl.pallas_call(
        functools.partial(_mixer_kernel, tiles_per_seq=tiles_per_seq),
        grid=(n + 1,),
        in_specs=[pl.BlockSpec((t, d), lambda i: (0, 0)),
                  pl.BlockSpec((None, ADA_CHUNKS, d), lambda i: (0, 0, 0)),
                  pl.BlockSpec((t, d), lambda i: (nxt(i), 0)),
                  pl.BlockSpec((None, ADA_CHUNKS, d), lambda i: (nxt(i) // tiles_per_seq, 0, 0))]
        + [_resident(a.shape) for a in consts],
        out_specs=[pl.BlockSpec((t, SSD_WIDTH), cur),
                   pl.BlockSpec((t, GLA_V_WIDTH), done)],
        out_shape=[jax.ShapeDtypeStruct((b * s, SSD_WIDTH), BF16),
                   jax.ShapeDtypeStruct((b * s, GLA_V_WIDTH), BF16)],
        scratch_shapes=[
            pltpu.VMEM((HALO + SSD_CHUNK, SSD_CONV_DIM), F32),
            pltpu.VMEM((SSD_GROUPS, SSD_STATE, GROUP_W), F32),
            pltpu.VMEM((GLA_HEADS, GLA_HEAD_V, GLA_HEAD_K), F32),
            pltpu.VMEM((t, d), BF16),
            pltpu.VMEM((GLA_CHUNK, GLA_V_WIDTH), BF16),
            pltpu.VMEM((GLA_CHUNK, GLA_K_WIDTH), F32),
            pltpu.VMEM((GLA_CHUNK, GLA_K_WIDTH), F32),
            pltpu.VMEM((GLA_CHUNK, GLA_V_WIDTH), BF16),
            pltpu.VMEM((GLA_CHUNK, GLA_V_WIDTH), F32),
            pltpu.VMEM((GLA_CHUNK, SMALL_W), F32),
        ],
        compiler_params=_params("arbitrary"),
        name="token_mixer",
    )(xf, mod, xf, mod, *consts)
    return ys.reshape(b, s, SSD_WIDTH), yg.reshape(b, s, GLA_V_WIDTH)


_UP_CHUNK = 384
_UP_DOT = 768
_FFN_TM = 256
_CAST_ROWS = 128


def _cast_weight(w_hbm, dst, stage, sem):
    rows = stage.shape[1]
    n = w_hbm.shape[0] // rows

    def copy(c):
        return pltpu.make_async_copy(w_hbm.at[pl.ds(c * rows, rows), :], stage.at[c % 2], sem.at[c % 2])

    copy(0).start()
    for c in range(n):
        if c + 1 < n:
            copy(c + 1).start()
        yield
        copy(c).wait()
        dst[c * rows:(c + 1) * rows, :] = stage[c % 2].astype(BF16)


def _stage_ffn_input(ys_ref, yg_ref, x_ref, mod_ref, npost1_ref, npre_ref, wout_ref, x1_s, hb_s):
    y = _dot(ys_ref[...], wout_ref[0:SSD_WIDTH, :]) + _dot(yg_ref[...], wout_ref[SSD_WIDTH:, :])
    yn = (y * _rms_scale(y)) * npost1_ref[...]
    x1 = x_ref[...] + mod_ref[2:3, :] * yn
    x1_s[...] = x1
    h = ((x1 * _rms_scale(x1)) * npre_ref[...]) * (1.0 + mod_ref[4:5, :]) + mod_ref[3:4, :]
    hb_s[...] = h.astype(BF16)


def _ffn_kernel(ys0_ref, yg0_ref, x0_ref, mod0_ref, ysn_ref, ygn_ref, xn_ref, modn_ref, mod_ref,
                npost1_ref, npre_ref, wout_hbm, wup_hbm, cw_ref, cb_ref, npost2_ref, wdown_hbm,
                o_ref, upad, gate, x1_s, hb_s,
                wout_ref, wup_ref, wdown_ref, st_out, st_up, st_down, sem, *, tiles_per_seq):
    tm = o_ref.shape[0]
    i = pl.program_id(0)

    @pl.when(i == 0)
    def _():
        casts = [_cast_weight(wout_hbm, wout_ref, st_out, sem.at[0]),
                 _cast_weight(wup_hbm, wup_ref, st_up, sem.at[1]),
                 _cast_weight(wdown_hbm, wdown_ref, st_down, sem.at[2])]
        while casts:
            casts = [g for g in casts if next(g, True) is None]
        _stage_ffn_input(ys0_ref, yg0_ref, x0_ref, mod0_ref, npost1_ref, npre_ref, wout_ref, x1_s, hb_s)

    @pl.when(lax.rem(i, tiles_per_seq) == 0)
    def _():
        upad[0:HALO, :] = jnp.zeros((HALO, upad.shape[1]), F32)

    x1 = x1_s[...]
    hb = hb_s[...]

    done = set()

    def up(col):
        p = col // _UP_DOT
        if p not in done:
            done.add(p)
            cols = slice(p * _UP_DOT, (p + 1) * _UP_DOT)
            upad[HALO:HALO + tm, cols] = _dot(hb, wup_ref[:, cols])

    def conv(c0):
        cols = slice(c0, c0 + _UP_CHUNK)
        up(c0)
        up(c0 + _UP_CHUNK - 1)
        ext = upad[:, cols]
        acc = cw_ref[0:1, cols] * ext
        for k in range(1, FFN_CONV):
            acc = pltpu.roll(acc, 1, 0) + cw_ref[k:k + 1, cols] * ext
        upad[0:HALO, cols] = upad[tm:tm + HALO, cols]
        return acc[HALO:HALO + tm, :] + cb_ref[:, cols]

    for c in range(0, FFN_HIDDEN, _UP_CHUNK):
        gate[:, c:c + _UP_CHUNK] = (_silu(conv(c)) * conv(FFN_HIDDEN + c)).astype(BF16)
    _stage_ffn_input(ysn_ref, ygn_ref, xn_ref, modn_ref, npost1_ref, npre_ref, wout_ref, x1_s, hb_s)
    f = _dot(gate[...], wdown_ref[...])
    fn = (f * _rms_scale(f)) * npost2_ref[...]
    o_ref[...] = x1 + mod_ref[5:6, :] * fn


def _ffn_call(ys, yg, x, mod, npost1, npre, wout, wup, cw, cb, npost2, wdown):
    b, s, d = x.shape
    tm = _FFN_TM
    tiles_per_seq = s // tm
    n = b * tiles_per_seq
    flat = lambda a: a.reshape(b * s, a.shape[-1])
    nxt = lambda i: jnp.minimum(i + 1, n - 1)
    first = lambda w: pl.BlockSpec((tm, w), lambda i: (0, 0))
    ahead = lambda w: pl.BlockSpec((tm, w), lambda i: (nxt(i), 0))
    mod_of = lambda tile: pl.BlockSpec((None, ADA_CHUNKS, d), lambda i: (tile(i) // tiles_per_seq, 0, 0))
    consts = (npost1, npre, wout, wup, cw, cb, npost2, wdown)
    in_hbm = pl.BlockSpec(memory_space=pl.ANY)
    out = pl.pallas_call(
        functools.partial(_ffn_kernel, tiles_per_seq=tiles_per_seq),
        grid=(n,),
        in_specs=[first(SSD_WIDTH), first(GLA_V_WIDTH), first(d), mod_of(lambda i: 0 * i),
                  ahead(SSD_WIDTH), ahead(GLA_V_WIDTH), ahead(d), mod_of(nxt), mod_of(lambda i: i)]
        + [in_hbm if any(a is w for w in (wout, wup, wdown)) else _resident(a.shape) for a in consts],
        out_specs=pl.BlockSpec((tm, d), lambda i: (i, 0)),
        out_shape=jax.ShapeDtypeStruct((b * s, d), F32),
        scratch_shapes=[pltpu.VMEM((HALO + tm, 2 * FFN_HIDDEN), F32),
                        pltpu.VMEM((tm, FFN_HIDDEN), BF16),
                        pltpu.VMEM((tm, d), F32), pltpu.VMEM((tm, d), BF16),
                        pltpu.VMEM(wout.shape, BF16), pltpu.VMEM(wup.shape, BF16), pltpu.VMEM(wdown.shape, BF16),
                        pltpu.VMEM((2, _CAST_ROWS, wout.shape[1]), F32),
                        pltpu.VMEM((2, _CAST_ROWS, wup.shape[1]), F32),
                        pltpu.VMEM((2, _CAST_ROWS, wdown.shape[1]), F32),
                        pltpu.SemaphoreType.DMA((3, 2))],
        compiler_params=_params("arbitrary"),
        name="channel_mixer",
    )(flat(ys), flat(yg), flat(x), mod, flat(ys), flat(yg), flat(x), mod, mod, *consts)
    return out.reshape(b, s, d)


def _lane_pad(v, width):
    return jnp.pad(v, ((0, 0), (0, width - v.shape[1])))


def _constants():
    q, c = SSD_CHUNK, GLA_CHUNK
    tri_q = np.tril(np.ones((q, q), np.float32))
    tri_c = np.tril(np.ones((c, c), np.float32))
    expand = np.zeros((SMALL_W, SSD_WIDTH), np.float32)
    for h in range(SSD_HEADS):
        expand[h, h * SSD_HEAD_DIM:(h + 1) * SSD_HEAD_DIM] = 1.0
    return (jnp.asarray(np.tile(tri_q, (1, 3)), BF16),
            jnp.asarray(np.tile(tri_c, (1, 3)), BF16),
            jnp.asarray(np.tile(expand, (3, 1)), BF16))


def kernel(x, c, w_ada, b_ada, norm_mix_pre, norm_mix_post, norm_ffn_pre, norm_ffn_post, w_in, ssd_conv_w, ssd_conv_b, ssd_dt_bias, ssd_a_log, ssd_d, ssd_norm, gla_gate_w, gla_gate_b, gla_norm, w_out, ffn_up, ffn_conv_w, ffn_conv_b, ffn_down):
    bsz, seqlen, d = x.shape
    depth = w_ada.shape[0]
    tri_q3, tri_c3, expand3 = _constants()
    c_pad = jnp.pad(c, ((0, SUBLANES - bsz % SUBLANES), (0, 0))) if bsz % SUBLANES else c

    sizes = [SSD_WIDTH, SSD_CONV_DIM, SSD_HEADS, GLA_K_WIDTH, GLA_K_WIDTH, GLA_V_WIDTH,
             GLA_GATE_RANK, GLA_V_WIDTH]
    o = np.concatenate([[0], np.cumsum(sizes)])

    for i in range(depth):
        mod = _ada_call(c_pad, w_ada[i], b_ada[i][None, :])[:bsz].reshape(bsz, ADA_CHUNKS, d)

        wbig, wsmall = _reorder_call(jnp.swapaxes(w_in[i], 0, 1), o)
        ssd_consts = (ssd_conv_w[i], ssd_conv_b[i][None, :],
                      _lane_pad(ssd_dt_bias[i][None, :], SMALL_W), _lane_pad(ssd_a_log[i][None, :], SMALL_W),
                      jnp.repeat(ssd_d[i], SSD_HEAD_DIM)[None, :], ssd_norm[i][None, :], tri_q3, expand3)
        gw = jnp.zeros((SMALL_W, GLA_K_WIDTH), F32).at[SSD_HEADS:SSD_HEADS + GLA_GATE_RANK].set(
            gla_gate_w[i]).astype(BF16)
        gla_consts = (gw, gla_gate_b[i][None, :], gla_norm[i][None, :], tri_c3)
        y_ssd, y_gla = _mixer_call(x, mod, norm_mix_pre[i][None, :], wbig, wsmall, ssd_consts, gla_consts)

        x = _ffn_call(y_ssd, y_gla, x, mod, norm_mix_post[i][None, :], norm_ffn_pre[i][None, :],
                      w_out[i], ffn_up[i], ffn_conv_w[i], ffn_conv_b[i][None, :],
                      norm_ffn_post[i][None, :], ffn_down[i])
    return x
```

```python
import functools

import numpy as np
import jax
import jax.numpy as jnp
from jax import lax
from jax.experimental import pallas as pl
from jax.experimental.pallas import tpu as pltpu

F32 = jnp.float32
BF16 = jnp.bfloat16

D_MODEL = 1024
MIX_WIDTH = 2 * D_MODEL
SSD_WIDTH = MIX_WIDTH // 2
GLA_V_WIDTH = MIX_WIDTH - SSD_WIDTH
SSD_HEAD_DIM = 64
SSD_HEADS = SSD_WIDTH // SSD_HEAD_DIM
SSD_GROUPS = 2
SSD_STATE = 128
SSD_CONV = 4
SSD_CHUNK = 128
SSD_CONV_DIM = SSD_WIDTH + 2 * SSD_GROUPS * SSD_STATE
GLA_HEADS = 4
GLA_K_WIDTH = GLA_V_WIDTH // 2
GLA_HEAD_K = GLA_K_WIDTH // GLA_HEADS
GLA_HEAD_V = GLA_V_WIDTH // GLA_HEADS
GLA_GATE_RANK = 16
GLA_GATE_NORM = 16.0
GLA_CHUNK = 64
FFN_HIDDEN = int(round(8 * D_MODEL / 3 / 128)) * 128
FFN_CONV = 3
ADA_CHUNKS = 6
NORM_EPS = 1e-6
_LOG2E = float(np.log2(np.e))

LANES = 128
SUBLANES = 8
VMEM_LIMIT_BYTES = 58 * 1024 * 1024

SMALL_W = LANES
GROUP_W = SSD_WIDTH // SSD_GROUPS
HEADS_PER_GROUP = SSD_HEADS // SSD_GROUPS
HALO = SUBLANES


def _silu(x):
    return x * jax.nn.sigmoid(x)


def _softplus(x):
    return jnp.maximum(x, 0.0) + jnp.log(1.0 + jnp.exp(-jnp.abs(x)))


def _rms_scale(x):
    return lax.rsqrt(jnp.mean(x * x, axis=-1, keepdims=True) + NORM_EPS)


def _split3(x):
    hi = x.astype(BF16)
    r1 = x - hi.astype(F32)
    mid = r1.astype(BF16)
    lo = (r1 - mid.astype(F32)).astype(BF16)
    return hi, mid, lo


def _dot(a, b):
    return jnp.dot(a, b, preferred_element_type=F32)


def _dot_nt(a, b):
    return lax.dot_general(a, b, (((1,), (1,)), ((), ())), preferred_element_type=F32)


def _dot_tn(a, b):
    return lax.dot_general(a, b, (((0,), (0,)), ((), ())), preferred_element_type=F32)


def _resident(shape):
    nd = len(shape)
    return pl.BlockSpec(shape, lambda *_: (0,) * nd, pipeline_mode=pl.Buffered(1))


def _params(*sem):
    return pltpu.CompilerParams(dimension_semantics=sem, vmem_limit_bytes=VMEM_LIMIT_BYTES)


def _ada_kernel(c_ref, w_ref, b_ref, o_ref):
    ca = _silu(c_ref[...]).astype(BF16)
    o_ref[...] = _dot(ca, w_ref[...].astype(BF16)) + b_ref[...]


def _ada_call(c_pad, w, b):
    rows, d = c_pad.shape
    n = w.shape[1]
    tn = 1024
    return pl.pallas_call(
        _ada_kernel,
        grid=(n // tn,),
        in_specs=[
            pl.BlockSpec((rows, d), lambda j: (0, 0)),
            pl.BlockSpec((d, tn), lambda j: (0, j)),
            pl.BlockSpec((1, tn), lambda j: (0, j)),
        ],
        out_specs=pl.BlockSpec((rows, tn), lambda j: (0, j)),
        out_shape=jax.ShapeDtypeStruct((rows, n), F32),
        compiler_params=_params("arbitrary"),
        name="ada_mod",
    )(c_pad, w, b)


def _reorder_kernel(wt_ref, big_ref, small_ref, *, offs):
    dst = pl.program_id(0) * _PROJ_CHUNK
    skip_dt = offs[3] - offs[2]
    skip_glr = offs[7] - offs[6]
    src = dst + jnp.where(dst >= offs[2], skip_dt, 0) + jnp.where(dst >= offs[6] - skip_dt, skip_glr, 0)
    src = jnp.minimum(src, wt_ref.shape[0] - _PROJ_CHUNK)
    rows = wt_ref[pl.ds(pl.multiple_of(src, 2 * SUBLANES), _PROJ_CHUNK), :]
    big_ref[...] = rows.T.astype(BF16)

    @pl.when(pl.program_id(0) == 0)
    def _():
        d = wt_ref.shape[1]
        pad = jnp.zeros((SMALL_W - skip_dt - skip_glr, d), F32)
        small = jnp.concatenate([wt_ref[offs[2]:offs[3], :], wt_ref[offs[6]:offs[7], :], pad], axis=0)
        small_ref[...] = small.T.astype(BF16)


def _reorder_call(wt, offs):
    n_in, d = wt.shape
    return pl.pallas_call(
        functools.partial(_reorder_kernel, offs=tuple(int(v) for v in offs)),
        grid=(_PROJ_W // _PROJ_CHUNK + 1,),
        in_specs=[_resident((n_in, d))],
        out_specs=[pl.BlockSpec((d, _PROJ_CHUNK), lambda j: (0, j)), pl.BlockSpec((d, SMALL_W), lambda j: (0, 0))],
        out_shape=[jax.ShapeDtypeStruct((d, _PROJ_W + LANES), BF16), jax.ShapeDtypeStruct((d, SMALL_W), BF16)],
        compiler_params=_params("arbitrary"),
        name="reorder_w_in",
    )(wt)


_Z0 = 0
_XBC0 = _Z0 + SSD_WIDTH
_Q0 = _XBC0 + SSD_CONV_DIM
_K0 = _Q0 + GLA_K_WIDTH
_V0 = _K0 + GLA_K_WIDTH
_GO0 = _V0 + GLA_V_WIDTH
_PROJ_W = _GO0 + GLA_V_WIDTH
_PROJ_CHUNK = 512


def _project(hb, w_ref, c0, width):
    parts = [_dot(hb, w_ref[:, c:c + min(_PROJ_CHUNK, c0 + width - c)])
             for c in range(c0, c0 + width, _PROJ_CHUNK)]
    return parts[0] if len(parts) == 1 else jnp.concatenate(parts, axis=1)


def _ssd_stages(env, live, cw_ref, cb_ref, dtb_ref, alog_ref, dexp_ref, nw_ref, tri_ref, xpad, state, y_ref):
    q = SSD_CHUNK
    n = SSD_STATE

    xpad[HALO:HALO + q, :] = env["xbc"]
    ext = xpad[...]
    acc = cw_ref[0:1, :] * ext
    for k in range(1, SSD_CONV):
        acc = pltpu.roll(acc, 1, 0) + cw_ref[k:k + 1, :] * ext
    acc = acc[HALO:HALO + q, :] + cb_ref[...]
    xpad[0:HALO, :] = xpad[q:q + HALO, :]
    xc = _silu(acc)
    xs = xc[:, :SSD_WIDTH]
    bm = xc[:, SSD_WIDTH:SSD_WIDTH + SSD_GROUPS * n].astype(BF16)
    cm = xc[:, SSD_WIDTH + SSD_GROUPS * n:].astype(BF16)
    yield

    lane = lax.broadcasted_iota(jnp.int32, (q, SMALL_W), 1)
    head_lane = lane < SSD_HEADS
    dt = _softplus(env["small"] + dtb_ref[...])
    a = -jnp.exp(alog_ref[...]) * _LOG2E
    dt = jnp.where(head_lane, dt, 0.0)
    da = jnp.where(head_lane, dt * a, 0.0)
    cs = _dot(tri_ref[...], jnp.concatenate(_split3(da), axis=0))
    cs_t = cs.T
    half = (lane >= SSD_HEAD_DIM).astype(jnp.int32)

    def expand(v):
        return jnp.concatenate([jnp.take_along_axis(v, 2 * j + half, axis=1)
                                for j in range(SSD_WIDTH // LANES)], axis=1)

    cs_e = expand(cs)
    dt_e = expand(dt)
    cs_last = cs_e[q - 1:q, :]

    xdt = xs * dt_e
    xdt_b = xdt.astype(BF16)
    xdec_b = (xdt * jnp.exp2(cs_last - cs_e)).astype(BF16)
    decay_in = jnp.exp2(cs_e)
    decay_chunk = jnp.exp2(cs_last)
    yield

    row = lax.broadcasted_iota(jnp.int32, (q, q), 0)
    col = lax.broadcasted_iota(jnp.int32, (q, q), 1)
    causal = row >= col
    first_half = col < SSD_HEAD_DIM

    y_parts = []
    for g in range(SSD_GROUPS):
        cmg = cm[:, g * n:(g + 1) * n]
        bmg = bm[:, g * n:(g + 1) * n]
        gs = slice(g * GROUP_W, (g + 1) * GROUP_W)
        scores = _dot_nt(cmg, bmg)
        y_off = _dot(cmg, state[g].astype(BF16)) * decay_in[:, gs]
        state[g] = decay_chunk[:, gs] * state[g] + _dot_tn(bmg, xdec_b[:, gs])
        diag = []
        for pair in range(HEADS_PER_GROUP // 2):
            h0 = g * HEADS_PER_GROUP + 2 * pair
            ms = []
            for h in (h0, h0 + 1):
                seg = cs[:, h:h + 1] - cs_t[h:h + 1, :]
                decay = jnp.exp2(jnp.where(causal, seg, -jnp.inf))
                ms.append((scores * decay).astype(BF16))
            slab = xdt_b[:, h0 * SSD_HEAD_DIM:(h0 + 2) * SSD_HEAD_DIM]
            zero = jnp.zeros_like(slab)
            rhs = jnp.concatenate([jnp.where(first_half, slab, zero),
                                   jnp.where(first_half, zero, slab)], axis=0)
            diag.append(_dot(jnp.concatenate(ms, axis=1), rhs))
        y_parts.append(jnp.concatenate(diag, axis=1) + y_off)
        yield
    y = jnp.concatenate(y_parts, axis=1) + xs * dexp_ref[...]
    y = y * _silu(env["z"])
    outs = []
    for g in range(SSD_GROUPS):
        yg = y[:, g * GROUP_W:(g + 1) * GROUP_W]
        outs.append(yg * _rms_scale(yg))
    out = (jnp.concatenate(outs, axis=1) * nw_ref[...]).astype(BF16)
    y_ref[...] = jnp.where(live, out, y_ref[...])
    yield


def _gla_stages(env, r, fresh, gw_ref, gb_ref, nw_ref, tri_ref, state, y_ref):
    c = GLA_CHUNK
    dk, dv = GLA_HEAD_K, GLA_HEAD_V
    row = lax.broadcasted_iota(jnp.int32, (c, c), 0)
    col = lax.broadcasted_iota(jnp.int32, (c, c), 1)
    causal = row >= col

    qf = env["q"][r, :] * (dk ** -0.5)
    kf = env["k"][r, :]
    logit = _dot(env["small"][r, :].astype(BF16), gw_ref[...]) + gb_ref[...]
    lg = -_softplus(-logit) * (_LOG2E / GLA_GATE_NORM)
    bcum = _dot(tri_ref[...], jnp.concatenate(_split3(lg), axis=0))
    blast = bcum[c - 1:c, :]
    qt = (qf * jnp.exp2(bcum)).astype(BF16)
    kt = (kf * jnp.exp2(-bcum)).astype(BF16)
    kd = (kf * jnp.exp2(blast - bcum)).astype(BF16)
    eblast = jnp.exp2(blast)
    yield
    for h in range(GLA_HEADS):
        ks = slice(h * dk, (h + 1) * dk)
        vs = slice(h * dv, (h + 1) * dv)
        attn = jnp.where(causal, _dot_nt(qt[:, ks], kt[:, ks]), 0.0)
        vh = env["v"][r, vs]
        st = jnp.where(fresh, 0.0, state[h])
        o = _dot(attn.astype(BF16), vh) + _dot_nt(qt[:, ks], st.astype(BF16))
        state[h] = eblast[:, ks] * st + _dot_tn(vh, kd[:, ks])
        o = o * _rms_scale(o) * nw_ref[...]
        o = o * _silu(env["go"][r, vs])
        y_ref[:, vs] = o.astype(BF16)
        if h % 2 == 1:
            yield


def _gla_levels(env, rows, gw_ref, gb_ref, nw_ref, tri_ref, state, y_ref):
    c = GLA_CHUNK
    dk, dv = GLA_HEAD_K, GLA_HEAD_V
    row = lax.broadcasted_iota(jnp.int32, (c, c), 0)
    col = lax.broadcasted_iota(jnp.int32, (c, c), 1)
    causal = row >= col
    heads = range(GLA_HEADS)

    logit = _dot(env["small"].astype(BF16), gw_ref[...]) + gb_ref[...]
    lg = -_softplus(-logit) * (_LOG2E / GLA_GATE_NORM)
    yield
    bcum = _dot(tri_ref[...], jnp.concatenate(_split3(lg), axis=0))
    blast = bcum[c - 1:c, :]
    qf = env["q"] * (dk ** -0.5)
    kf = env["k"]
    qt = (qf * jnp.exp2(bcum)).astype(BF16)
    kt = (kf * jnp.exp2(-bcum)).astype(BF16)
    kd = (kf * jnp.exp2(blast - bcum)).astype(BF16)
    eblast = jnp.exp2(blast)
    yield
    ks = [slice(h * dk, (h + 1) * dk) for h in heads]
    vs = [slice(h * dv, (h + 1) * dv) for h in heads]
    attn = [jnp.where(causal, _dot_nt(qt[:, ks[h]], kt[:, ks[h]]), 0.0).astype(BF16) for h in heads]
    yield
    o = [_dot(attn[h], env["v"][:, vs[h]]) + _dot_nt(qt[:, ks[h]], state[h].astype(BF16)) for h in heads]
    for h in heads:
        state[h] = eblast[:, ks[h]] * state[h] + _dot_tn(env["v"][:, vs[h]], kd[:, ks[h]])
    for h in heads:
        oh = o[h] * _rms_scale(o[h]) * nw_ref[...]
        y_ref[rows, vs[h]] = (oh * _silu(env["go"][:, vs[h]])).astype(BF16)
    yield


def _prenorm(x_ref, mod_ref, npre_ref, hb_s):
    x = x_ref[...]
    xn = (x * _rms_scale(x)) * npre_ref[...]
    hb_s[...] = (xn * (1.0 + mod_ref[1:2, :]) + mod_ref[0:1, :]).astype(BF16)


def _mixer_kernel(x0_ref, mod0_ref, xn_ref, modn_ref, npre_ref, wbig_ref, wsmall_ref,
                  cw_ref, cb_ref, dtb_ref, alog_ref, dexp_ref, snw_ref, triq_ref,
                  gw_ref, gb_ref, gnw_ref, tric_ref,
                  ys_ref, yg_ref, xpad, sstate, gstate, hb_s,
                  yg0_s, q_s, k_s, v_s, go_s, small_s, *, tiles_per_seq):
    i = pl.program_id(0)
    fresh = lax.rem(i, tiles_per_seq) == 0
    live = i < pl.num_programs(0) - 1

    @pl.when(i == 0)
    def _():
        _prenorm(x0_ref, mod0_ref, npre_ref, hb_s)
        gstate[...] = jnp.zeros(gstate.shape, F32)
        for ref in (yg0_s, q_s, k_s, v_s, go_s, small_s):
            ref[...] = jnp.zeros(ref.shape, ref.dtype)

    @pl.when(fresh)
    def _():
        xpad[0:HALO, :] = jnp.zeros((HALO, SSD_CONV_DIM), F32)
        sstate[...] = jnp.zeros(sstate.shape, F32)

    hb = hb_s[...]
    first = slice(0, GLA_CHUNK)
    second = slice(GLA_CHUNK, 2 * GLA_CHUNK)
    xbc_cols = [slice(c, c + _PROJ_CHUNK) for c in range(_XBC0, _XBC0 + SSD_CONV_DIM, _PROJ_CHUNK)]
    prev = {"q": q_s[...], "k": k_s[...], "v": v_s[...], "go": go_s[...], "small": small_s[...]}
    gla1 = _gla_levels(prev, second, gw_ref, gb_ref, gnw_ref, tric_ref, gstate, yg_ref)
    yg_ref[first, :] = yg0_s[...]
    next(gla1)
    env = {"small": _dot(hb, wsmall_ref[...])}
    xbc = [_dot(hb, wbig_ref[:, xbc_cols[0]])]
    next(gla1)
    xbc.append(_dot(hb, wbig_ref[:, xbc_cols[1]]))
    next(gla1)
    xbc.append(_dot(hb, wbig_ref[:, xbc_cols[2]]))
    env["xbc"] = jnp.concatenate(xbc, axis=1)
    ssd = _ssd_stages(env, live, cw_ref, cb_ref, dtb_ref, alog_ref, dexp_ref, snw_ref, triq_ref,
                      xpad, sstate, ys_ref)
    gla0 = _gla_stages(env, first, fresh, gw_ref, gb_ref, gnw_ref, tric_ref, gstate, yg0_s)
    env["q"] = _project(hb, wbig_ref, _Q0, GLA_K_WIDTH)
    next(gla1)
    env["k"] = _project(hb, wbig_ref, _K0, GLA_K_WIDTH)
    next(ssd)
    env["v"] = _project(hb, wbig_ref, _V0, GLA_V_WIDTH).astype(BF16)
    next(gla0)
    next(ssd)
    env["go"] = _project(hb, wbig_ref, _GO0, GLA_V_WIDTH)
    next(gla0)
    next(ssd)
    env["z"] = _project(hb, wbig_ref, _Z0, SSD_WIDTH)
    _prenorm(xn_ref, modn_ref, npre_ref, hb_s)
    next(gla0)
    next(ssd)
    next(ssd)
    q_s[...] = env["q"][second, :]
    k_s[...] = env["k"][second, :]
    v_s[...] = env["v"][second, :]
    go_s[...] = env["go"][second, :]
    small_s[...] = env["small"][second, :]


def _mixer_call(x, mod, npre, wbig, wsmall, ssd_consts, gla_consts):
    b, s, d = x.shape
    t = SSD_CHUNK
    assert t == 2 * GLA_CHUNK
    tiles_per_seq = s // t
    n = b * tiles_per_seq
    xf = x.reshape(b * s, d)
    nxt = lambda i: jnp.minimum(i + 1, n - 1)
    cur = lambda i: (jnp.minimum(i, n - 1), 0)
    done = lambda i: (jnp.maximum(i - 1, 0), 0)
    consts = (npre, wbig, wsmall) + tuple(ssd_consts) + tuple(gla_consts)
    ys, yg = pl.pallas_call(
        functools.partial(_mixer_kernel, tiles_per_seq=tiles_per_seq),
        grid=(n + 1,),
        in_specs=[pl.BlockSpec((t, d), lambda i: (0, 0)),
                  pl.BlockSpec((None, ADA_CHUNKS, d), lambda i: (0, 0, 0)),
                  pl.BlockSpec((t, d), lambda i: (nxt(i), 0)),
                  pl.BlockSpec((None, ADA_CHUNKS, d), lambda i: (nxt(i) // tiles_per_seq, 0, 0))]
        + [_resident(a.shape) for a in consts],
        out_specs=[pl.BlockSpec((t, SSD_WIDTH), cur),
                   pl.BlockSpec((t, GLA_V_WIDTH), done)],
        out_shape=[jax.ShapeDtypeStruct((b * s, SSD_WIDTH), BF16),
                   jax.ShapeDtypeStruct((b * s, GLA_V_WIDTH), BF16)],
        scratch_shapes=[
            pltpu.VMEM((HALO + SSD_CHUNK, SSD_CONV_DIM), F32),
            pltpu.VMEM((SSD_GROUPS, SSD_STATE, GROUP_W), F32),
            pltpu.VMEM((GLA_HEADS, GLA_HEAD_V, GLA_HEAD_K), F32),
            pltpu.VMEM((t, d), BF16),
            pltpu.VMEM((GLA_CHUNK, GLA_V_WIDTH), BF16),
            pltpu.VMEM((GLA_CHUNK, GLA_K_WIDTH), F32),
            pltpu.VMEM((GLA_CHUNK, GLA_K_WIDTH), F32),
            pltpu.VMEM((GLA_CHUNK, GLA_V_WIDTH), BF16),
            pltpu.VMEM((GLA_CHUNK, GLA_V_WIDTH), F32),
            pltpu.VMEM((GLA_CHUNK, SMALL_W), F32),
        ],
        compiler_params=_params("arbitrary"),
        name="token_mixer",
    )(xf, mod, xf, mod, *consts)
    return ys.reshape(b, s, SSD_WIDTH), yg.reshape(b, s, GLA_V_WIDTH)


_UP_CHUNK = 384
_UP_DOT = 768
_FFN_TM = 256
_CAST_ROWS = 128


def _cast_weight(w_hbm, dst, stage, sem):
    rows = stage.shape[1]
    n = w_hbm.shape[0] // rows

    def copy(c):
        return pltpu.make_async_copy(w_hbm.at[pl.ds(c * rows, rows), :], stage.at[c % 2], sem.at[c % 2])

    copy(0).start()
    for c in range(n):
        if c + 1 < n:
            copy(c + 1).start()
        yield
        copy(c).wait()
        dst[c * rows:(c + 1) * rows, :] = stage[c % 2].astype(BF16)


def _stage_ffn_input(ys_ref, yg_ref, x_ref, mod_ref, npost1_ref, npre_ref, wout_ref, x1_s, hb_s):
    y = _dot(ys_ref[...], wout_ref[0:SSD_WIDTH, :]) + _dot(yg_ref[...], wout_ref[SSD_WIDTH:, :])
    yn = (y * _rms_scale(y)) * npost1_ref[...]
    x1 = x_ref[...] + mod_ref[2:3, :] * yn
    x1_s[...] = x1
    h = ((x1 * _rms_scale(x1)) * npre_ref[...]) * (1.0 + mod_ref[4:5, :]) + mod_ref[3:4, :]
    hb_s[...] = h.astype(BF16)


def _ffn_kernel(ys0_ref, yg0_ref, x0_ref, mod0_ref, ysn_ref, ygn_ref, xn_ref, modn_ref, mod_ref,
                npost1_ref, npre_ref, wout_hbm, wup_hbm, cw_ref, cb_ref, npost2_ref, wdown_hbm,
                o_ref, upad, gate, x1_s, hb_s,
                wout_ref, wup_ref, wdown_ref, st_out, st_up, st_down, sem, *, tiles_per_seq):
    tm = o_ref.shape[0]
    i = pl.program_id(0)

    @pl.when(i == 0)
    def _():
        casts = [_cast_weight(wout_hbm, wout_ref, st_out, sem.at[0]),
                 _cast_weight(wup_hbm, wup_ref, st_up, sem.at[1]),
                 _cast_weight(wdown_hbm, wdown_ref, st_down, sem.at[2])]
        while casts:
            casts = [g for g in casts if next(g, True) is None]
        _stage_ffn_input(ys0_ref, yg0_ref, x0_ref, mod0_ref, npost1_ref, npre_ref, wout_ref, x1_s, hb_s)

    @pl.when(lax.rem(i, tiles_per_seq) == 0)
    def _():
        upad[0:HALO, :] = jnp.zeros((HALO, upad.shape[1]), F32)

    x1 = x1_s[...]
    hb = hb_s[...]

    done = set()

    def up(col):
        p = col // _UP_DOT
        if p not in done:
            done.add(p)
            cols = slice(p * _UP_DOT, (p + 1) * _UP_DOT)
            upad[HALO:HALO + tm, cols] = _dot(hb, wup_ref[:, cols])

    def conv(c0):
        cols = slice(c0, c0 + _UP_CHUNK)
        up(c0)
        up(c0 + _UP_CHUNK - 1)
        ext = upad[:, cols]
        acc = cw_ref[0:1, cols] * ext
        for k in range(1, FFN_CONV):
            acc = pltpu.roll(acc, 1, 0) + cw_ref[k:k + 1, cols] * ext
        upad[0:HALO, cols] = upad[tm:tm + HALO, cols]
        return acc[HALO:HALO + tm, :] + cb_ref[:, cols]

    for c in range(0, FFN_HIDDEN, _UP_CHUNK):
        gate[:, c:c + _UP_CHUNK] = (_silu(conv(c)) * conv(FFN_HIDDEN + c)).astype(BF16)
    _stage_ffn_input(ysn_ref, ygn_ref, xn_ref, modn_ref, npost1_ref, npre_ref, wout_ref, x1_s, hb_s)
    f = _dot(gate[...], wdown_ref[...])
    fn = (f * _rms_scale(f)) * npost2_ref[...]
    o_ref[...] = x1 + mod_ref[5:6, :] * fn


def _ffn_call(ys, yg, x, mod, npost1, npre, wout, wup, cw, cb, npost2, wdown):
    b, s, d = x.shape
    tm = _FFN_TM
    tiles_per_seq = s // tm
    n = b * tiles_per_seq
    flat = lambda a: a.reshape(b * s, a.shape[-1])
    nxt = lambda i: jnp.minimum(i + 1, n - 1)
    first = lambda w: pl.BlockSpec((tm, w), lambda i: (0, 0))
    ahead = lambda w: pl.BlockSpec((tm, w), lambda i: (nxt(i), 0))
    mod_of = lambda tile: pl.BlockSpec((None, ADA_CHUNKS, d), lambda i: (tile(i) // tiles_per_seq, 0, 0))
    consts = (npost1, npre, wout, wup, cw, cb, npost2, wdown)
    in_hbm = pl.BlockSpec(memory_space=pl.ANY)
    out = pl.pallas_call(
        functools.partial(_ffn_kernel, tiles_per_seq=tiles_per_seq),
        grid=(n,),
        in_specs=[first(SSD_WIDTH), first(GLA_V_WIDTH), first(d), mod_of(lambda i: 0 * i),
                  ahead(SSD_WIDTH), ahead(GLA_V_WIDTH), ahead(d), mod_of(nxt), mod_of(lambda i: i)]
        + [in_hbm if any(a is w for w in (wout, wup, wdown)) else _resident(a.shape) for a in consts],
        out_specs=pl.BlockSpec((tm, d), lambda i: (i, 0)),
        out_shape=jax.ShapeDtypeStruct((b * s, d), F32),
        scratch_shapes=[pltpu.VMEM((HALO + tm, 2 * FFN_HIDDEN), F32),
                        pltpu.VMEM((tm, FFN_HIDDEN), BF16),
                        pltpu.VMEM((tm, d), F32), pltpu.VMEM((tm, d), BF16),
                        pltpu.VMEM(wout.shape, BF16), pltpu.VMEM(wup.shape, BF16), pltpu.VMEM(wdown.shape, BF16),
                        pltpu.VMEM((2, _CAST_ROWS, wout.shape[1]), F32),
                        pltpu.VMEM((2, _CAST_ROWS, wup.shape[1]), F32),
                        pltpu.VMEM((2, _CAST_ROWS, wdown.shape[1]), F32),
                        pltpu.SemaphoreType.DMA((3, 2))],
        compiler_params=_params("arbitrary"),
        name="channel_mixer",
    )(flat(ys), flat(yg), flat(x), mod, flat(ys), flat(yg), flat(x), mod, mod, *consts)
    return out.reshape(b, s, d)


def _lane_pad(v, width):
    return jnp.pad(v, ((0, 0), (0, width - v.shape[1])))


def _constants():
    q, c = SSD_CHUNK, GLA_CHUNK
    tri_q = np.tril(np.ones((q, q), np.float32))
    tri_c = np.tril(np.ones((c, c), np.float32))
    return (jnp.asarray(np.tile(tri_q, (1, 3)), BF16),
            jnp.asarray(np.tile(tri_c, (1, 3)), BF16))


def kernel(x, c, w_ada, b_ada, norm_mix_pre, norm_mix_post, norm_ffn_pre, norm_ffn_post, w_in, ssd_conv_w, ssd_conv_b, ssd_dt_bias, ssd_a_log, ssd_d, ssd_norm, gla_gate_w, gla_gate_b, gla_norm, w_out, ffn_up, ffn_conv_w, ffn_conv_b, ffn_down):
    bsz, seqlen, d = x.shape
    depth = w_ada.shape[0]
    tri_q3, tri_c3 = _constants()
    c_pad = jnp.pad(c, ((0, SUBLANES - bsz % SUBLANES), (0, 0))) if bsz % SUBLANES else c

    sizes = [SSD_WIDTH, SSD_CONV_DIM, SSD_HEADS, GLA_K_WIDTH, GLA_K_WIDTH, GLA_V_WIDTH,
             GLA_GATE_RANK, GLA_V_WIDTH]
    o = np.concatenate([[0], np.cumsum(sizes)])

    for i in range(depth):
        mod = _ada_call(c_pad, w_ada[i], b_ada[i][None, :])[:bsz].reshape(bsz, ADA_CHUNKS, d)

        wbig, wsmall = _reorder_call(jnp.swapaxes(w_in[i], 0, 1), o)
        ssd_consts = (ssd_conv_w[i], ssd_conv_b[i][None, :],
                      _lane_pad(ssd_dt_bias[i][None, :], SMALL_W), _lane_pad(ssd_a_log[i][None, :], SMALL_W),
                      jnp.repeat(ssd_d[i], SSD_HEAD_DIM)[None, :], ssd_norm[i][None, :], tri_q3)
        gw = jnp.zeros((SMALL_W, GLA_K_WIDTH), F32).at[SSD_HEADS:SSD_HEADS + GLA_GATE_RANK].set(
            gla_gate_w[i]).astype(BF16)
        gla_consts = (gw, gla_gate_b[i][None, :], gla_norm[i][None, :], tri_c3)
        y_ssd, y_gla = _mixer_call(x, mod, norm_mix_pre[i][None, :], wbig, wsmall, ssd_consts, gla_consts)

        x = _ffn_call(y_ssd, y_gla, x, mod, norm_mix_post[i][None, :], norm_ffn_pre[i][None, :],
                      w_out[i], ffn_up[i], ffn_conv_w[i], ffn_conv_b[i][None, :],
                      norm_ffn_post[i][None, :], ffn_down[i])
    return x
```

```python
import functools

import numpy as np
import jax
import jax.numpy as jnp
from jax import lax
from jax.experimental import pallas as pl
from jax.experimental.pallas import tpu as pltpu

F32 = jnp.float32
BF16 = jnp.bfloat16

D_MODEL = 1024
MIX_WIDTH = 2 * D_MODEL
SSD_WIDTH = MIX_WIDTH // 2
GLA_V_WIDTH = MIX_WIDTH - SSD_WIDTH
SSD_HEAD_DIM = 64
SSD_HEADS = SSD_WIDTH // SSD_HEAD_DIM
SSD_GROUPS = 2
SSD_STATE = 128
SSD_CONV = 4
SSD_CHUNK = 128
SSD_CONV_DIM = SSD_WIDTH + 2 * SSD_GROUPS * SSD_STATE
GLA_HEADS = 4
GLA_K_WIDTH = GLA_V_WIDTH // 2
GLA_HEAD_K = GLA_K_WIDTH // GLA_HEADS
GLA_HEAD_V = GLA_V_WIDTH // GLA_HEADS
GLA_GATE_RANK = 16
GLA_GATE_NORM = 16.0
GLA_CHUNK = 64
FFN_HIDDEN = int(round(8 * D_MODEL / 3 / 128)) * 128
FFN_CONV = 3
ADA_CHUNKS = 6
NORM_EPS = 1e-6
_LOG2E = float(np.log2(np.e))

LANES = 128
SUBLANES = 8
VMEM_LIMIT_BYTES = 58 * 1024 * 1024

SMALL_W = LANES
GROUP_W = SSD_WIDTH // SSD_GROUPS
HEADS_PER_GROUP = SSD_HEADS // SSD_GROUPS
HALO = SUBLANES


def _silu(x):
    return x * jax.nn.sigmoid(x)


def _softplus(x):
    return jnp.maximum(x, 0.0) + jnp.log(1.0 + jnp.exp(-jnp.abs(x)))


def _rms_scale(x):
    return lax.rsqrt(jnp.mean(x * x, axis=-1, keepdims=True) + NORM_EPS)


def _split3(x):
    hi = x.astype(BF16)
    r1 = x - hi.astype(F32)
    mid = r1.astype(BF16)
    lo = (r1 - mid.astype(F32)).astype(BF16)
    return hi, mid, lo


def _dot(a, b):
    return jnp.dot(a, b, preferred_element_type=F32)


def _dot_nt(a, b):
    return lax.dot_general(a, b, (((1,), (1,)), ((), ())), preferred_element_type=F32)


def _dot_tn(a, b):
    return lax.dot_general(a, b, (((0,), (0,)), ((), ())), preferred_element_type=F32)


def _resident(shape):
    nd = len(shape)
    return pl.BlockSpec(shape, lambda *_: (0,) * nd, pipeline_mode=pl.Buffered(1))


def _params(*sem):
    return pltpu.CompilerParams(dimension_semantics=sem, vmem_limit_bytes=VMEM_LIMIT_BYTES)


def _ada_kernel(c_ref, w_ref, b_ref, o_ref):
    ca = _silu(c_ref[...]).astype(BF16)
    o_ref[...] = _dot(ca, w_ref[...].astype(BF16)) + b_ref[...]


def _ada_call(c_pad, w, b):
    rows, d = c_pad.shape
    n = w.shape[1]
    tn = 1024
    return pl.pallas_call(
        _ada_kernel,
        grid=(n // tn,),
        in_specs=[
            pl.BlockSpec((rows, d), lambda j: (0, 0)),
            pl.BlockSpec((d, tn), lambda j: (0, j)),
            pl.BlockSpec((1, tn), lambda j: (0, j)),
        ],
        out_specs=pl.BlockSpec((rows, tn), lambda j: (0, j)),
        out_shape=jax.ShapeDtypeStruct((rows, n), F32),
        compiler_params=_params("arbitrary"),
        name="ada_mod",
    )(c_pad, w, b)


_Z0 = 0
_XBC0 = _Z0 + SSD_WIDTH
_Q0 = _XBC0 + SSD_CONV_DIM
_K0 = _Q0 + GLA_K_WIDTH
_V0 = _K0 + GLA_K_WIDTH
_GO0 = _V0 + GLA_V_WIDTH
_PROJ_W = _GO0 + GLA_V_WIDTH
_PROJ_CHUNK = 512


def _load_projection(wt_hbm, wbig_ref, wsmall_ref, stage, small_stage, sem, offs):
    d = wt_hbm.shape[1]
    skip_dt = offs[3] - offs[2]
    skip_glr = offs[7] - offs[6]
    n = _PROJ_W // _PROJ_CHUNK

    def copy(j):
        dst = j * _PROJ_CHUNK
        src = dst + (skip_dt if dst >= offs[2] else 0) + (skip_glr if dst >= offs[6] - skip_dt else 0)
        return pltpu.make_async_copy(wt_hbm.at[pl.ds(src, _PROJ_CHUNK), :], stage.at[j % 2], sem.at[j % 2])

    small_copies = [
        pltpu.make_async_copy(wt_hbm.at[pl.ds(offs[2], skip_dt), :], small_stage.at[pl.ds(0, skip_dt), :], sem.at[2]),
        pltpu.make_async_copy(wt_hbm.at[pl.ds(offs[6], skip_glr), :], small_stage.at[pl.ds(skip_dt, skip_glr), :],
                              sem.at[3])]
    for cp in small_copies:
        cp.start()
    copy(0).start()
    for j in range(n):
        if j + 1 < n:
            copy(j + 1).start()
        copy(j).wait()
        wbig_ref[:, j * _PROJ_CHUNK:(j + 1) * _PROJ_CHUNK] = stage[j % 2].T.astype(BF16)
    for cp in small_copies:
        cp.wait()
    pad = jnp.zeros((SMALL_W - skip_dt - skip_glr, d), F32)
    wsmall_ref[...] = jnp.concatenate([small_stage[...], pad], axis=0).T.astype(BF16)


def _project(hb, w_ref, c0, width):
    parts = [_dot(hb, w_ref[:, c:c + min(_PROJ_CHUNK, c0 + width - c)])
             for c in range(c0, c0 + width, _PROJ_CHUNK)]
    return parts[0] if len(parts) == 1 else jnp.concatenate(parts, axis=1)


def _ssd_stages(env, live, cw_ref, cb_ref, dtb_ref, alog_ref, dexp_ref, nw_ref, tri_ref, xpad, state, y_ref):
    q = SSD_CHUNK
    n = SSD_STATE

    xpad[HALO:HALO + q, :] = env["xbc"]
    ext = xpad[...]
    acc = cw_ref[0:1, :] * ext
    for k in range(1, SSD_CONV):
        acc = pltpu.roll(acc, 1, 0) + cw_ref[k:k + 1, :] * ext
    acc = acc[HALO:HALO + q, :] + cb_ref[...]
    xpad[0:HALO, :] = xpad[q:q + HALO, :]
    xc = _silu(acc)
    xs = xc[:, :SSD_WIDTH]
    bm = xc[:, SSD_WIDTH:SSD_WIDTH + SSD_GROUPS * n].astype(BF16)
    cm = xc[:, SSD_WIDTH + SSD_GROUPS * n:].astype(BF16)
    yield

    lane = lax.broadcasted_iota(jnp.int32, (q, SMALL_W), 1)
    head_lane = lane < SSD_HEADS
    dt = _softplus(env["small"] + dtb_ref[...])
    a = -jnp.exp(alog_ref[...]) * _LOG2E
    dt = jnp.where(head_lane, dt, 0.0)
    da = jnp.where(head_lane, dt * a, 0.0)
    cs = _dot(tri_ref[...], jnp.concatenate(_split3(da), axis=0))
    cs_t = cs.T
    half = (lane >= SSD_HEAD_DIM).astype(jnp.int32)

    def expand(v):
        return jnp.concatenate([jnp.take_along_axis(v, 2 * j + half, axis=1)
                                for j in range(SSD_WIDTH // LANES)], axis=1)

    cs_e = expand(cs)
    dt_e = expand(dt)
    cs_last = cs_e[q - 1:q, :]

    xdt = xs * dt_e
    xdt_b = xdt.astype(BF16)
    xdec_b = (xdt * jnp.exp2(cs_last - cs_e)).astype(BF16)
    decay_in = jnp.exp2(cs_e)
    decay_chunk = jnp.exp2(cs_last)
    yield

    row = lax.broadcasted_iota(jnp.int32, (q, q), 0)
    col = lax.broadcasted_iota(jnp.int32, (q, q), 1)
    causal = row >= col
    first_half = col < SSD_HEAD_DIM

    y_parts = []
    for g in range(SSD_GROUPS):
        cmg = cm[:, g * n:(g + 1) * n]
        bmg = bm[:, g * n:(g + 1) * n]
        gs = slice(g * GROUP_W, (g + 1) * GROUP_W)
        scores = _dot_nt(cmg, bmg)
        y_off = _dot(cmg, state[g].astype(BF16)) * decay_in[:, gs]
        state[g] = decay_chunk[:, gs] * state[g] + _dot_tn(bmg, xdec_b[:, gs])
        diag = []
        for pair in range(HEADS_PER_GROUP // 2):
            h0 = g * HEADS_PER_GROUP + 2 * pair
            ms = []
            for h in (h0, h0 + 1):
                seg = cs[:, h:h + 1] - cs_t[h:h + 1, :]
                decay = jnp.exp2(jnp.where(causal, seg, -jnp.inf))
                ms.append((scores * decay).astype(BF16))
            slab = xdt_b[:, h0 * SSD_HEAD_DIM:(h0 + 2) * SSD_HEAD_DIM]
            zero = jnp.zeros_like(slab)
            rhs = jnp.concatenate([jnp.where(first_half, slab, zero),
                                   jnp.where(first_half, zero, slab)], axis=0)
            diag.append(_dot(jnp.concatenate(ms, axis=1), rhs))
        y_parts.append(jnp.concatenate(diag, axis=1) + y_off)
        yield
    y = jnp.concatenate(y_parts, axis=1) + xs * dexp_ref[...]
    y = y * _silu(env["z"])
    outs = []
    for g in range(SSD_GROUPS):
        yg = y[:, g * GROUP_W:(g + 1) * GROUP_W]
        outs.append(yg * _rms_scale(yg))
    out = (jnp.concatenate(outs, axis=1) * nw_ref[...]).astype(BF16)
    y_ref[...] = jnp.where(live, out, y_ref[...])
    yield


def _gla_stages(env, r, fresh, gw_ref, gb_ref, nw_ref, tri_ref, state, y_ref):
    c = GLA_CHUNK
    dk, dv = GLA_HEAD_K, GLA_HEAD_V
    row = lax.broadcasted_iota(jnp.int32, (c, c), 0)
    col = lax.broadcasted_iota(jnp.int32, (c, c), 1)
    causal = row >= col

    qf = env["q"][r, :] * (dk ** -0.5)
    kf = env["k"][r, :]
    logit = _dot(env["small"][r, :].astype(BF16), gw_ref[...]) + gb_ref[...]
    lg = -_softplus(-logit) * (_LOG2E / GLA_GATE_NORM)
    bcum = _dot(tri_ref[...], jnp.concatenate(_split3(lg), axis=0))
    blast = bcum[c - 1:c, :]
    qt = (qf * jnp.exp2(bcum)).astype(BF16)
    kt = (kf * jnp.exp2(-bcum)).astype(BF16)
    kd = (kf * jnp.exp2(blast - bcum)).astype(BF16)
    eblast = jnp.exp2(blast)
    yield
    for h in range(GLA_HEADS):
        ks = slice(h * dk, (h + 1) * dk)
        vs = slice(h * dv, (h + 1) * dv)
        attn = jnp.where(causal, _dot_nt(qt[:, ks], kt[:, ks]), 0.0)
        vh = env["v"][r, vs]
        st = jnp.where(fresh, 0.0, state[h])
        o = _dot(attn.astype(BF16), vh) + _dot_nt(qt[:, ks], st.astype(BF16))
        state[h] = eblast[:, ks] * st + _dot_tn(vh, kd[:, ks])
        o = o * _rms_scale(o) * nw_ref[...]
        o = o * _silu(env["go"][r, vs])
        y_ref[:, vs] = o.astype(BF16)
        if h % 2 == 1:
            yield


def _gla_levels(env, rows, gw_ref, gb_ref, nw_ref, tri_ref, state, y_ref):
    c = GLA_CHUNK
    dk, dv = GLA_HEAD_K, GLA_HEAD_V
    row = lax.broadcasted_iota(jnp.int32, (c, c), 0)
    col = lax.broadcasted_iota(jnp.int32, (c, c), 1)
    causal = row >= col
    heads = range(GLA_HEADS)

    logit = _dot(env["small"].astype(BF16), gw_ref[...]) + gb_ref[...]
    lg = -_softplus(-logit) * (_LOG2E / GLA_GATE_NORM)
    yield
    bcum = _dot(tri_ref[...], jnp.concatenate(_split3(lg), axis=0))
    blast = bcum[c - 1:c, :]
    qf = env["q"] * (dk ** -0.5)
    kf = env["k"]
    qt = (qf * jnp.exp2(bcum)).astype(BF16)
    kt = (kf * jnp.exp2(-bcum)).astype(BF16)
    kd = (kf * jnp.exp2(blast - bcum)).astype(BF16)
    eblast = jnp.exp2(blast)
    yield
    ks = [slice(h * dk, (h + 1) * dk) for h in heads]
    vs = [slice(h * dv, (h + 1) * dv) for h in heads]
    attn = [jnp.where(causal, _dot_nt(qt[:, ks[h]], kt[:, ks[h]]), 0.0).astype(BF16) for h in heads]
    yield
    o = [_dot(attn[h], env["v"][:, vs[h]]) + _dot_nt(qt[:, ks[h]], state[h].astype(BF16)) for h in heads]
    for h in heads:
        state[h] = eblast[:, ks[h]] * state[h] + _dot_tn(env["v"][:, vs[h]], kd[:, ks[h]])
    for h in heads:
        oh = o[h] * _rms_scale(o[h]) * nw_ref[...]
        y_ref[rows, vs[h]] = (oh * _silu(env["go"][:, vs[h]])).astype(BF16)
    yield


def _prenorm(x_ref, mod_ref, npre_ref, hb_s):
    x = x_ref[...]
    xn = (x * _rms_scale(x)) * npre_ref[...]
    hb_s[...] = (xn * (1.0 + mod_ref[1:2, :]) + mod_ref[0:1, :]).astype(BF16)


def _mixer_kernel(x0_ref, mod0_ref, xn_ref, modn_ref, wt_hbm, npre_ref,
                  cw_ref, cb_ref, dtb_ref, alog_ref, dexp_ref, snw_ref, triq_ref,
                  gw_ref, gb_ref, gnw_ref, tric_ref,
                  ys_ref, yg_ref, xpad, sstate, gstate, hb_s,
                  yg0_s, q_s, k_s, v_s, go_s, small_s,
                  wbig_ref, wsmall_ref, w_stage, wsmall_stage, w_sem, *, tiles_per_seq, offs):
    i = pl.program_id(0)
    fresh = lax.rem(i, tiles_per_seq) == 0
    live = i < pl.num_programs(0) - 1

    @pl.when(i == 0)
    def _():
        _load_projection(wt_hbm, wbig_ref, wsmall_ref, w_stage, wsmall_stage, w_sem, offs)
        _prenorm(x0_ref, mod0_ref, npre_ref, hb_s)
        gstate[...] = jnp.zeros(gstate.shape, F32)
        for ref in (yg0_s, q_s, k_s, v_s, go_s, small_s):
            ref[...] = jnp.zeros(ref.shape, ref.dtype)

    @pl.when(fresh)
    def _():
        xpad[0:HALO, :] = jnp.zeros((HALO, SSD_CONV_DIM), F32)
        sstate[...] = jnp.zeros(sstate.shape, F32)

    hb = hb_s[...]
    first = slice(0, GLA_CHUNK)
    second = slice(GLA_CHUNK, 2 * GLA_CHUNK)
    xbc_cols = [slice(c, c + _PROJ_CHUNK) for c in range(_XBC0, _XBC0 + SSD_CONV_DIM, _PROJ_CHUNK)]
    prev = {"q": q_s[...], "k": k_s[...], "v": v_s[...], "go": go_s[...], "small": small_s[...]}
    gla1 = _gla_levels(prev, second, gw_ref, gb_ref, gnw_ref, tric_ref, gstate, yg_ref)
    yg_ref[first, :] = yg0_s[...]
    next(gla1)
    env = {"small": _dot(hb, wsmall_ref[...])}
    xbc = [_dot(hb, wbig_ref[:, xbc_cols[0]])]
    next(gla1)
    xbc.append(_dot(hb, wbig_ref[:, xbc_cols[1]]))
    next(gla1)
    xbc.append(_dot(hb, wbig_ref[:, xbc_cols[2]]))
    env["xbc"] = jnp.concatenate(xbc, axis=1)
    ssd = _ssd_stages(env, live, cw_ref, cb_ref, dtb_ref, alog_ref, dexp_ref, snw_ref, triq_ref,
                      xpad, sstate, ys_ref)
    gla0 = _gla_stages(env, first, fresh, gw_ref, gb_ref, gnw_ref, tric_ref, gstate, yg0_s)
    env["q"] = _project(hb, wbig_ref, _Q0, GLA_K_WIDTH)
    next(gla1)
    env["k"] = _project(hb, wbig_ref, _K0, GLA_K_WIDTH)
    next(ssd)
    env["v"] = _project(hb, wbig_ref, _V0, GLA_V_WIDTH).astype(BF16)
    next(gla0)
    next(ssd)
    env["go"] = _project(hb, wbig_ref, _GO0, GLA_V_WIDTH)
    next(gla0)
    next(ssd)
    env["z"] = _project(hb, wbig_ref, _Z0, SSD_WIDTH)
    _prenorm(xn_ref, modn_ref, npre_ref, hb_s)
    next(gla0)
    next(ssd)
    next(ssd)
    q_s[...] = env["q"][second, :]
    k_s[...] = env["k"][second, :]
    v_s[...] = env["v"][second, :]
    go_s[...] = env["go"][second, :]
    small_s[...] = env["small"][second, :]


def _mixer_call(x, mod, npre, wt, offs, ssd_consts, gla_consts):
    b, s, d = x.shape
    t = SSD_CHUNK
    assert t == 2 * GLA_CHUNK
    tiles_per_seq = s // t
    n = b * tiles_per_seq
    xf = x.reshape(b * s, d)
    nxt = lambda i: jnp.minimum(i + 1, n - 1)
    cur = lambda i: (jnp.minimum(i, n - 1), 0)
    done = lambda i: (jnp.maximum(i - 1, 0), 0)
    consts = (npre,) + tuple(ssd_consts) + tuple(gla_consts)
    offs = tuple(int(v) for v in offs)
    skip = (offs[3] - offs[2]) + (offs[7] - offs[6])
    ys, yg = pl.pallas_call(
        functools.partial(_mixer_kernel, tiles_per_seq=tiles_per_seq, offs=offs),
        grid=(n + 1,),
        in_specs=[pl.BlockSpec((t, d), lambda i: (0, 0)),
                  pl.BlockSpec((None, ADA_CHUNKS, d), lambda i: (0, 0, 0)),
                  pl.BlockSpec((t, d), lambda i: (nxt(i), 0)),
                  pl.BlockSpec((None, ADA_CHUNKS, d), lambda i: (nxt(i) // tiles_per_seq, 0, 0)),
                  pl.BlockSpec(memory_space=pl.ANY)]
        + [_resident(a.shape) for a in consts],
        out_specs=[pl.BlockSpec((t, SSD_WIDTH), cur),
                   pl.BlockSpec((t, GLA_V_WIDTH), done)],
        out_shape=[jax.ShapeDtypeStruct((b * s, SSD_WIDTH), BF16),
                   jax.ShapeDtypeStruct((b * s, GLA_V_WIDTH), BF16)],
        scratch_shapes=[
            pltpu.VMEM((HALO + SSD_CHUNK, SSD_CONV_DIM), F32),
            pltpu.VMEM((SSD_GROUPS, SSD_STATE, GROUP_W), F32),
            pltpu.VMEM((GLA_HEADS, GLA_HEAD_V, GLA_HEAD_K), F32),
            pltpu.VMEM((t, d), BF16),
            pltpu.VMEM((GLA_CHUNK, GLA_V_WIDTH), BF16),
            pltpu.VMEM((GLA_CHUNK, GLA_K_WIDTH), F32),
            pltpu.VMEM((GLA_CHUNK, GLA_K_WIDTH), F32),
            pltpu.VMEM((GLA_CHUNK, GLA_V_WIDTH), BF16),
            pltpu.VMEM((GLA_CHUNK, GLA_V_WIDTH), F32),
            pltpu.VMEM((GLA_CHUNK, SMALL_W), F32),
            pltpu.VMEM((d, _PROJ_W), BF16),
            pltpu.VMEM((d, SMALL_W), BF16),
            pltpu.VMEM((2, _PROJ_CHUNK, d), F32),
            pltpu.VMEM((skip, d), F32),
            pltpu.SemaphoreType.DMA((4,)),
        ],
        compiler_params=_params("arbitrary"),
        name="token_mixer",
    )(xf, mod, xf, mod, wt, *consts)
    return ys.reshape(b, s, SSD_WIDTH), yg.reshape(b, s, GLA_V_WIDTH)


_UP_CHUNK = 384
_UP_DOT = 768
_FFN_TM = 256
_CAST_ROWS = 128


def _cast_weight(w_hbm, dst, stage, sem):
    rows = stage.shape[1]
    n = w_hbm.shape[0] // rows

    def copy(c):
        return pltpu.make_async_copy(w_hbm.at[pl.ds(c * rows, rows), :], stage.at[c % 2], sem.at[c % 2])

    copy(0).start()
    for c in range(n):
        if c + 1 < n:
            copy(c + 1).start()
        yield
        copy(c).wait()
        dst[c * rows:(c + 1) * rows, :] = stage[c % 2].astype(BF16)


def _stage_ffn_input(ys_ref, yg_ref, x_ref, mod_ref, npost1_ref, npre_ref, wout_ref, x1_s, hb_s):
    y = _dot(ys_ref[...], wout_ref[0:SSD_WIDTH, :]) + _dot(yg_ref[...], wout_ref[SSD_WIDTH:, :])
    yn = (y * _rms_scale(y)) * npost1_ref[...]
    x1 = x_ref[...] + mod_ref[2:3, :] * yn
    x1_s[...] = x1
    h = ((x1 * _rms_scale(x1)) * npre_ref[...]) * (1.0 + mod_ref[4:5, :]) + mod_ref[3:4, :]
    hb_s[...] = h.astype(BF16)


def _ffn_kernel(ys0_ref, yg0_ref, x0_ref, mod0_ref, ysn_ref, ygn_ref, xn_ref, modn_ref, mod_ref,
                npost1_ref, npre_ref, wout_hbm, wup_hbm, cw_ref, cb_ref, npost2_ref, wdown_hbm,
                o_ref, upad, gate, x1_s, hb_s,
                wout_ref, wup_ref, wdown_ref, st_out, st_up, st_down, sem, *, tiles_per_seq):
    tm = o_ref.shape[0]
    i = pl.program_id(0)

    @pl.when(i == 0)
    def _():
        casts = [_cast_weight(wout_hbm, wout_ref, st_out, sem.at[0]),
                 _cast_weight(wup_hbm, wup_ref, st_up, sem.at[1]),
                 _cast_weight(wdown_hbm, wdown_ref, st_down, sem.at[2])]
        while casts:
            casts = [g for g in casts if next(g, True) is None]
        _stage_ffn_input(ys0_ref, yg0_ref, x0_ref, mod0_ref, npost1_ref, npre_ref, wout_ref, x1_s, hb_s)

    @pl.when(lax.rem(i, tiles_per_seq) == 0)
    def _():
        upad[0:HALO, :] = jnp.zeros((HALO, upad.shape[1]), F32)

    x1 = x1_s[...]
    hb = hb_s[...]

    done = set()

    def up(col):
        p = col // _UP_DOT
        if p not in done:
            done.add(p)
            cols = slice(p * _UP_DOT, (p + 1) * _UP_DOT)
            upad[HALO:HALO + tm, cols] = _dot(hb, wup_ref[:, cols])

    def conv(c0):
        cols = slice(c0, c0 + _UP_CHUNK)
        up(c0)
        up(c0 + _UP_CHUNK - 1)
        ext = upad[:, cols]
        acc = cw_ref[0:1, cols] * ext
        for k in range(1, FFN_CONV):
            acc = pltpu.roll(acc, 1, 0) + cw_ref[k:k + 1, cols] * ext
        upad[0:HALO, cols] = upad[tm:tm + HALO, cols]
        return acc[HALO:HALO + tm, :] + cb_ref[:, cols]

    for c in range(0, FFN_HIDDEN, _UP_CHUNK):
        gate[:, c:c + _UP_CHUNK] = (_silu(conv(c)) * conv(FFN_HIDDEN + c)).astype(BF16)
    _stage_ffn_input(ysn_ref, ygn_ref, xn_ref, modn_ref, npost1_ref, npre_ref, wout_ref, x1_s, hb_s)
    f = _dot(gate[...], wdown_ref[...])
    fn = (f * _rms_scale(f)) * npost2_ref[...]
    o_ref[...] = x1 + mod_ref[5:6, :] * fn


def _ffn_call(ys, yg, x, mod, npost1, npre, wout, wup, cw, cb, npost2, wdown):
    b, s, d = x.shape
    tm = _FFN_TM
    tiles_per_seq = s // tm
    n = b * tiles_per_seq
    flat = lambda a: a.reshape(b * s, a.shape[-1])
    nxt = lambda i: jnp.minimum(i + 1, n - 1)
    first = lambda w: pl.BlockSpec((tm, w), lambda i: (0, 0))
    ahead = lambda w: pl.BlockSpec((tm, w), lambda i: (nxt(i), 0))
    mod_of = lambda tile: pl.BlockSpec((None, ADA_CHUNKS, d), lambda i: (tile(i) // tiles_per_seq, 0, 0))
    consts = (npost1, npre, wout, wup, cw, cb, npost2, wdown)
    in_hbm = pl.BlockSpec(memory_space=pl.ANY)
    out = pl.pallas_call(
        functools.partial(_ffn_kernel, tiles_per_seq=tiles_per_seq),
        grid=(n,),
        in_specs=[first(SSD_WIDTH), first(GLA_V_WIDTH), first(d), mod_of(lambda i: 0 * i),
                  ahead(SSD_WIDTH), ahead(GLA_V_WIDTH), ahead(d), mod_of(nxt), mod_of(lambda i: i)]
        + [in_hbm if any(a is w for w in (wout, wup, wdown)) else _resident(a.shape) for a in consts],
        out_specs=pl.BlockSpec((tm, d), lambda i: (i, 0)),
        out_shape=jax.ShapeDtypeStruct((b * s, d), F32),
        scratch_shapes=[pltpu.VMEM((HALO + tm, 2 * FFN_HIDDEN), F32),
                        pltpu.VMEM((tm, FFN_HIDDEN), BF16),
                        pltpu.VMEM((tm, d), F32), pltpu.VMEM((tm, d), BF16),
                        pltpu.VMEM(wout.shape, BF16), pltpu.VMEM(wup.shape, BF16), pltpu.VMEM(wdown.shape, BF16),
                        pltpu.VMEM((2, _CAST_ROWS, wout.shape[1]), F32),
                        pltpu.VMEM((2, _CAST_ROWS, wup.shape[1]), F32),
                        pltpu.VMEM((2, _CAST_ROWS, wdown.shape[1]), F32),
                        pltpu.SemaphoreType.DMA((3, 2))],
        compiler_params=_params("arbitrary"),
        name="channel_mixer",
    )(flat(ys), flat(yg), flat(x), mod, flat(ys), flat(yg), flat(x), mod, mod, *consts)
    return out.reshape(b, s, d)


def _lane_pad(v, width):
    return jnp.pad(v, ((0, 0), (0, width - v.shape[1])))


def _constants():
    q, c = SSD_CHUNK, GLA_CHUNK
    tri_q = np.tril(np.ones((q, q), np.float32))
    tri_c = np.tril(np.ones((c, c), np.float32))
    return (jnp.asarray(np.tile(tri_q, (1, 3)), BF16),
            jnp.asarray(np.tile(tri_c, (1, 3)), BF16))


def kernel(x, c, w_ada, b_ada, norm_mix_pre, norm_mix_post, norm_ffn_pre, norm_ffn_post, w_in, ssd_conv_w, ssd_conv_b, ssd_dt_bias, ssd_a_log, ssd_d, ssd_norm, gla_gate_w, gla_gate_b, gla_norm, w_out, ffn_up, ffn_conv_w, ffn_conv_b, ffn_down):
    bsz, seqlen, d = x.shape
    depth = w_ada.shape[0]
    tri_q3, tri_c3 = _constants()
    c_pad = jnp.pad(c, ((0, SUBLANES - bsz % SUBLANES), (0, 0))) if bsz % SUBLANES else c

    sizes = [SSD_WIDTH, SSD_CONV_DIM, SSD_HEADS, GLA_K_WIDTH, GLA_K_WIDTH, GLA_V_WIDTH,
             GLA_GATE_RANK, GLA_V_WIDTH]
    o = np.concatenate([[0], np.cumsum(sizes)])

    for i in range(depth):
        mod = _ada_call(c_pad, w_ada[i], b_ada[i][None, :])[:bsz].reshape(bsz, ADA_CHUNKS, d)

        ssd_consts = (ssd_conv_w[i], ssd_conv_b[i][None, :],
                      _lane_pad(ssd_dt_bias[i][None, :], SMALL_W), _lane_pad(ssd_a_log[i][None, :], SMALL_W),
                      jnp.repeat(ssd_d[i], SSD_HEAD_DIM)[None, :], ssd_norm[i][None, :], tri_q3)
        gw = jnp.zeros((SMALL_W, GLA_K_WIDTH), F32).at[SSD_HEADS:SSD_HEADS + GLA_GATE_RANK].set(
            gla_gate_w[i]).astype(BF16)
        gla_consts = (gw, gla_gate_b[i][None, :], gla_norm[i][None, :], tri_c3)
        y_ssd, y_gla = _mixer_call(x, mod, norm_mix_pre[i][None, :], jnp.swapaxes(w_in[i], 0, 1), o,
                                   ssd_consts, gla_consts)

        x = _ffn_call(y_ssd, y_gla, x, mod, norm_mix_post[i][None, :], norm_ffn_pre[i][None, :],
                      w_out[i], ffn_up[i], ffn_conv_w[i], ffn_conv_b[i][None, :],
                      norm_ffn_post[i][None, :], ffn_down[i])
    return x
```

```python
import functools

import numpy as np
import jax
import jax.numpy as jnp
from jax import lax
from jax.experimental import pallas as pl
from jax.experimental.pallas import tpu as pltpu

F32 = jnp.float32
BF16 = jnp.bfloat16

D_MODEL = 1024
MIX_WIDTH = 2 * D_MODEL
SSD_WIDTH = MIX_WIDTH // 2
GLA_V_WIDTH = MIX_WIDTH - SSD_WIDTH
SSD_HEAD_DIM = 64
SSD_HEADS = SSD_WIDTH // SSD_HEAD_DIM
SSD_GROUPS = 2
SSD_STATE = 128
SSD_CONV = 4
SSD_CHUNK = 128
SSD_CONV_DIM = SSD_WIDTH + 2 * SSD_GROUPS * SSD_STATE
GLA_HEADS = 4
GLA_K_WIDTH = GLA_V_WIDTH // 2
GLA_HEAD_K = GLA_K_WIDTH // GLA_HEADS
GLA_HEAD_V = GLA_V_WIDTH // GLA_HEADS
GLA_GATE_RANK = 16
GLA_GATE_NORM = 16.0
GLA_CHUNK = 64
FFN_HIDDEN = int(round(8 * D_MODEL / 3 / 128)) * 128
FFN_CONV = 3
ADA_CHUNKS = 6
NORM_EPS = 1e-6
_LOG2E = float(np.log2(np.e))

LANES = 128
SUBLANES = 8
VMEM_LIMIT_BYTES = 58 * 1024 * 1024

SMALL_W = LANES
GROUP_W = SSD_WIDTH // SSD_GROUPS
HEADS_PER_GROUP = SSD_HEADS // SSD_GROUPS
HALO = SUBLANES


def _silu(x):
    return x * jax.nn.sigmoid(x)


def _softplus(x):
    return jnp.maximum(x, 0.0) + jnp.log(1.0 + jnp.exp(-jnp.abs(x)))


def _rms_scale(x):
    return lax.rsqrt(jnp.mean(x * x, axis=-1, keepdims=True) + NORM_EPS)


def _split3(x):
    hi = x.astype(BF16)
    r1 = x - hi.astype(F32)
    mid = r1.astype(BF16)
    lo = (r1 - mid.astype(F32)).astype(BF16)
    return hi, mid, lo


def _dot(a, b):
    return jnp.dot(a, b, preferred_element_type=F32)


def _dot_nt(a, b):
    return lax.dot_general(a, b, (((1,), (1,)), ((), ())), preferred_element_type=F32)


def _dot_tn(a, b):
    return lax.dot_general(a, b, (((0,), (0,)), ((), ())), preferred_element_type=F32)


def _resident(shape):
    nd = len(shape)
    return pl.BlockSpec(shape, lambda *_: (0,) * nd, pipeline_mode=pl.Buffered(1))


def _params(*sem):
    return pltpu.CompilerParams(dimension_semantics=sem, vmem_limit_bytes=VMEM_LIMIT_BYTES)


def _ada_mod(c_ref, b_ref, w_hbm, mod_ref, stage, sem):
    d = c_ref.shape[1]
    ca = _silu(c_ref[...]).astype(BF16)

    def copy(k):
        return pltpu.make_async_copy(w_hbm.at[:, pl.ds(k * d, d)], stage.at[k % 2], sem.at[k % 2])

    copy(0).start()
    for k in range(ADA_CHUNKS):
        if k + 1 < ADA_CHUNKS:
            copy(k + 1).start()
        yield
        copy(k).wait()
        mod_ref[k] = _dot(ca, stage[k % 2].astype(BF16)) + b_ref[:, k * d:(k + 1) * d]


def _mod_row(mod_ref, k, b):
    return mod_ref[k, pl.ds(b, 1), :]


_Z0 = 0
_XBC0 = _Z0 + SSD_WIDTH
_Q0 = _XBC0 + SSD_CONV_DIM
_K0 = _Q0 + GLA_K_WIDTH
_V0 = _K0 + GLA_K_WIDTH
_GO0 = _V0 + GLA_V_WIDTH
_PROJ_W = _GO0 + GLA_V_WIDTH
_PROJ_CHUNK = 512


def _load_projection(wt_hbm, wbig_ref, wsmall_ref, stage, small_stage, sem, offs):
    d = wt_hbm.shape[1]
    skip_dt = offs[3] - offs[2]
    skip_glr = offs[7] - offs[6]
    n = _PROJ_W // _PROJ_CHUNK

    def copy(j):
        dst = j * _PROJ_CHUNK
        src = dst + (skip_dt if dst >= offs[2] else 0) + (skip_glr if dst >= offs[6] - skip_dt else 0)
        return pltpu.make_async_copy(wt_hbm.at[pl.ds(src, _PROJ_CHUNK), :], stage.at[j % 2], sem.at[j % 2])

    small_copies = [
        pltpu.make_async_copy(wt_hbm.at[pl.ds(offs[2], skip_dt), :], small_stage.at[pl.ds(0, skip_dt), :], sem.at[2]),
        pltpu.make_async_copy(wt_hbm.at[pl.ds(offs[6], skip_glr), :], small_stage.at[pl.ds(skip_dt, skip_glr), :],
                              sem.at[3])]
    for cp in small_copies:
        cp.start()
    copy(0).start()
    for j in range(n):
        if j + 1 < n:
            copy(j + 1).start()
        yield
        copy(j).wait()
        wbig_ref[:, j * _PROJ_CHUNK:(j + 1) * _PROJ_CHUNK] = stage[j % 2].T.astype(BF16)
    for cp in small_copies:
        cp.wait()
    pad = jnp.zeros((SMALL_W - skip_dt - skip_glr, d), F32)
    wsmall_ref[...] = jnp.concatenate([small_stage[...], pad], axis=0).T.astype(BF16)


def _project(hb, w_ref, c0, width):
    parts = [_dot(hb, w_ref[:, c:c + min(_PROJ_CHUNK, c0 + width - c)])
             for c in range(c0, c0 + width, _PROJ_CHUNK)]
    return parts[0] if len(parts) == 1 else jnp.concatenate(parts, axis=1)


def _ssd_stages(env, rows, cw_ref, cb_ref, dtb_ref, alog_ref, dexp_ref, nw_ref, tri_ref, xpad, state, y_ref):
    q = SSD_CHUNK
    n = SSD_STATE

    xpad[HALO:HALO + q, :] = env["xbc"]
    ext = xpad[...]
    acc = cw_ref[0:1, :] * ext
    for k in range(1, SSD_CONV):
        acc = pltpu.roll(acc, 1, 0) + cw_ref[k:k + 1, :] * ext
    acc = acc[HALO:HALO + q, :] + cb_ref[...]
    xpad[0:HALO, :] = xpad[q:q + HALO, :]
    xc = _silu(acc)
    xs = xc[:, :SSD_WIDTH]
    bm = xc[:, SSD_WIDTH:SSD_WIDTH + SSD_GROUPS * n].astype(BF16)
    cm = xc[:, SSD_WIDTH + SSD_GROUPS * n:].astype(BF16)
    yield

    lane = lax.broadcasted_iota(jnp.int32, (q, SMALL_W), 1)
    head_lane = lane < SSD_HEADS
    dt = _softplus(env["small"] + dtb_ref[...])
    a = -jnp.exp(alog_ref[...]) * _LOG2E
    dt = jnp.where(head_lane, dt, 0.0)
    da = jnp.where(head_lane, dt * a, 0.0)
    cs = _dot(tri_ref[...], jnp.concatenate(_split3(da), axis=0))
    cs_t = cs.T
    half = (lane >= SSD_HEAD_DIM).astype(jnp.int32)

    def expand(v):
        return jnp.concatenate([jnp.take_along_axis(v, 2 * j + half, axis=1)
                                for j in range(SSD_WIDTH // LANES)], axis=1)

    cs_e = expand(cs)
    dt_e = expand(dt)
    cs_last = cs_e[q - 1:q, :]

    xdt = xs * dt_e
    xdt_b = xdt.astype(BF16)
    xdec_b = (xdt * jnp.exp2(cs_last - cs_e)).astype(BF16)
    decay_in = jnp.exp2(cs_e)
    decay_chunk = jnp.exp2(cs_last)
    yield

    row = lax.broadcasted_iota(jnp.int32, (q, q), 0)
    col = lax.broadcasted_iota(jnp.int32, (q, q), 1)
    causal = row >= col
    first_half = col < SSD_HEAD_DIM

    y_parts = []
    for g in range(SSD_GROUPS):
        cmg = cm[:, g * n:(g + 1) * n]
        bmg = bm[:, g * n:(g + 1) * n]
        gs = slice(g * GROUP_W, (g + 1) * GROUP_W)
        scores = _dot_nt(cmg, bmg)
        y_off = _dot(cmg, state[g].astype(BF16)) * decay_in[:, gs]
        state[g] = decay_chunk[:, gs] * state[g] + _dot_tn(bmg, xdec_b[:, gs])
        diag = []
        for pair in range(HEADS_PER_GROUP // 2):
            h0 = g * HEADS_PER_GROUP + 2 * pair
            ms = []
            for h in (h0, h0 + 1):
                seg = cs[:, h:h + 1] - cs_t[h:h + 1, :]
                decay = jnp.exp2(jnp.where(causal, seg, -jnp.inf))
                ms.append((scores * decay).astype(BF16))
            slab = xdt_b[:, h0 * SSD_HEAD_DIM:(h0 + 2) * SSD_HEAD_DIM]
            zero = jnp.zeros_like(slab)
            rhs = jnp.concatenate([jnp.where(first_half, slab, zero),
                                   jnp.where(first_half, zero, slab)], axis=0)
            diag.append(_dot(jnp.concatenate(ms, axis=1), rhs))
        y_parts.append(jnp.concatenate(diag, axis=1) + y_off)
        yield
    y = jnp.concatenate(y_parts, axis=1) + xs * dexp_ref[...]
    y = y * _silu(env["z"])
    outs = []
    for g in range(SSD_GROUPS):
        yg = y[:, g * GROUP_W:(g + 1) * GROUP_W]
        outs.append(yg * _rms_scale(yg))
    y_ref[rows, :] = (jnp.concatenate(outs, axis=1) * nw_ref[...]).astype(BF16)
    yield


def _gla_stages(env, r, fresh, gw_ref, gb_ref, nw_ref, tri_ref, state, y_ref):
    c = GLA_CHUNK
    dk, dv = GLA_HEAD_K, GLA_HEAD_V
    row = lax.broadcasted_iota(jnp.int32, (c, c), 0)
    col = lax.broadcasted_iota(jnp.int32, (c, c), 1)
    causal = row >= col

    qf = env["q"][r, :] * (dk ** -0.5)
    kf = env["k"][r, :]
    logit = _dot(env["small"][r, :].astype(BF16), gw_ref[...]) + gb_ref[...]
    lg = -_softplus(-logit) * (_LOG2E / GLA_GATE_NORM)
    bcum = _dot(tri_ref[...], jnp.concatenate(_split3(lg), axis=0))
    blast = bcum[c - 1:c, :]
    qt = (qf * jnp.exp2(bcum)).astype(BF16)
    kt = (kf * jnp.exp2(-bcum)).astype(BF16)
    kd = (kf * jnp.exp2(blast - bcum)).astype(BF16)
    eblast = jnp.exp2(blast)
    yield
    for h in range(GLA_HEADS):
        ks = slice(h * dk, (h + 1) * dk)
        vs = slice(h * dv, (h + 1) * dv)
        attn = jnp.where(causal, _dot_nt(qt[:, ks], kt[:, ks]), 0.0)
        vh = env["v"][r, vs]
        st = jnp.where(fresh, 0.0, state[h])
        o = _dot(attn.astype(BF16), vh) + _dot_nt(qt[:, ks], st.astype(BF16))
        state[h] = eblast[:, ks] * st + _dot_tn(vh, kd[:, ks])
        o = o * _rms_scale(o) * nw_ref[...]
        o = o * _silu(env["go"][r, vs])
        y_ref[:, vs] = o.astype(BF16)
        if h % 2 == 1:
            yield


def _gla_levels(env, rows, gw_ref, gb_ref, nw_ref, tri_ref, state, y_ref):
    c = GLA_CHUNK
    dk, dv = GLA_HEAD_K, GLA_HEAD_V
    row = lax.broadcasted_iota(jnp.int32, (c, c), 0)
    col = lax.broadcasted_iota(jnp.int32, (c, c), 1)
    causal = row >= col
    heads = range(GLA_HEADS)

    logit = _dot(env["small"].astype(BF16), gw_ref[...]) + gb_ref[...]
    lg = -_softplus(-logit) * (_LOG2E / GLA_GATE_NORM)
    yield
    bcum = _dot(tri_ref[...], jnp.concatenate(_split3(lg), axis=0))
    blast = bcum[c - 1:c, :]
    qf = env["q"] * (dk ** -0.5)
    kf = env["k"]
    qt = (qf * jnp.exp2(bcum)).astype(BF16)
    kt = (kf * jnp.exp2(-bcum)).astype(BF16)
    kd = (kf * jnp.exp2(blast - bcum)).astype(BF16)
    eblast = jnp.exp2(blast)
    yield
    ks = [slice(h * dk, (h + 1) * dk) for h in heads]
    vs = [slice(h * dv, (h + 1) * dv) for h in heads]
    attn = [jnp.where(causal, _dot_nt(qt[:, ks[h]], kt[:, ks[h]]), 0.0).astype(BF16) for h in heads]
    yield
    o = [_dot(attn[h], env["v"][:, vs[h]]) + _dot_nt(qt[:, ks[h]], state[h].astype(BF16)) for h in heads]
    for h in heads:
        state[h] = eblast[:, ks[h]] * state[h] + _dot_tn(env["v"][:, vs[h]], kd[:, ks[h]])
    for h in heads:
        oh = o[h] * _rms_scale(o[h]) * nw_ref[...]
        y_ref[rows, vs[h]] = (oh * _silu(env["go"][:, vs[h]])).astype(BF16)
    yield


def _prenorm(x_ref, mod_ref, b, npre_ref, hb_s):
    x = x_ref[...]
    xn = (x * _rms_scale(x)) * npre_ref[...]
    hb_s[...] = (xn * (1.0 + _mod_row(mod_ref, 1, b)) + _mod_row(mod_ref, 0, b)).astype(BF16)


def _mixer_kernel(x0_ref, xn_ref, c_ref, bada_ref, wada_hbm, wt_hbm, npre_ref,
                  cw_ref, cb_ref, dtb_ref, alog_ref, dexp_ref, snw_ref, triq_ref,
                  gw_ref, gb_ref, gnw_ref, tric_ref,
                  ys_ref, yg_ref, mod_ref, xpad, sstate, gstate, hb_s,
                  yg0_s, q_s, k_s, v_s, go_s, small_s,
                  wbig_ref, wsmall_ref, w_stage, wsmall_stage, w_sem, a_stage, a_sem,
                  *, tiles_per_seq, n_tiles, offs):
    i = pl.program_id(0)
    fresh = lax.rem(i, tiles_per_seq) == 0

    @pl.when(i == 0)
    def _():
        loads = [_ada_mod(c_ref, bada_ref, wada_hbm, mod_ref, a_stage, a_sem),
                 _load_projection(wt_hbm, wbig_ref, wsmall_ref, w_stage, wsmall_stage, w_sem, offs)]
        while loads:
            loads = [g for g in loads if next(g, True) is None]
        _prenorm(x0_ref, mod_ref, 0, npre_ref, hb_s)
        gstate[...] = jnp.zeros(gstate.shape, F32)
        for ref in (yg0_s, q_s, k_s, v_s, go_s, small_s):
            ref[...] = jnp.zeros(ref.shape, ref.dtype)

    @pl.when(fresh)
    def _():
        xpad[0:HALO, :] = jnp.zeros((HALO, SSD_CONV_DIM), F32)
        sstate[...] = jnp.zeros(sstate.shape, F32)

    hb = hb_s[...]
    first = slice(0, GLA_CHUNK)
    second = slice(GLA_CHUNK, 2 * GLA_CHUNK)
    xbc_cols = [slice(c, c + _PROJ_CHUNK) for c in range(_XBC0, _XBC0 + SSD_CONV_DIM, _PROJ_CHUNK)]
    prev = {"q": q_s[...], "k": k_s[...], "v": v_s[...], "go": go_s[...], "small": small_s[...]}
    off = pl.multiple_of(lax.rem(i + 1, 2) * SSD_CHUNK, SSD_CHUNK)
    gla1 = _gla_levels(prev, pl.ds(off + GLA_CHUNK, GLA_CHUNK), gw_ref, gb_ref, gnw_ref, tric_ref, gstate, yg_ref)
    yg_ref[pl.ds(off, GLA_CHUNK), :] = yg0_s[...]
    next(gla1)
    env = {"small": _dot(hb, wsmall_ref[...])}
    xbc = [_dot(hb, wbig_ref[:, xbc_cols[0]])]
    next(gla1)
    xbc.append(_dot(hb, wbig_ref[:, xbc_cols[1]]))
    next(gla1)
    xbc.append(_dot(hb, wbig_ref[:, xbc_cols[2]]))
    env["xbc"] = jnp.concatenate(xbc, axis=1)
    own = pl.ds(pl.multiple_of(lax.rem(i, 2) * SSD_CHUNK, SSD_CHUNK), SSD_CHUNK)
    ssd = _ssd_stages(env, own, cw_ref, cb_ref, dtb_ref, alog_ref, dexp_ref, snw_ref, triq_ref,
                      xpad, sstate, ys_ref)
    gla0 = _gla_stages(env, first, fresh, gw_ref, gb_ref, gnw_ref, tric_ref, gstate, yg0_s)
    env["q"] = _project(hb, wbig_ref, _Q0, GLA_K_WIDTH)
    next(gla1)
    env["k"] = _project(hb, wbig_ref, _K0, GLA_K_WIDTH)
    next(ssd)
    env["v"] = _project(hb, wbig_ref, _V0, GLA_V_WIDTH).astype(BF16)
    next(gla0)
    next(ssd)
    env["go"] = _project(hb, wbig_ref, _GO0, GLA_V_WIDTH)
    next(gla0)
    next(ssd)
    env["z"] = _project(hb, wbig_ref, _Z0, SSD_WIDTH)
    b_next = jnp.minimum(i + 1, n_tiles - 1) // tiles_per_seq
    _prenorm(xn_ref, mod_ref, b_next, npre_ref, hb_s)
    next(gla0)
    next(ssd)
    next(ssd)
    q_s[...] = env["q"][second, :]
    k_s[...] = env["k"][second, :]
    v_s[...] = env["v"][second, :]
    go_s[...] = env["go"][second, :]
    small_s[...] = env["small"][second, :]

    @pl.when(i == n_tiles - 1)
    def _():
        last = {"q": q_s[...], "k": k_s[...], "v": v_s[...], "go": go_s[...], "small": small_s[...]}
        yg_ref[pl.ds(SSD_CHUNK, GLA_CHUNK), :] = yg0_s[...]
        for _ in _gla_levels(last, pl.ds(SSD_CHUNK + GLA_CHUNK, GLA_CHUNK), gw_ref, gb_ref, gnw_ref, tric_ref,
                             gstate, yg_ref):
            pass


def _mixer_call(x, c_pad, wada, bada, npre, wt, offs, ssd_consts, gla_consts):
    b, s, d = x.shape
    t = SSD_CHUNK
    assert t == 2 * GLA_CHUNK
    tiles_per_seq = s // t
    n = b * tiles_per_seq
    assert n % 2 == 0
    xf = x.reshape(b * s, d)
    nxt = lambda i: jnp.minimum(i + 1, n - 1)
    done = lambda i: (jnp.maximum(i - 1, 0) // 2, 0)
    consts = (npre,) + tuple(ssd_consts) + tuple(gla_consts)
    offs = tuple(int(v) for v in offs)
    skip = (offs[3] - offs[2]) + (offs[7] - offs[6])
    mod_shape = (ADA_CHUNKS, c_pad.shape[0], d)
    ys, yg, mod = pl.pallas_call(
        functools.partial(_mixer_kernel, tiles_per_seq=tiles_per_seq, n_tiles=n, offs=offs),
        grid=(n,),
        in_specs=[pl.BlockSpec((t, d), lambda i: (0, 0)),
                  pl.BlockSpec((t, d), lambda i: (nxt(i), 0)),
                  _resident(c_pad.shape), _resident(bada.shape),
                  pl.BlockSpec(memory_space=pl.ANY), pl.BlockSpec(memory_space=pl.ANY)]
        + [_resident(a.shape) for a in consts],
        out_specs=[pl.BlockSpec((2 * t, SSD_WIDTH), lambda i: (i // 2, 0)),
                   pl.BlockSpec((2 * t, GLA_V_WIDTH), done),
                   pl.BlockSpec(mod_shape, lambda i: (0, 0, 0))],
        out_shape=[jax.ShapeDtypeStruct((b * s, SSD_WIDTH), BF16),
                   jax.ShapeDtypeStruct((b * s, GLA_V_WIDTH), BF16),
                   jax.ShapeDtypeStruct(mod_shape, F32)],
        scratch_shapes=[
            pltpu.VMEM((HALO + SSD_CHUNK, SSD_CONV_DIM), F32),
            pltpu.VMEM((SSD_GROUPS, SSD_STATE, GROUP_W), F32),
            pltpu.VMEM((GLA_HEADS, GLA_HEAD_V, GLA_HEAD_K), F32),
            pltpu.VMEM((t, d), BF16),
            pltpu.VMEM((GLA_CHUNK, GLA_V_WIDTH), BF16),
            pltpu.VMEM((GLA_CHUNK, GLA_K_WIDTH), F32),
            pltpu.VMEM((GLA_CHUNK, GLA_K_WIDTH), F32),
            pltpu.VMEM((GLA_CHUNK, GLA_V_WIDTH), BF16),
            pltpu.VMEM((GLA_CHUNK, GLA_V_WIDTH), F32),
            pltpu.VMEM((GLA_CHUNK, SMALL_W), F32),
            pltpu.VMEM((d, _PROJ_W), BF16),
            pltpu.VMEM((d, SMALL_W), BF16),
            pltpu.VMEM((2, _PROJ_CHUNK, d), F32),
            pltpu.VMEM((skip, d), F32),
            pltpu.SemaphoreType.DMA((4,)),
            pltpu.VMEM((2, d, d), F32),
            pltpu.SemaphoreType.DMA((2,)),
        ],
        compiler_params=_params("arbitrary"),
        name="token_mixer",
    )(xf, xf, c_pad, bada, wada, wt, *consts)
    return ys.reshape(b, s, SSD_WIDTH), yg.reshape(b, s, GLA_V_WIDTH), mod


_UP_CHUNK = 384
_UP_DOT = 768
_FFN_TM = 256
_CAST_ROWS = 128


def _cast_weight(w_hbm, dst, stage, sem):
    rows = stage.shape[1]
    n = w_hbm.shape[0] // rows

    def copy(c):
        return pltpu.make_async_copy(w_hbm.at[pl.ds(c * rows, rows), :], stage.at[c % 2], sem.at[c % 2])

    copy(0).start()
    for c in range(n):
        if c + 1 < n:
            copy(c + 1).start()
        yield
        copy(c).wait()
        dst[c * rows:(c + 1) * rows, :] = stage[c % 2].astype(BF16)


def _stage_ffn_input(ys_ref, yg_ref, x_ref, mod_ref, b, npost1_ref, npre_ref, wout_ref, x1_s, hb_s):
    y = _dot(ys_ref[...], wout_ref[0:SSD_WIDTH, :]) + _dot(yg_ref[...], wout_ref[SSD_WIDTH:, :])
    yn = (y * _rms_scale(y)) * npost1_ref[...]
    x1 = x_ref[...] + _mod_row(mod_ref, 2, b) * yn
    x1_s[...] = x1
    h = ((x1 * _rms_scale(x1)) * npre_ref[...]) * (1.0 + _mod_row(mod_ref, 4, b)) + _mod_row(mod_ref, 3, b)
    hb_s[...] = h.astype(BF16)


def _ffn_kernel(ys0_ref, yg0_ref, x0_ref, ysn_ref, ygn_ref, xn_ref, mod_ref,
                npost1_ref, npre_ref, wout_hbm, wup_hbm, cw_ref, cb_ref, npost2_ref, wdown_hbm,
                o_ref, upad, gate, x1_s, hb_s,
                wout_ref, wup_ref, wdown_ref, st_out, st_up, st_down, sem, *, tiles_per_seq, n_tiles):
    tm = o_ref.shape[0]
    i = pl.program_id(0)

    @pl.when(i == 0)
    def _():
        casts = [_cast_weight(wout_hbm, wout_ref, st_out, sem.at[0]),
                 _cast_weight(wup_hbm, wup_ref, st_up, sem.at[1]),
                 _cast_weight(wdown_hbm, wdown_ref, st_down, sem.at[2])]
        while casts:
            casts = [g for g in casts if next(g, True) is None]
        _stage_ffn_input(ys0_ref, yg0_ref, x0_ref, mod_ref, 0, npost1_ref, npre_ref, wout_ref, x1_s, hb_s)

    @pl.when(lax.rem(i, tiles_per_seq) == 0)
    def _():
        upad[0:HALO, :] = jnp.zeros((HALO, upad.shape[1]), F32)

    x1 = x1_s[...]
    hb = hb_s[...]

    done = set()

    def up(col):
        p = col // _UP_DOT
        if p not in done:
            done.add(p)
            cols = slice(p * _UP_DOT, (p + 1) * _UP_DOT)
            upad[HALO:HALO + tm, cols] = _dot(hb, wup_ref[:, cols])

    def conv(c0):
        cols = slice(c0, c0 + _UP_CHUNK)
        up(c0)
        up(c0 + _UP_CHUNK - 1)
        ext = upad[:, cols]
        acc = cw_ref[0:1, cols] * ext
        for k in range(1, FFN_CONV):
            acc = pltpu.roll(acc, 1, 0) + cw_ref[k:k + 1, cols] * ext
        upad[0:HALO, cols] = upad[tm:tm + HALO, cols]
        return acc[HALO:HALO + tm, :] + cb_ref[:, cols]

    for c in range(0, FFN_HIDDEN, _UP_CHUNK):
        gate[:, c:c + _UP_CHUNK] = (_silu(conv(c)) * conv(FFN_HIDDEN + c)).astype(BF16)
    b_next = jnp.minimum(i + 1, n_tiles - 1) // tiles_per_seq
    _stage_ffn_input(ysn_ref, ygn_ref, xn_ref, mod_ref, b_next, npost1_ref, npre_ref, wout_ref, x1_s, hb_s)
    f = _dot(gate[...], wdown_ref[...])
    fn = (f * _rms_scale(f)) * npost2_ref[...]
    o_ref[...] = x1 + _mod_row(mod_ref, 5, i // tiles_per_seq) * fn


def _ffn_call(ys, yg, x, mod, npost1, npre, wout, wup, cw, cb, npost2, wdown):
    b, s, d = x.shape
    tm = _FFN_TM
    tiles_per_seq = s // tm
    n = b * tiles_per_seq
    flat = lambda a: a.reshape(b * s, a.shape[-1])
    nxt = lambda i: jnp.minimum(i + 1, n - 1)
    first = lambda w: pl.BlockSpec((tm, w), lambda i: (0, 0))
    ahead = lambda w: pl.BlockSpec((tm, w), lambda i: (nxt(i), 0))
    consts = (npost1, npre, wout, wup, cw, cb, npost2, wdown)
    in_hbm = pl.BlockSpec(memory_space=pl.ANY)
    out = pl.pallas_call(
        functools.partial(_ffn_kernel, tiles_per_seq=tiles_per_seq, n_tiles=n),
        grid=(n,),
        in_specs=[first(SSD_WIDTH), first(GLA_V_WIDTH), first(d),
                  ahead(SSD_WIDTH), ahead(GLA_V_WIDTH), ahead(d), _resident(mod.shape)]
        + [in_hbm if any(a is w for w in (wout, wup, wdown)) else _resident(a.shape) for a in consts],
        out_specs=pl.BlockSpec((tm, d), lambda i: (i, 0)),
        out_shape=jax.ShapeDtypeStruct((b * s, d), F32),
        scratch_shapes=[pltpu.VMEM((HALO + tm, 2 * FFN_HIDDEN), F32),
                        pltpu.VMEM((tm, FFN_HIDDEN), BF16),
                        pltpu.VMEM((tm, d), F32), pltpu.VMEM((tm, d), BF16),
                        pltpu.VMEM(wout.shape, BF16), pltpu.VMEM(wup.shape, BF16), pltpu.VMEM(wdown.shape, BF16),
                        pltpu.VMEM((2, _CAST_ROWS, wout.shape[1]), F32),
                        pltpu.VMEM((2, _CAST_ROWS, wup.shape[1]), F32),
                        pltpu.VMEM((2, _CAST_ROWS, wdown.shape[1]), F32),
                        pltpu.SemaphoreType.DMA((3, 2))],
        compiler_params=_params("arbitrary"),
        name="channel_mixer",
    )(flat(ys), flat(yg), flat(x), flat(ys), flat(yg), flat(x), mod, *consts)
    return out.reshape(b, s, d)


def _lane_pad(v, width):
    return jnp.pad(v, ((0, 0), (0, width - v.shape[1])))


def _constants():
    q, c = SSD_CHUNK, GLA_CHUNK
    tri_q = np.tril(np.ones((q, q), np.float32))
    tri_c = np.tril(np.ones((c, c), np.float32))
    return (jnp.asarray(np.tile(tri_q, (1, 3)), BF16),
            jnp.asarray(np.tile(tri_c, (1, 3)), BF16))


def kernel(x, c, w_ada, b_ada, norm_mix_pre, norm_mix_post, norm_ffn_pre, norm_ffn_post, w_in, ssd_conv_w, ssd_conv_b, ssd_dt_bias, ssd_a_log, ssd_d, ssd_norm, gla_gate_w, gla_gate_b, gla_norm, w_out, ffn_up, ffn_conv_w, ffn_conv_b, ffn_down):
    bsz, seqlen, d = x.shape
    depth = w_ada.shape[0]
    tri_q3, tri_c3 = _constants()
    c_pad = jnp.pad(c, ((0, SUBLANES - bsz % SUBLANES), (0, 0))) if bsz % SUBLANES else c

    sizes = [SSD_WIDTH, SSD_CONV_DIM, SSD_HEADS, GLA_K_WIDTH, GLA_K_WIDTH, GLA_V_WIDTH,
             GLA_GATE_RANK, GLA_V_WIDTH]
    o = np.concatenate([[0], np.cumsum(sizes)])

    for i in range(depth):
        ssd_consts = (ssd_conv_w[i], ssd_conv_b[i][None, :],
                      _lane_pad(ssd_dt_bias[i][None, :], SMALL_W), _lane_pad(ssd_a_log[i][None, :], SMALL_W),
                      jnp.repeat(ssd_d[i], SSD_HEAD_DIM)[None, :], ssd_norm[i][None, :], tri_q3)
        gw = jnp.zeros((SMALL_W, GLA_K_WIDTH), F32).at[SSD_HEADS:SSD_HEADS + GLA_GATE_RANK].set(
            gla_gate_w[i]).astype(BF16)
        gla_consts = (gw, gla_gate_b[i][None, :], gla_norm[i][None, :], tri_c3)
        y_ssd, y_gla, mod = _mixer_call(x, c_pad, w_ada[i], b_ada[i][None, :], norm_mix_pre[i][None, :],
                                        jnp.swapaxes(w_in[i], 0, 1), o, ssd_consts, gla_consts)

        x = _ffn_call(y_ssd, y_gla, x, mod, norm_mix_post[i][None, :], norm_ffn_pre[i][None, :],
                      w_out[i], ffn_up[i], ffn_conv_w[i], ffn_conv_b[i][None, :],
                      norm_ffn_post[i][None, :], ffn_down[i])
    return x
```

```python
import functools

import numpy as np
import jax
import jax.numpy as jnp
from jax import lax
from jax.experimental import pallas as pl
from jax.experimental.pallas import tpu as pltpu

F32 = jnp.float32
BF16 = jnp.bfloat16

D_MODEL = 1024
MIX_WIDTH = 2 * D_MODEL
SSD_WIDTH = MIX_WIDTH // 2
GLA_V_WIDTH = MIX_WIDTH - SSD_WIDTH
SSD_HEAD_DIM = 64
SSD_HEADS = SSD_WIDTH // SSD_HEAD_DIM
SSD_GROUPS = 2
SSD_STATE = 128
SSD_CONV = 4
SSD_CHUNK = 128
SSD_CONV_DIM = SSD_WIDTH + 2 * SSD_GROUPS * SSD_STATE
GLA_HEADS = 4
GLA_K_WIDTH = GLA_V_WIDTH // 2
GLA_HEAD_K = GLA_K_WIDTH // GLA_HEADS
GLA_HEAD_V = GLA_V_WIDTH // GLA_HEADS
GLA_GATE_RANK = 16
GLA_GATE_NORM = 16.0
GLA_CHUNK = 64
FFN_HIDDEN = int(round(8 * D_MODEL / 3 / 128)) * 128
FFN_CONV = 3
ADA_CHUNKS = 6
NORM_EPS = 1e-6
_LOG2E = float(np.log2(np.e))

LANES = 128
SUBLANES = 8
VMEM_LIMIT_BYTES = 58 * 1024 * 1024
MIXER_VMEM_LIMIT_BYTES = 52 * 1024 * 1024

SMALL_W = LANES
GROUP_W = SSD_WIDTH // SSD_GROUPS
HEADS_PER_GROUP = SSD_HEADS // SSD_GROUPS
HALO = SUBLANES


def _silu(x):
    return x * jax.nn.sigmoid(x)


def _softplus(x):
    return jnp.maximum(x, 0.0) + jnp.log(1.0 + jnp.exp(-jnp.abs(x)))


def _rms_scale(x):
    return lax.rsqrt(jnp.mean(x * x, axis=-1, keepdims=True) + NORM_EPS)


def _split3(x):
    hi = x.astype(BF16)
    r1 = x - hi.astype(F32)
    mid = r1.astype(BF16)
    lo = (r1 - mid.astype(F32)).astype(BF16)
    return hi, mid, lo


def _dot(a, b):
    return jnp.dot(a, b, preferred_element_type=F32)


def _dot_nt(a, b):
    return lax.dot_general(a, b, (((1,), (1,)), ((), ())), preferred_element_type=F32)


def _dot_tn(a, b):
    return lax.dot_general(a, b, (((0,), (0,)), ((), ())), preferred_element_type=F32)


def _resident(shape):
    nd = len(shape)
    return pl.BlockSpec(shape, lambda *_: (0,) * nd, pipeline_mode=pl.Buffered(1))


def _params(*sem, vmem_limit_bytes=VMEM_LIMIT_BYTES):
    return pltpu.CompilerParams(dimension_semantics=sem, vmem_limit_bytes=vmem_limit_bytes)


def _ada_mod(c_ref, b_ref, w_hbm, mod_ref, stage, sem):
    d = c_ref.shape[1]
    ca = _silu(c_ref[...]).astype(BF16)

    def copy(k):
        return pltpu.make_async_copy(w_hbm.at[:, pl.ds(k * d, d)], stage.at[k % 2], sem.at[k % 2])

    copy(0).start()
    for k in range(ADA_CHUNKS):
        if k + 1 < ADA_CHUNKS:
            copy(k + 1).start()
        yield
        copy(k).wait()
        mod_ref[k] = _dot(ca, stage[k % 2].astype(BF16)) + b_ref[:, k * d:(k + 1) * d]


def _mod_row(mod_ref, k, b):
    return mod_ref[k, pl.ds(b, 1), :]


_Z0 = 0
_XBC0 = _Z0 + SSD_WIDTH
_Q0 = _XBC0 + SSD_CONV_DIM
_K0 = _Q0 + GLA_K_WIDTH
_V0 = _K0 + GLA_K_WIDTH
_GO0 = _V0 + GLA_V_WIDTH
_PROJ_W = _GO0 + GLA_V_WIDTH
_PROJ_CHUNK = 512


def _load_projection(wt_hbm, wbig_ref, wsmall_ref, stage, small_stage, sem, offs):
    d = wt_hbm.shape[1]
    skip_dt = offs[3] - offs[2]
    skip_glr = offs[7] - offs[6]
    n = _PROJ_W // _PROJ_CHUNK

    def copy(j):
        dst = j * _PROJ_CHUNK
        src = dst + (skip_dt if dst >= offs[2] else 0) + (skip_glr if dst >= offs[6] - skip_dt else 0)
        return pltpu.make_async_copy(wt_hbm.at[pl.ds(src, _PROJ_CHUNK), :], stage.at[j % 2], sem.at[j % 2])

    small_copies = [
        pltpu.make_async_copy(wt_hbm.at[pl.ds(offs[2], skip_dt), :], small_stage.at[pl.ds(0, skip_dt), :], sem.at[2]),
        pltpu.make_async_copy(wt_hbm.at[pl.ds(offs[6], skip_glr), :], small_stage.at[pl.ds(skip_dt, skip_glr), :],
                              sem.at[3])]
    for cp in small_copies:
        cp.start()
    copy(0).start()
    for j in range(n):
        if j + 1 < n:
            copy(j + 1).start()
        yield
        copy(j).wait()
        wbig_ref[:, j * _PROJ_CHUNK:(j + 1) * _PROJ_CHUNK] = stage[j % 2].T.astype(BF16)
    for cp in small_copies:
        cp.wait()
    pad = jnp.zeros((SMALL_W - skip_dt - skip_glr, d), F32)
    wsmall_ref[...] = jnp.concatenate([small_stage[...], pad], axis=0).T.astype(BF16)


def _project(hb, w_ref, c0, width):
    parts = [_dot(hb, w_ref[:, c:c + min(_PROJ_CHUNK, c0 + width - c)])
             for c in range(c0, c0 + width, _PROJ_CHUNK)]
    return parts[0] if len(parts) == 1 else jnp.concatenate(parts, axis=1)


def _ssd_stages(env, cw_ref, cb_ref, dtb_ref, alog_ref, dexp_ref, nw_ref, tri_ref, xpad, state, y_ref):
    q = SSD_CHUNK
    n = SSD_STATE

    xpad[HALO:HALO + q, :] = env["xbc"]
    ext = xpad[...]
    acc = cw_ref[0:1, :] * ext
    for k in range(1, SSD_CONV):
        acc = pltpu.roll(acc, 1, 0) + cw_ref[k:k + 1, :] * ext
    acc = acc[HALO:HALO + q, :] + cb_ref[...]
    xpad[0:HALO, :] = xpad[q:q + HALO, :]
    xc = _silu(acc)
    xs = xc[:, :SSD_WIDTH]
    bm = xc[:, SSD_WIDTH:SSD_WIDTH + SSD_GROUPS * n].astype(BF16)
    cm = xc[:, SSD_WIDTH + SSD_GROUPS * n:].astype(BF16)
    yield

    lane = lax.broadcasted_iota(jnp.int32, (q, SMALL_W), 1)
    head_lane = lane < SSD_HEADS
    dt = _softplus(env["small"] + dtb_ref[...])
    a = -jnp.exp(alog_ref[...]) * _LOG2E
    dt = jnp.where(head_lane, dt, 0.0)
    da = jnp.where(head_lane, dt * a, 0.0)
    cs = _dot(tri_ref[...], jnp.concatenate(_split3(da), axis=0))
    cs_t = cs.T
    half = (lane >= SSD_HEAD_DIM).astype(jnp.int32)

    def expand(v):
        return jnp.concatenate([jnp.take_along_axis(v, 2 * j + half, axis=1)
                                for j in range(SSD_WIDTH // LANES)], axis=1)

    cs_e = expand(cs)
    dt_e = expand(dt)
    cs_last = cs_e[q - 1:q, :]

    xdt = xs * dt_e
    xdt_b = xdt.astype(BF16)
    xdec_b = (xdt * jnp.exp2(cs_last - cs_e)).astype(BF16)
    decay_in = jnp.exp2(cs_e)
    decay_chunk = jnp.exp2(cs_last)
    yield

    row = lax.broadcasted_iota(jnp.int32, (q, q), 0)
    col = lax.broadcasted_iota(jnp.int32, (q, q), 1)
    causal = row >= col
    first_half = col < SSD_HEAD_DIM

    y_parts = []
    for g in range(SSD_GROUPS):
        cmg = cm[:, g * n:(g + 1) * n]
        bmg = bm[:, g * n:(g + 1) * n]
        gs = slice(g * GROUP_W, (g + 1) * GROUP_W)
        scores = _dot_nt(cmg, bmg)
        y_off = _dot(cmg, state[g].astype(BF16)) * decay_in[:, gs]
        state[g] = decay_chunk[:, gs] * state[g] + _dot_tn(bmg, xdec_b[:, gs])
        diag = []
        for pair in range(HEADS_PER_GROUP // 2):
            h0 = g * HEADS_PER_GROUP + 2 * pair
            ms = []
            for h in (h0, h0 + 1):
                seg = cs[:, h:h + 1] - cs_t[h:h + 1, :]
                decay = jnp.exp2(jnp.where(causal, seg, -jnp.inf))
                ms.append((scores * decay).astype(BF16))
            slab = xdt_b[:, h0 * SSD_HEAD_DIM:(h0 + 2) * SSD_HEAD_DIM]
            zero = jnp.zeros_like(slab)
            rhs = jnp.concatenate([jnp.where(first_half, slab, zero),
                                   jnp.where(first_half, zero, slab)], axis=0)
            diag.append(_dot(jnp.concatenate(ms, axis=1), rhs))
        y_parts.append(jnp.concatenate(diag, axis=1) + y_off)
        yield
    y = jnp.concatenate(y_parts, axis=1) + xs * dexp_ref[...]
    y = y * _silu(env["z"])
    outs = []
    for g in range(SSD_GROUPS):
        yg = y[:, g * GROUP_W:(g + 1) * GROUP_W]
        outs.append(yg * _rms_scale(yg))
    y_ref[...] = (jnp.concatenate(outs, axis=1) * nw_ref[...]).astype(BF16)
    yield


def _gla_stages(env, r, fresh, gw_ref, gb_ref, nw_ref, tri_ref, state, y_ref):
    c = GLA_CHUNK
    dk, dv = GLA_HEAD_K, GLA_HEAD_V
    row = lax.broadcasted_iota(jnp.int32, (c, c), 0)
    col = lax.broadcasted_iota(jnp.int32, (c, c), 1)
    causal = row >= col

    qf = env["q"][r, :] * (dk ** -0.5)
    kf = env["k"][r, :]
    logit = _dot(env["small"][r, :].astype(BF16), gw_ref[...]) + gb_ref[...]
    lg = -_softplus(-logit) * (_LOG2E / GLA_GATE_NORM)
    bcum = _dot(tri_ref[...], jnp.concatenate(_split3(lg), axis=0))
    blast = bcum[c - 1:c, :]
    qt = (qf * jnp.exp2(bcum)).astype(BF16)
    kt = (kf * jnp.exp2(-bcum)).astype(BF16)
    kd = (kf * jnp.exp2(blast - bcum)).astype(BF16)
    eblast = jnp.exp2(blast)
    yield
    for h in range(GLA_HEADS):
        ks = slice(h * dk, (h + 1) * dk)
        vs = slice(h * dv, (h + 1) * dv)
        attn = jnp.where(causal, _dot_nt(qt[:, ks], kt[:, ks]), 0.0)
        vh = env["v"][r, vs]
        st = jnp.where(fresh, 0.0, state[h])
        o = _dot(attn.astype(BF16), vh) + _dot_nt(qt[:, ks], st.astype(BF16))
        state[h] = eblast[:, ks] * st + _dot_tn(vh, kd[:, ks])
        o = o * _rms_scale(o) * nw_ref[...]
        o = o * _silu(env["go"][r, vs])
        y_ref[:, vs] = o.astype(BF16)
        if h % 2 == 1:
            yield


def _gla_levels(env, rows, gw_ref, gb_ref, nw_ref, tri_ref, state, y_ref):
    c = GLA_CHUNK
    dk, dv = GLA_HEAD_K, GLA_HEAD_V
    row = lax.broadcasted_iota(jnp.int32, (c, c), 0)
    col = lax.broadcasted_iota(jnp.int32, (c, c), 1)
    causal = row >= col
    heads = range(GLA_HEADS)

    logit = _dot(env["small"].astype(BF16), gw_ref[...]) + gb_ref[...]
    lg = -_softplus(-logit) * (_LOG2E / GLA_GATE_NORM)
    yield
    bcum = _dot(tri_ref[...], jnp.concatenate(_split3(lg), axis=0))
    blast = bcum[c - 1:c, :]
    qf = env["q"] * (dk ** -0.5)
    kf = env["k"]
    qt = (qf * jnp.exp2(bcum)).astype(BF16)
    kt = (kf * jnp.exp2(-bcum)).astype(BF16)
    kd = (kf * jnp.exp2(blast - bcum)).astype(BF16)
    eblast = jnp.exp2(blast)
    yield
    ks = [slice(h * dk, (h + 1) * dk) for h in heads]
    vs = [slice(h * dv, (h + 1) * dv) for h in heads]
    attn = [jnp.where(causal, _dot_nt(qt[:, ks[h]], kt[:, ks[h]]), 0.0).astype(BF16) for h in heads]
    yield
    o = [_dot(attn[h], env["v"][:, vs[h]]) + _dot_nt(qt[:, ks[h]], state[h].astype(BF16)) for h in heads]
    for h in heads:
        state[h] = eblast[:, ks[h]] * state[h] + _dot_tn(env["v"][:, vs[h]], kd[:, ks[h]])
    for h in heads:
        oh = o[h] * _rms_scale(o[h]) * nw_ref[...]
        y_ref[rows, vs[h]] = (oh * _silu(env["go"][:, vs[h]])).astype(BF16)
    yield


def _prenorm(x_ref, mod_ref, b, npre_ref, hb_s):
    x = x_ref[...]
    xn = (x * _rms_scale(x)) * npre_ref[...]
    hb_s[...] = (xn * (1.0 + _mod_row(mod_ref, 1, b)) + _mod_row(mod_ref, 0, b)).astype(BF16)


def _mixer_kernel(x0_ref, xn_ref, c_ref, bada_ref, wada_hbm, wt_hbm, npre_ref,
                  cw_ref, cb_ref, dtb_ref, alog_ref, dexp_ref, snw_ref, triq_ref,
                  gw_ref, gb_ref, gnw_ref, tric_ref,
                  ys_ref, yg_ref, mod_ref, xpad, sstate, gstate, hb_s,
                  yg0_s, q_s, k_s, v_s, go_s, small_s,
                  wbig_ref, wsmall_ref, w_stage, wsmall_stage, w_sem, a_stage, a_sem,
                  *, tiles_per_seq, n_tiles, offs):
    i = pl.program_id(0)
    fresh = lax.rem(i, tiles_per_seq) == 0

    @pl.when(i == 0)
    def _():
        loads = [_ada_mod(c_ref, bada_ref, wada_hbm, mod_ref, a_stage, a_sem),
                 _load_projection(wt_hbm, wbig_ref, wsmall_ref, w_stage, wsmall_stage, w_sem, offs)]
        while loads:
            loads = [g for g in loads if next(g, True) is None]
        _prenorm(x0_ref, mod_ref, 0, npre_ref, hb_s)
        gstate[...] = jnp.zeros(gstate.shape, F32)
        for ref in (yg0_s, q_s, k_s, v_s, go_s, small_s):
            ref[...] = jnp.zeros(ref.shape, ref.dtype)

    @pl.when(fresh)
    def _():
        xpad[0:HALO, :] = jnp.zeros((HALO, SSD_CONV_DIM), F32)
        sstate[...] = jnp.zeros(sstate.shape, F32)

    hb = hb_s[...]
    first = slice(0, GLA_CHUNK)
    second = slice(GLA_CHUNK, 2 * GLA_CHUNK)
    xbc_cols = [slice(c, c + _PROJ_CHUNK) for c in range(_XBC0, _XBC0 + SSD_CONV_DIM, _PROJ_CHUNK)]
    prev = {"q": q_s[...], "k": k_s[...], "v": v_s[...], "go": go_s[...], "small": small_s[...]}
    off = pl.multiple_of(lax.rem(i + 1, 2) * SSD_CHUNK, SSD_CHUNK)
    gla1 = _gla_levels(prev, pl.ds(off + GLA_CHUNK, GLA_CHUNK), gw_ref, gb_ref, gnw_ref, tric_ref, gstate, yg_ref)
    yg_ref[pl.ds(off, GLA_CHUNK), :] = yg0_s[...]
    next(gla1)
    env = {"small": _dot(hb, wsmall_ref[...])}
    xbc = [_dot(hb, wbig_ref[:, xbc_cols[0]])]
    next(gla1)
    xbc.append(_dot(hb, wbig_ref[:, xbc_cols[1]]))
    next(gla1)
    xbc.append(_dot(hb, wbig_ref[:, xbc_cols[2]]))
    env["xbc"] = jnp.concatenate(xbc, axis=1)
    ssd = _ssd_stages(env, cw_ref, cb_ref, dtb_ref, alog_ref, dexp_ref, snw_ref, triq_ref,
                      xpad, sstate, ys_ref)
    gla0 = _gla_stages(env, first, fresh, gw_ref, gb_ref, gnw_ref, tric_ref, gstate, yg0_s)
    env["q"] = _project(hb, wbig_ref, _Q0, GLA_K_WIDTH)
    next(gla1)
    env["k"] = _project(hb, wbig_ref, _K0, GLA_K_WIDTH)
    next(ssd)
    env["v"] = _project(hb, wbig_ref, _V0, GLA_V_WIDTH).astype(BF16)
    next(gla0)
    next(ssd)
    env["go"] = _project(hb, wbig_ref, _GO0, GLA_V_WIDTH)
    next(gla0)
    next(ssd)
    env["z"] = _project(hb, wbig_ref, _Z0, SSD_WIDTH)
    b_next = jnp.minimum(i + 1, n_tiles - 1) // tiles_per_seq
    _prenorm(xn_ref, mod_ref, b_next, npre_ref, hb_s)
    next(gla0)
    next(ssd)
    next(ssd)
    q_s[...] = env["q"][second, :]
    k_s[...] = env["k"][second, :]
    v_s[...] = env["v"][second, :]
    go_s[...] = env["go"][second, :]
    small_s[...] = env["small"][second, :]

    @pl.when(i == n_tiles - 1)
    def _():
        last = {"q": q_s[...], "k": k_s[...], "v": v_s[...], "go": go_s[...], "small": small_s[...]}
        yg_ref[pl.ds(SSD_CHUNK, GLA_CHUNK), :] = yg0_s[...]
        for _ in _gla_levels(last, pl.ds(SSD_CHUNK + GLA_CHUNK, GLA_CHUNK), gw_ref, gb_ref, gnw_ref, tric_ref,
                             gstate, yg_ref):
            pass


def _mixer_call(x, c_pad, wada, bada, npre, wt, offs, ssd_consts, gla_consts):
    b, s, d = x.shape
    t = SSD_CHUNK
    assert t == 2 * GLA_CHUNK
    tiles_per_seq = s // t
    n = b * tiles_per_seq
    assert n % 2 == 0
    xf = x.reshape(b * s, d)
    nxt = lambda i: jnp.minimum(i + 1, n - 1)
    done = lambda i: (jnp.maximum(i - 1, 0) // 2, 0)
    consts = (npre,) + tuple(ssd_consts) + tuple(gla_consts)
    offs = tuple(int(v) for v in offs)
    skip = (offs[3] - offs[2]) + (offs[7] - offs[6])
    mod_shape = (ADA_CHUNKS, c_pad.shape[0], d)
    ys, yg, mod = pl.pallas_call(
        functools.partial(_mixer_kernel, tiles_per_seq=tiles_per_seq, n_tiles=n, offs=offs),
        grid=(n,),
        in_specs=[pl.BlockSpec((t, d), lambda i: (0, 0)),
                  pl.BlockSpec((t, d), lambda i: (nxt(i), 0)),
                  _resident(c_pad.shape), _resident(bada.shape),
                  pl.BlockSpec(memory_space=pl.ANY), pl.BlockSpec(memory_space=pl.ANY)]
        + [_resident(a.shape) for a in consts],
        out_specs=[pl.BlockSpec((t, SSD_WIDTH), lambda i: (i, 0)),
                   pl.BlockSpec((2 * t, GLA_V_WIDTH), done),
                   pl.BlockSpec(mod_shape, lambda i: (0, 0, 0))],
        out_shape=[jax.ShapeDtypeStruct((b * s, SSD_WIDTH), BF16),
                   jax.ShapeDtypeStruct((b * s, GLA_V_WIDTH), BF16),
                   jax.ShapeDtypeStruct(mod_shape, F32)],
        scratch_shapes=[
            pltpu.VMEM((HALO + SSD_CHUNK, SSD_CONV_DIM), F32),
            pltpu.VMEM((SSD_GROUPS, SSD_STATE, GROUP_W), F32),
            pltpu.VMEM((GLA_HEADS, GLA_HEAD_V, GLA_HEAD_K), F32),
            pltpu.VMEM((t, d), BF16),
            pltpu.VMEM((GLA_CHUNK, GLA_V_WIDTH), BF16),
            pltpu.VMEM((GLA_CHUNK, GLA_K_WIDTH), F32),
            pltpu.VMEM((GLA_CHUNK, GLA_K_WIDTH), F32),
            pltpu.VMEM((GLA_CHUNK, GLA_V_WIDTH), BF16),
            pltpu.VMEM((GLA_CHUNK, GLA_V_WIDTH), F32),
            pltpu.VMEM((GLA_CHUNK, SMALL_W), F32),
            pltpu.VMEM((d, _PROJ_W), BF16),
            pltpu.VMEM((d, SMALL_W), BF16),
            pltpu.VMEM((2, _PROJ_CHUNK, d), F32),
            pltpu.VMEM((skip, d), F32),
            pltpu.SemaphoreType.DMA((4,)),
            pltpu.VMEM((2, d, d), F32),
            pltpu.SemaphoreType.DMA((2,)),
        ],
        compiler_params=_params("arbitrary", vmem_limit_bytes=MIXER_VMEM_LIMIT_BYTES),
        name="token_mixer",
    )(xf, xf, c_pad, bada, wada, wt, *consts)
    return ys.reshape(b, s, SSD_WIDTH), yg.reshape(b, s, GLA_V_WIDTH), mod


_UP_CHUNK = 384
_UP_DOT = 768
_FFN_TM = 256
_CAST_ROWS = 128


def _cast_weight(w_hbm, dst, stage, sem):
    rows = stage.shape[1]
    n = w_hbm.shape[0] // rows

    def copy(c):
        return pltpu.make_async_copy(w_hbm.at[pl.ds(c * rows, rows), :], stage.at[c % 2], sem.at[c % 2])

    copy(0).start()
    for c in range(n):
        if c + 1 < n:
            copy(c + 1).start()
        yield
        copy(c).wait()
        dst[c * rows:(c + 1) * rows, :] = stage[c % 2].astype(BF16)


def _stage_ffn_input(ys_ref, yg_ref, x_ref, mod_ref, b, npost1_ref, npre_ref, wout_ref, x1_s, hb_s):
    y = _dot(ys_ref[...], wout_ref[0:SSD_WIDTH, :]) + _dot(yg_ref[...], wout_ref[SSD_WIDTH:, :])
    yn = (y * _rms_scale(y)) * npost1_ref[...]
    x1 = x_ref[...] + _mod_row(mod_ref, 2, b) * yn
    x1_s[...] = x1
    h = ((x1 * _rms_scale(x1)) * npre_ref[...]) * (1.0 + _mod_row(mod_ref, 4, b)) + _mod_row(mod_ref, 3, b)
    hb_s[...] = h.astype(BF16)


def _ffn_kernel(ys0_ref, yg0_ref, x0_ref, ysn_ref, ygn_ref, xn_ref, mod_ref,
                npost1_ref, npre_ref, wout_hbm, wup_hbm, cw_ref, cb_ref, npost2_ref, wdown_hbm,
                o_ref, upad, gate, x1_s, hb_s,
                wout_ref, wup_ref, wdown_ref, st_out, st_up, st_down, sem, *, tiles_per_seq, n_tiles):
    tm = o_ref.shape[0]
    i = pl.program_id(0)

    @pl.when(i == 0)
    def _():
        casts = [_cast_weight(wout_hbm, wout_ref, st_out, sem.at[0]),
                 _cast_weight(wup_hbm, wup_ref, st_up, sem.at[1]),
                 _cast_weight(wdown_hbm, wdown_ref, st_down, sem.at[2])]
        while casts:
            casts = [g for g in casts if next(g, True) is None]
        _stage_ffn_input(ys0_ref, yg0_ref, x0_ref, mod_ref, 0, npost1_ref, npre_ref, wout_ref, x1_s, hb_s)

    @pl.when(lax.rem(i, tiles_per_seq) == 0)
    def _():
        upad[0:HALO, :] = jnp.zeros((HALO, upad.shape[1]), F32)

    x1 = x1_s[...]
    hb = hb_s[...]

    done = set()

    def up(col):
        p = col // _UP_DOT
        if p not in done:
            done.add(p)
            cols = slice(p * _UP_DOT, (p + 1) * _UP_DOT)
            upad[HALO:HALO + tm, cols] = _dot(hb, wup_ref[:, cols])

    def conv(c0):
        cols = slice(c0, c0 + _UP_CHUNK)
        up(c0)
        up(c0 + _UP_CHUNK - 1)
        ext = upad[:, cols]
        acc = cw_ref[0:1, cols] * ext
        for k in range(1, FFN_CONV):
            acc = pltpu.roll(acc, 1, 0) + cw_ref[k:k + 1, cols] * ext
        upad[0:HALO, cols] = upad[tm:tm + HALO, cols]
        return acc[HALO:HALO + tm, :] + cb_ref[:, cols]

    for c in range(0, FFN_HIDDEN, _UP_CHUNK):
        gate[:, c:c + _UP_CHUNK] = (_silu(conv(c)) * conv(FFN_HIDDEN + c)).astype(BF16)
    b_next = jnp.minimum(i + 1, n_tiles - 1) // tiles_per_seq
    _stage_ffn_input(ysn_ref, ygn_ref, xn_ref, mod_ref, b_next, npost1_ref, npre_ref, wout_ref, x1_s, hb_s)
    f = _dot(gate[...], wdown_ref[...])
    fn = (f * _rms_scale(f)) * npost2_ref[...]
    o_ref[...] = x1 + _mod_row(mod_ref, 5, i // tiles_per_seq) * fn


def _ffn_call(ys, yg, x, mod, npost1, npre, wout, wup, cw, cb, npost2, wdown):
    b, s, d = x.shape
    tm = _FFN_TM
    tiles_per_seq = s // tm
    n = b * tiles_per_seq
    flat = lambda a: a.reshape(b * s, a.shape[-1])
    nxt = lambda i: jnp.minimum(i + 1, n - 1)
    first = lambda w: pl.BlockSpec((tm, w), lambda i: (0, 0))
    ahead = lambda w: pl.BlockSpec((tm, w), lambda i: (nxt(i), 0))
    consts = (npost1, npre, wout, wup, cw, cb, npost2, wdown)
    in_hbm = pl.BlockSpec(memory_space=pl.ANY)
    out = pl.pallas_call(
        functools.partial(_ffn_kernel, tiles_per_seq=tiles_per_seq, n_tiles=n),
        grid=(n,),
        in_specs=[first(SSD_WIDTH), first(GLA_V_WIDTH), first(d),
                  ahead(SSD_WIDTH), ahead(GLA_V_WIDTH), ahead(d), _resident(mod.shape)]
        + [in_hbm if any(a is w for w in (wout, wup, wdown)) else _resident(a.shape) for a in consts],
        out_specs=pl.BlockSpec((tm, d), lambda i: (i, 0)),
        out_shape=jax.ShapeDtypeStruct((b * s, d), F32),
        scratch_shapes=[pltpu.VMEM((HALO + tm, 2 * FFN_HIDDEN), F32),
                        pltpu.VMEM((tm, FFN_HIDDEN), BF16),
                        pltpu.VMEM((tm, d), F32), pltpu.VMEM((tm, d), BF16),
                        pltpu.VMEM(wout.shape, BF16), pltpu.VMEM(wup.shape, BF16), pltpu.VMEM(wdown.shape, BF16),
                        pltpu.VMEM((2, _CAST_ROWS, wout.shape[1]), F32),
                        pltpu.VMEM((2, _CAST_ROWS, wup.shape[1]), F32),
                        pltpu.VMEM((2, _CAST_ROWS, wdown.shape[1]), F32),
                        pltpu.SemaphoreType.DMA((3, 2))],
        compiler_params=_params("arbitrary"),
        name="channel_mixer",
    )(flat(ys), flat(yg), flat(x), flat(ys), flat(yg), flat(x), mod, *consts)
    return out.reshape(b, s, d)


def _lane_pad(v, width):
    return jnp.pad(v, ((0, 0), (0, width - v.shape[1])))


def _constants():
    q, c = SSD_CHUNK, GLA_CHUNK
    tri_q = np.tril(np.ones((q, q), np.float32))
    tri_c = np.tril(np.ones((c, c), np.float32))
    return (jnp.asarray(np.tile(tri_q, (1, 3)), BF16),
            jnp.asarray(np.tile(tri_c, (1, 3)), BF16))


def kernel(x, c, w_ada, b_ada, norm_mix_pre, norm_mix_post, norm_ffn_pre, norm_ffn_post, w_in, ssd_conv_w, ssd_conv_b, ssd_dt_bias, ssd_a_log, ssd_d, ssd_norm, gla_gate_w, gla_gate_b, gla_norm, w_out, ffn_up, ffn_conv_w, ffn_conv_b, ffn_down):
    bsz, seqlen, d = x.shape
    depth = w_ada.shape[0]
    tri_q3, tri_c3 = _constants()
    c_pad = jnp.pad(c, ((0, SUBLANES - bsz % SUBLANES), (0, 0))) if bsz % SUBLANES else c

    sizes = [SSD_WIDTH, SSD_CONV_DIM, SSD_HEADS, GLA_K_WIDTH, GLA_K_WIDTH, GLA_V_WIDTH,
             GLA_GATE_RANK, GLA_V_WIDTH]
    o = np.concatenate([[0], np.cumsum(sizes)])

    for i in range(depth):
        ssd_consts = (ssd_conv_w[i], ssd_conv_b[i][None, :],
                      _lane_pad(ssd_dt_bias[i][None, :], SMALL_W), _lane_pad(ssd_a_log[i][None, :], SMALL_W),
                      jnp.repeat(ssd_d[i], SSD_HEAD_DIM)[None, :], ssd_norm[i][None, :], tri_q3)
        gw = jnp.zeros((SMALL_W, GLA_K_WIDTH), F32).at[SSD_HEADS:SSD_HEADS + GLA_GATE_RANK].set(
            gla_gate_w[i]).astype(BF16)
        gla_consts = (gw, gla_gate_b[i][None, :], gla_norm[i][None, :], tri_c3)
        y_ssd, y_gla, mod = _mixer_call(x, c_pad, w_ada[i], b_ada[i][None, :], norm_mix_pre[i][None, :],
                                        jnp.swapaxes(w_in[i], 0, 1), o, ssd_consts, gla_consts)

        x = _ffn_call(y_ssd, y_gla, x, mod, norm_mix_post[i][None, :], norm_ffn_pre[i][None, :],
                      w_out[i], ffn_up[i], ffn_conv_w[i], ffn_conv_b[i][None, :],
                      norm_ffn_post[i][None, :], ffn_down[i])
    return x
```
